```python
import math
import jax, jax.numpy as jnp
from jax import lax
import numpy as np

D_MODEL = 1024
BATCH = 8
SEQ = 8192
DEPTH = 4

HEAD_DIM = 64
DIFF_HEADS = 4
NSA_Q_HEADS = 8
NSA_KV_HEADS = 2
NSA_GROUP = NSA_Q_HEADS // NSA_KV_HEADS
CMP_BLOCK = 32
CMP_STRIDE = 16
CMP_HIDDEN = 256
SEL_BLOCK = 64
SEL_TOP = 16
SEL_FORCED_SCORE = 1.0e4
WINDOW = 512
Q_BLOCK = 128
CONV_WIDTH = 31
N_EXPERTS = 32
TOP_K = 4
D_EXPERT = D_MODEL
SWIGLU_ALPHA = 1.702
SWIGLU_LIMIT = 7.0
MOE_BLOCK = 256
NORM_EPS = 1e-6

DIFF_QK = DIFF_HEADS * 2 * HEAD_DIM
DIFF_V = DIFF_HEADS * 2 * HEAD_DIM
NSA_Q = NSA_Q_HEADS * HEAD_DIM
NSA_KV = NSA_KV_HEADS * HEAD_DIM
NSA_GATES = NSA_Q_HEADS * 3
IN_SPLITS = (DIFF_QK, DIFF_QK, DIFF_V, NSA_Q, 6 * NSA_KV, NSA_GATES)
W_IN = sum(IN_SPLITS)
ATTN_OUT = DIFF_V + NSA_Q

kernel_name = "hybrid_diffattn_nsa_conformer_moe_trunk"


def rms_norm(x, g):
    xf = x.astype(jnp.float32)
    y = xf * lax.rsqrt(jnp.mean(xf * xf, axis=-1, keepdims=True) + NORM_EPS)
    return (y * g.astype(jnp.float32)).astype(x.dtype)


def split_cols(z, sizes):
    return jnp.split(z, np.cumsum(sizes)[:-1].tolist(), axis=-1)


def masked_softmax(s, mask):
    s = jnp.where(mask, s, -jnp.inf)
    m = jnp.max(s, axis=-1, keepdims=True)
    m = jnp.where(jnp.isfinite(m), m, 0.0)
    e = jnp.exp(s - m)
    return e / jnp.maximum(jnp.sum(e, axis=-1, keepdims=True), 1e-30)


def over_query_blocks(fn, seq_len):
    starts = jnp.arange(seq_len // Q_BLOCK, dtype=jnp.int32) * Q_BLOCK
    out = lax.map(fn, starts)
    out = jnp.moveaxis(out, 0, 1)
    return out.reshape((out.shape[0], seq_len) + out.shape[3:])


def diff_attention(q, k, v, qk_gain, lam_vecs, subln_g, lam_init):
    B, S = q.shape[:2]
    q = rms_norm(q, qk_gain[0])
    k = rms_norm(k, qk_gain[1])
    lv = lam_vecs.astype(jnp.float32)
    lam = jnp.exp(jnp.sum(lv[0] * lv[1])) - jnp.exp(jnp.sum(lv[2] * lv[3])) + lam_init
    scale = HEAD_DIM ** -0.5
    kpos = jnp.arange(S)

    def block(qs):
        qb = lax.dynamic_slice_in_dim(q, qs, Q_BLOCK, axis=1)
        s = jnp.einsum('bqhmd,bkhmd->bhmqk', qb, k, preferred_element_type=jnp.float32) * scale
        t = qs + jnp.arange(Q_BLOCK)
        p = masked_softmax(s, kpos[None, :] <= t[:, None])
        a = p[:, :, 0] - lam * p[:, :, 1]
        return jnp.einsum('bhqk,bkhe->bqhe', a.astype(v.dtype), v)

    o = over_query_blocks(block, S)
    o = rms_norm(o, subln_g) * (1.0 - lam_init)
    return o.reshape(B, S, DIFF_HEADS * 2 * HEAD_DIM)


def nsa_attention(q, k_cmp, v_cmp, k_sel, v_sel, k_win, v_win, gates,
                  q_gain, k_gain, cmp_pos, cmp_w1, cmp_w2):
    B, S = q.shape[:2]
    q = rms_norm(q, q_gain)
    scale = HEAD_DIM ** -0.5

    n_cmp = (S - CMP_BLOCK) // CMP_STRIDE + 1
    blk_idx = np.arange(n_cmp)[:, None] * CMP_STRIDE + np.arange(CMP_BLOCK)[None, :]

    def compress(z, j):
        zb = z[:, blk_idx] + cmp_pos[j][None, None, :, None, :]
        zb = jnp.moveaxis(zb, 3, 2).reshape(B, n_cmp, NSA_KV_HEADS, CMP_BLOCK * HEAD_DIM)
        return jax.nn.gelu(zb @ cmp_w1[j]) @ cmp_w2[j]

    kc = rms_norm(compress(k_cmp, 0), k_gain[0])
    vc = compress(v_cmp, 1)
    cmp_end = jnp.asarray(blk_idx[:, -1])

    n_sel = S // SEL_BLOCK
    n_top = min(SEL_TOP, n_sel)
    ks = rms_norm(k_sel, k_gain[1]).reshape(B, n_sel, SEL_BLOCK, NSA_KV_HEADS, HEAD_DIM).transpose(0, 3, 1, 2, 4)
    vs = v_sel.reshape(B, n_sel, SEL_BLOCK, NSA_KV_HEADS, HEAD_DIM).transpose(0, 3, 1, 2, 4)
    c_start = blk_idx[:, 0]
    s_start = np.arange(n_sel) * SEL_BLOCK
    overlap = jnp.asarray(((c_start[:, None] < s_start[None, :] + SEL_BLOCK)
                           & (c_start[:, None] + CMP_BLOCK > s_start[None, :])).astype(np.float32))

    pad = ((0, 0), (WINDOW, 0), (0, 0), (0, 0))
    kw = jnp.pad(rms_norm(k_win, k_gain[2]), pad)
    vw = jnp.pad(v_win, pad)

    qg = q.reshape(B, S, NSA_KV_HEADS, NSA_GROUP, HEAD_DIM)
    bi = jnp.arange(B)[:, None, None, None]
    hi = jnp.arange(NSA_KV_HEADS)[None, :, None, None]
    sel_off = jnp.arange(SEL_BLOCK)
    win_off = jnp.arange(Q_BLOCK + WINDOW) - WINDOW
    jsel = jnp.arange(n_sel)

    def block(qs):
        qb = lax.dynamic_slice_in_dim(qg, qs, Q_BLOCK, axis=1)
        t = qs + jnp.arange(Q_BLOCK)
        s_c = jnp.einsum('bqkgd,bckd->bkgqc', qb, kc, preferred_element_type=jnp.float32) * scale
        p_c = masked_softmax(s_c, cmp_end[None, :] <= t[:, None])
        o_c = jnp.einsum('bkgqc,bckd->bqkgd', p_c.astype(vc.dtype), vc)
        imp = jnp.einsum('bkgqc,cn->bkqn', p_c, overlap)
        cur = (t // SEL_BLOCK)[:, None]
        forced = (jsel[None, :] == 0) | (jsel[None, :] == cur) | (jsel[None, :] == cur - 1)
        score = jnp.where(forced, SEL_FORCED_SCORE, jnp.where(jsel[None, :] <= cur, imp, -jnp.inf))
        _, top = lax.top_k(score, n_top)
        kg = ks[bi, hi, top]
        vg = vs[bi, hi, top]
        s_s = jnp.einsum('bqkgd,bkqnld->bkgqnl', qb, kg, preferred_element_type=jnp.float32) * scale
        pos = top[..., None] * SEL_BLOCK + sel_off
        m_s = (pos <= t[None, None, :, None, None]).reshape(B, NSA_KV_HEADS, 1, Q_BLOCK, n_top * SEL_BLOCK)
        p_s = masked_softmax(s_s.reshape(B, NSA_KV_HEADS, NSA_GROUP, Q_BLOCK, n_top * SEL_BLOCK), m_s)
        p_s = p_s.reshape(B, NSA_KV_HEADS, NSA_GROUP, Q_BLOCK, n_top, SEL_BLOCK)
        o_s = jnp.einsum('bkgqnl,bkqnld->bqkgd', p_s.astype(vg.dtype), vg)
        kwb = lax.dynamic_slice_in_dim(kw, qs, Q_BLOCK + WINDOW, axis=1)
        vwb = lax.dynamic_slice_in_dim(vw, qs, Q_BLOCK + WINDOW, axis=1)
        kpos = qs + win_off
        dist = t[:, None] - kpos[None, :]
        m_w = (kpos[None, :] >= 0) & (dist >= 0) & (dist < WINDOW)
        s_w = jnp.einsum('bqkgd,bskd->bkgqs', qb, kwb, preferred_element_type=jnp.float32) * scale
        p_w = masked_softmax(s_w, m_w)
        o_w = jnp.einsum('bkgqs,bskd->bqkgd', p_w.astype(vwb.dtype), vwb)
        return jnp.stack([o_c, o_s, o_w], axis=-1)

    o = over_query_blocks(block, S)
    g = jax.nn.sigmoid(gates.astype(jnp.float32)).reshape(B, S, NSA_KV_HEADS, NSA_GROUP, 1, 3)
    out = jnp.sum(o * g.astype(o.dtype), axis=-1)
    return out.reshape(B, S, NSA_Q)


def attention_mixer(h, w_in, w_out, diff_qk_gain, diff_lambda, diff_subln,
                    nsa_q_gain, nsa_k_gain, cmp_pos, cmp_w1, cmp_w2, lam_init):
    B, S, _ = h.shape
    z = h @ w_in
    dq, dk, dv, nq, nkv, ng = split_cols(z, IN_SPLITS)
    o_diff = diff_attention(dq.reshape(B, S, DIFF_HEADS, 2, HEAD_DIM),
                            dk.reshape(B, S, DIFF_HEADS, 2, HEAD_DIM),
                            dv.reshape(B, S, DIFF_HEADS, 2 * HEAD_DIM),
                            diff_qk_gain, diff_lambda, diff_subln, lam_init)
    kv = nkv.reshape(B, S, 6, NSA_KV_HEADS, HEAD_DIM)
    o_nsa = nsa_attention(nq.reshape(B, S, NSA_Q_HEADS, HEAD_DIM),
                          kv[:, :, 0], kv[:, :, 1], kv[:, :, 2], kv[:, :, 3], kv[:, :, 4], kv[:, :, 5],
                          ng.reshape(B, S, NSA_Q_HEADS, 3),
                          nsa_q_gain, nsa_k_gain, cmp_pos, cmp_w1, cmp_w2)
    return jnp.concatenate([o_diff, o_nsa], axis=-1) @ w_out


def conv_module(h, pw1_w, pw1_b, dw_w, dw_b, ln_g, ln_b, pw2_w, pw2_b):
    u = h @ pw1_w + pw1_b
    a, b = jnp.split(u, 2, axis=-1)
    u = a * jax.nn.sigmoid(b)
    u = lax.conv_general_dilated(u, dw_w, window_strides=(1,), padding=[(CONV_WIDTH - 1, 0)],
                                 dimension_numbers=("NWC", "WIO", "NWC"),
                                 feature_group_count=u.shape[-1]) + dw_b
    uf = u.astype(jnp.float32)
    mu = jnp.mean(uf, axis=-1, keepdims=True)
    var = jnp.mean(jnp.square(uf - mu), axis=-1, keepdims=True)
    uf = (uf - mu) * lax.rsqrt(var + NORM_EPS) * ln_g.astype(jnp.float32) + ln_b.astype(jnp.float32)
    u = jax.nn.silu(uf).astype(h.dtype)
    return u @ pw2_w + pw2_b


def clamped_swiglu(u):
    glu, lin = u[..., ::2], u[..., 1::2]
    glu = jnp.minimum(glu, SWIGLU_LIMIT)
    lin = jnp.clip(lin, -SWIGLU_LIMIT, SWIGLU_LIMIT)
    return glu * jax.nn.sigmoid(SWIGLU_ALPHA * glu) * (lin + 1.0)


def moe_ffn(h, router_w, router_b, w1, b1, w2, b2):
    B, S, D = h.shape
    xt = h.reshape(-1, D)
    T = xt.shape[0]
    logits = (xt @ router_w + router_b).astype(jnp.float32)
    top_val, top_idx = lax.top_k(logits, TOP_K)
    weights = jax.nn.softmax(top_val, axis=-1)
    n_assign = T * TOP_K
    flat_e = top_idx.reshape(-1)
    order = jnp.argsort(flat_e)
    sorted_e = flat_e[order]
    counts = jnp.bincount(flat_e, length=N_EXPERTS)
    padded = (counts + MOE_BLOCK - 1) // MOE_BLOCK * MOE_BLOCK
    start = jnp.cumsum(counts) - counts
    pad_end = jnp.cumsum(padded)
    pad_start = pad_end - padded
    dest = pad_start[sorted_e] + jnp.arange(n_assign, dtype=jnp.int32) - start[sorted_e]
    n_blocks = -(-n_assign // MOE_BLOCK) + N_EXPERTS
    n_slots = n_blocks * MOE_BLOCK
    slot_token = jnp.zeros((n_slots,), jnp.int32).at[dest].set((order // TOP_K).astype(jnp.int32))
    slot_w = jnp.zeros((n_slots,), jnp.float32).at[dest].set(weights.reshape(-1)[order])
    block_expert = jnp.minimum(
        jnp.searchsorted(pad_end, jnp.arange(n_blocks, dtype=jnp.int32) * MOE_BLOCK, side='right'),
        N_EXPERTS - 1)
    xs = xt[slot_token].reshape(n_blocks, MOE_BLOCK, D)

    def expert_block(args):
        xb, wb, e = args
        u = clamped_swiglu(xb @ w1[e] + b1[e])
        return (u @ w2[e] + b2[e]) * wb.astype(xb.dtype)[:, None]

    ys = lax.map(expert_block, (xs, slot_w.reshape(n_blocks, MOE_BLOCK), block_expert))
    out = jnp.zeros((T, D), ys.dtype).at[slot_token].add(ys.reshape(-1, D))
    return out.reshape(B, S, D)


def setup_inputs(seed: int = 0) -> dict:
    key = jax.random.key(seed)
    keys = jax.random.split(key, 40)
    counter = [0]
    n_even = (DEPTH + 1) // 2
    n_odd = DEPTH // 2
    D, E, F = D_MODEL, N_EXPERTS, D_EXPERT

    def nrm(shape, s):
        k = keys[counter[0]]
        counter[0] += 1
        return s * jax.random.normal(k, shape, jnp.float32)

    def gain(shape):
        return 1.0 + nrm(shape, 0.05)

    return {
        "x": nrm((BATCH, SEQ, D), 1.0),
        "c": nrm((BATCH, D), 1.0),
        "mod_w": nrm((DEPTH, D, 6 * D), 0.3 * D ** -0.5),
        "mod_b": nrm((DEPTH, 6 * D), 0.02),
        "norm_mix": gain((DEPTH, D)),
        "norm_ffn": gain((DEPTH, D)),
        "attn_w_in": nrm((n_even, D, W_IN), D ** -0.5),
        "attn_w_out": nrm((n_even, ATTN_OUT, D), ATTN_OUT ** -0.5),
        "diff_qk_gain": gain((n_even, 2, HEAD_DIM)),
        "diff_lambda": nrm((n_even, 4, HEAD_DIM), 0.1),
        "diff_subln": gain((n_even, 2 * HEAD_DIM)),
        "nsa_q_gain": gain((n_even, HEAD_DIM)),
        "nsa_k_gain": gain((n_even, 3, HEAD_DIM)),
        "nsa_cmp_pos": nrm((n_even, 2, CMP_BLOCK, HEAD_DIM), 0.1),
        "nsa_cmp_w1": nrm((n_even, 2, CMP_BLOCK * HEAD_DIM, CMP_HIDDEN), (CMP_BLOCK * HEAD_DIM) ** -0.5),
        "nsa_cmp_w2": nrm((n_even, 2, CMP_HIDDEN, HEAD_DIM), CMP_HIDDEN ** -0.5),
        "conv_pw1_w": nrm((n_odd, D, 2 * D), D ** -0.5),
        "conv_pw1_b": nrm((n_odd, 2 * D), 0.02),
        "conv_dw_w": nrm((n_odd, CONV_WIDTH, 1, D), CONV_WIDTH ** -0.5),
        "conv_dw_b": nrm((n_odd, D), 0.02),
        "conv_ln_g": gain((n_odd, D)),
        "conv_ln_b": nrm((n_odd, D), 0.02),
        "conv_pw2_w": nrm((n_odd, D, D), D ** -0.5),
        "conv_pw2_b": nrm((n_odd, D), 0.02),
        "router_w": nrm((DEPTH, D, E), D ** -0.5),
        "router_b": nrm((DEPTH, E), 0.01),
        "moe_w1": nrm((DEPTH, E, D, 2 * F), D ** -0.5),
        "moe_b1": nrm((DEPTH, E, 2 * F), 0.02),
        "moe_w2": nrm((DEPTH, E, F, D), F ** -0.5),
        "moe_b2": nrm((DEPTH, E, D), 0.02),
    }


def reference(x, c, mod_w, mod_b, norm_mix, norm_ffn, attn_w_in, attn_w_out,
              diff_qk_gain, diff_lambda, diff_subln, nsa_q_gain, nsa_k_gain,
              nsa_cmp_pos, nsa_cmp_w1, nsa_cmp_w2,
              conv_pw1_w, conv_pw1_b, conv_dw_w, conv_dw_b, conv_ln_g, conv_ln_b,
              conv_pw2_w, conv_pw2_b,
              router_w, router_b, moe_w1, moe_b1, moe_w2, moe_b2):
    cond = jax.nn.silu(c)
    for i in range(DEPTH):
        mod = (cond @ mod_w[i] + mod_b[i])[:, None, :]
        sh1, sc1, g1, sh2, sc2, g2 = jnp.split(mod, 6, axis=-1)
        h = rms_norm(x, norm_mix[i]) * (1.0 + sc1) + sh1
        j = i // 2
        if i % 2 == 0:
            lam_init = 0.8 - 0.6 * math.exp(-0.3 * i)
            y = attention_mixer(h, attn_w_in[j], attn_w_out[j], diff_qk_gain[j], diff_lambda[j],
                                diff_subln[j], nsa_q_gain[j], nsa_k_gain[j],
                                nsa_cmp_pos[j], nsa_cmp_w1[j], nsa_cmp_w2[j], lam_init)
        else:
            y = conv_module(h, conv_pw1_w[j], conv_pw1_b[j], conv_dw_w[j], conv_dw_b[j],
                            conv_ln_g[j], conv_ln_b[j], conv_pw2_w[j], conv_pw2_b[j])
        x = x + g1 * y
        h = rms_norm(x, norm_ffn[i]) * (1.0 + sc2) + sh2
        x = x + g2 * moe_ffn(h, router_w[i], router_b[i], moe_w1[i], moe_b1[i], moe_w2[i], moe_b2[i])
    return x
```

```python
import functools
import math

import jax
import jax.numpy as jnp
import numpy as np
from jax import lax
from jax.experimental import pallas as pl
from jax.experimental.pallas import tpu as pltpu

F32 = jnp.float32
BF16 = jnp.bfloat16
I32 = jnp.int32

D_MODEL = 1024
HEAD_DIM = 64
DIFF_HEADS = 4
NSA_Q_HEADS = 8
NSA_KV_HEADS = 2
NSA_GROUP = NSA_Q_HEADS // NSA_KV_HEADS
CMP_BLOCK = 32
CMP_STRIDE = 16
CMP_HIDDEN = 256
SEL_BLOCK = 64
SEL_TOP = 16
SEL_FORCED_SCORE = 1.0e4
WINDOW = 512
CONV_WIDTH = 31
N_EXPERTS = 32
TOP_K = 4
SWIGLU_ALPHA = 1.702
SWIGLU_LIMIT = 7.0
MOE_BLOCK = 256
NORM_EPS = 1e-6

LANES = 128
NEG = -1e30
TAKEN = -3e38
VMEM_LIMIT = 56 * 1024 * 1024

COL_DQ, COL_DK, COL_DV, COL_NQ = 0, 512, 1024, 1536
COL_KCMP, COL_VCMP, COL_KSEL, COL_VSEL, COL_KWIN, COL_VWIN = 2048, 2176, 2304, 2432, 2560, 2688
COL_GATE = 2816
W_IN_PAD = 2944
NORM_SLABS = frozenset(list(range(0, 8)) + list(range(12, 16)) + [COL_KSEL // LANES, COL_KWIN // LANES])


def _cparams(sem):
    return pltpu.CompilerParams(dimension_semantics=sem, vmem_limit_bytes=VMEM_LIMIT)


def _sigmoid(v):
    return 1.0 / (1.0 + jnp.exp(-v))


def _norm_mod(x, g, sc, sh):
    ms = jnp.mean(x * x, axis=-1, keepdims=True)
    return (x * lax.rsqrt(ms + NORM_EPS)) * g * (1.0 + sc) + sh


def _mod_kernel(c_ref, w_ref, b_ref, o_ref):
    c = c_ref[...]
    cond = c * _sigmoid(c)
    o_ref[0] = jnp.dot(cond, w_ref[0], preferred_element_type=F32, precision=lax.Precision.HIGHEST) + b_ref[0]


def _modulation(c, mod_w, mod_b):
    depth, d, n = mod_w.shape
    b = c.shape[0]
    tn = 1536
    return pl.pallas_call(
        _mod_kernel,
        out_shape=jax.ShapeDtypeStruct((depth, b, n), F32),
        grid=(depth, n // tn),
        in_specs=[
            pl.BlockSpec((b, d), lambda i, j: (0, 0)),
            pl.BlockSpec((1, d, tn), lambda i, j: (i, 0, j)),
            pl.BlockSpec((1, 1, tn), lambda i, j: (i, 0, j)),
        ],
        out_specs=pl.BlockSpec((1, b, tn), lambda i, j: (i, 0, j)),
        compiler_params=_cparams(("parallel", "parallel")),
        name="modulation",
    )(c, mod_w, mod_b.reshape(depth, 1, n))


def _in_attn_kernel(x_ref, mod_ref, g_ref, w_ref, cg_ref, z_ref, gate_ref):
    m = mod_ref[0]
    h = _norm_mod(x_ref[0], g_ref[...], m[1:2], m[0:1]).astype(BF16)
    lo = lax.broadcasted_iota(I32, (1, LANES), 1) < HEAD_DIM
    groups = ((0, 512), (512, 1024), (1024, 1536), (1536, 2048), (2048, 2816), (2816, 2944))
    for c0, c1 in groups:
        z = jnp.dot(h, w_ref[:, c0:c1], preferred_element_type=F32)
        for s in range((c1 - c0) // LANES):
            a0 = c0 + s * LANES
            zs = z[:, s * LANES:(s + 1) * LANES]
            if a0 == COL_GATE:
                gate_ref[0] = _sigmoid(zs)
                continue
            if a0 // LANES in NORM_SLABS:
                z2 = zs * zs
                s_lo = jnp.sum(jnp.where(lo, z2, 0.0), axis=-1, keepdims=True)
                s_hi = jnp.sum(jnp.where(lo, 0.0, z2), axis=-1, keepdims=True)
                inv = jnp.where(lo, lax.rsqrt(s_lo * (1.0 / HEAD_DIM) + NORM_EPS),
                                lax.rsqrt(s_hi * (1.0 / HEAD_DIM) + NORM_EPS))
                zs = zs * inv
            z_ref[0, :, a0:a0 + LANES] = (zs * cg_ref[:, a0:a0 + LANES]).astype(BF16)


def _attn_in_proj(x, mod, gain, w_pad, colgain, tm):
    b, s, d = x.shape
    return pl.pallas_call(
        _in_attn_kernel,
        out_shape=(jax.ShapeDtypeStruct((b, s, COL_GATE), BF16), jax.ShapeDtypeStruct((b, s, LANES), F32)),
        grid=(b, s // tm),
        in_specs=[
            pl.BlockSpec((1, tm, d), lambda i, j: (i, j, 0)),
            pl.BlockSpec((1, 6, d), lambda i, j: (i, 0, 0)),
            pl.BlockSpec((1, d), lambda i, j: (0, 0)),
            pl.BlockSpec((d, W_IN_PAD), lambda i, j: (0, 0)),
            pl.BlockSpec((1, W_IN_PAD), lambda i, j: (0, 0)),
        ],
        out_specs=(pl.BlockSpec((1, tm, COL_GATE), lambda i, j: (i, j, 0)),
                   pl.BlockSpec((1, tm, LANES), lambda i, j: (i, j, 0))),
        compiler_params=_cparams(("parallel", "parallel")),
        name="attn_in_proj",
    )(x, mod, gain, w_pad, colgain)


def _online_update(idx, s, vt, m_sc, l_sc, acc_sc):
    m_old = m_sc[idx]
    m_new = jnp.maximum(m_old, jnp.max(s, axis=0, keepdims=True))
    alpha = jnp.exp(m_old - m_new)
    p = jnp.exp(s - m_new)
    l_sc[idx] = alpha * l_sc[idx] + jnp.sum(p, axis=0, keepdims=True)
    acc_sc[idx] = alpha * acc_sc[idx] + jnp.dot(vt, p.astype(BF16), preferred_element_type=F32)
    m_sc[idx] = m_new


def _diff_kernel(lam_ref, qT_ref, k_ref, vT_ref, sub_ref, o_ref, m_sc, l_sc, acc_sc, *, tq, tk, lam_init):
    i = pl.program_id(2)
    q = qT_ref[0, 0]
    zero = jnp.zeros((HEAD_DIM, tq), BF16)
    qp = (jnp.concatenate([q[0], zero], axis=0), jnp.concatenate([zero, q[1]], axis=0))
    m_sc[...] = jnp.full(m_sc.shape, NEG, F32)
    l_sc[...] = jnp.zeros(l_sc.shape, F32)
    acc_sc[...] = jnp.zeros(acc_sc.shape, F32)

    def step(j, masked):
        kt = k_ref[0, pl.ds(pl.multiple_of(j * tk, tk), tk), :]
        vt = vT_ref[0, 0, j]
        for mm in range(2):
            s = jnp.dot(kt, qp[mm], preferred_element_type=F32)
            if masked:
                kpos = j * tk + lax.broadcasted_iota(I32, (tk, tq), 0)
                t = i * tq + lax.broadcasted_iota(I32, (tk, tq), 1)
                s = jnp.where(kpos <= t, s, NEG)
            _online_update(mm, s, vt, m_sc, l_sc, acc_sc)

    n_full = (i * tq + 1) // tk
    n_kv = ((i + 1) * tq + tk - 1) // tk
    lax.fori_loop(0, n_full, lambda j, c: (step(j, False), c)[1], 0)
    lax.fori_loop(n_full, n_kv, lambda j, c: (step(j, True), c)[1], 0)

    lv = lam_ref[...]
    lam = (jnp.exp(jnp.sum(lv[0:1] * lv[1:2], axis=-1, keepdims=True))
           - jnp.exp(jnp.sum(lv[2:3] * lv[3:4], axis=-1, keepdims=True)) + lam_init)
    o = acc_sc[0] / l_sc[0] - lam * (acc_sc[1] / l_sc[1])
    ms = jnp.mean(o * o, axis=0, keepdims=True)
    o = o * lax.rsqrt(ms + NORM_EPS) * sub_ref[...] * (1.0 - lam_init)
    o_ref[0] = o.T.astype(BF16)


def _diff_attention(zb, qT, vT, lam_vecs, subln, lam_init, tq, tk):
    b, s, _ = zb.shape
    kern = functools.partial(_diff_kernel, tq=tq, tk=tk, lam_init=lam_init)
    return pl.pallas_call(
        kern,
        out_shape=jax.ShapeDtypeStruct((b, s, DIFF_HEADS * 2 * HEAD_DIM), BF16),
        grid=(b, DIFF_HEADS, s // tq),
        in_specs=[
            pl.BlockSpec((4, HEAD_DIM), lambda bi, h, i: (0, 0)),
            pl.BlockSpec((1, 1, 2, HEAD_DIM, tq), lambda bi, h, i: (bi, h, 0, 0, i)),
            pl.BlockSpec((1, s, LANES), lambda bi, h, i: (bi, 0, COL_DK // LANES + h)),
            pl.BlockSpec((1, 1, s // tk, 2 * HEAD_DIM, tk), lambda bi, h, i: (bi, h, 0, 0, 0)),
            pl.BlockSpec((2 * HEAD_DIM, 1), lambda bi, h, i: (0, 0)),
        ],
        out_specs=pl.BlockSpec((1, tq, LANES), lambda bi, h, i: (bi, i, h)),
        scratch_shapes=[pltpu.VMEM((2, 1, tq), F32), pltpu.VMEM((2, 1, tq), F32),
                        pltpu.VMEM((2, 2 * HEAD_DIM, tq), F32)],
        compiler_params=_cparams(("parallel", "parallel", "arbitrary")),
        name="diff_attention",
    )(lam_vecs, qT, zb, vT, subln.reshape(-1, 1))


def _gelu_tanh(v):
    return 0.5 * v * (1.0 + jnp.tanh(math.sqrt(2.0 / math.pi) * (v + 0.044715 * (v * v * v))))


def _cmp_kernel(c_ref, pos_ref, w1_ref, w2_ref, kg_ref, o_ref, *, n_cmp):
    jh = pl.program_id(1)
    half = CMP_STRIDE * HEAD_DIM
    c = c_ref[0, 0]
    w1a = w1_ref[0, :half, :].astype(BF16)
    w1b = w1_ref[0, half:, :].astype(BF16)
    pos = jnp.broadcast_to(pos_ref[0], (8, 2 * half)).astype(BF16)
    bias = (jnp.dot(pos[:, :half], w1a, preferred_element_type=F32)
            + jnp.dot(pos[:, half:], w1b, preferred_element_type=F32))[0:1]
    u = jnp.dot(c, w1a, preferred_element_type=F32)
    v = jnp.dot(c, w1b, preferred_element_type=F32)
    ncp = u.shape[0]
    hid = _gelu_tanh(u + pltpu.roll(v, ncp - 1, 0) + bias)
    y = jnp.dot(hid.astype(BF16), w2_ref[0].astype(BF16), preferred_element_type=F32)
    yn = y * lax.rsqrt(jnp.mean(y * y, axis=-1, keepdims=True) + NORM_EPS) * kg_ref[...]
    y = jnp.where(jh < NSA_KV_HEADS, yn, y)
    row = lax.broadcasted_iota(I32, y.shape, 0)
    o_ref[0, 0] = jnp.where(row < n_cmp, y, 0.0)


def _compress(chunks, pos, w1, w2, kgain):
    b, _, ncp, cd = chunks.shape
    kern = functools.partial(_cmp_kernel, n_cmp=ncp - 1)
    return pl.pallas_call(
        kern,
        out_shape=jax.ShapeDtypeStruct((b, 4, ncp, HEAD_DIM), F32),
        grid=(b, 4),
        in_specs=[
            pl.BlockSpec((1, 1, ncp, cd), lambda bi, j: (bi, j, 0, 0)),
            pl.BlockSpec((1, 1, 2 * cd), lambda bi, j: (j // NSA_KV_HEADS, 0, 0)),
            pl.BlockSpec((1, 2 * cd, CMP_HIDDEN), lambda bi, j: (j // NSA_KV_HEADS, 0, 0)),
            pl.BlockSpec((1, CMP_HIDDEN, HEAD_DIM), lambda bi, j: (j // NSA_KV_HEADS, 0, 0)),
            pl.BlockSpec((1, HEAD_DIM), lambda bi, j: (0, 0)),
        ],
        out_specs=pl.BlockSpec((1, 1, ncp, HEAD_DIM), lambda bi, j: (bi, j, 0, 0)),
        compiler_params=_cparams(("parallel", "parallel")),
        name="nsa_compress",
    )(chunks, pos, w1, w2, kgain)


def _nsa_kernel(qT_ref, ksel_ref, vselT_ref, kwin_ref, vwinT_ref, kc_ref, vcT_ref, ovl_ref, gate_ref, o_ref,
                selb_sc, m_sc, l_sc, acc_sc, *, tq, n_top):
    hk = pl.program_id(1)
    i = pl.program_id(2)
    g4 = NSA_GROUP
    nq = g4 * tq
    q4 = jnp.concatenate([qT_ref[0, 0, g] for g in range(g4)], axis=1)
    half = lax.broadcasted_iota(I32, (2 * HEAD_DIM, nq), 0) // HEAD_DIM
    qp = jnp.where(half == hk, jnp.concatenate([q4, q4], axis=0), jnp.zeros((), BF16))
    t1 = i * tq + lax.broadcasted_iota(I32, (1, tq), 1)

    def tile4(a):
        return jnp.concatenate([a] * g4, axis=1)

    ncp = kc_ref.shape[1]
    sc = jnp.dot(kc_ref[0], qp, preferred_element_type=F32)
    cend = CMP_STRIDE * lax.broadcasted_iota(I32, (ncp, tq), 0) + (CMP_BLOCK - 1)
    sc = sc + tile4(jnp.where(cend <= t1, 0.0, NEG))
    e = jnp.where(sc > 0.5 * NEG, jnp.exp(sc - jnp.max(sc, axis=0, keepdims=True)), 0.0)
    p = e / jnp.maximum(jnp.sum(e, axis=0, keepdims=True), 1e-30)
    o_c = jnp.dot(vcT_ref[0, 0], p.astype(BF16), preferred_element_type=F32)

    psum = p[:, 0:tq]
    for g in range(1, g4):
        psum = psum + p[:, g * tq:(g + 1) * tq]
    p_hi = psum.astype(BF16)
    p_lo = (psum - p_hi.astype(F32)).astype(BF16)
    imp = (jnp.dot(ovl_ref[...], p_hi, preferred_element_type=F32)
           + jnp.dot(ovl_ref[...], p_lo, preferred_element_type=F32))
    n_sel = imp.shape[0]
    jrow = lax.broadcasted_iota(I32, (n_sel, tq), 0)
    cur = t1 // SEL_BLOCK
    forced = (jrow == 0) | (jrow == cur) | (jrow == cur - 1)
    score = jnp.where(forced, SEL_FORCED_SCORE, jnp.where(jrow <= cur, imp, NEG))
    selb = jnp.full((n_sel, tq), NEG, F32)
    jrow_f = jrow.astype(F32)
    for _ in range(n_top):
        best = jnp.max(score, axis=0, keepdims=True)
        pick = jnp.min(jnp.where(score == best, jrow_f, float(n_sel)), axis=0, keepdims=True)
        hit = jrow_f == pick
        selb = jnp.where(hit, 0.0, selb)
        score = jnp.where(hit, TAKEN, score)
    selb_sc[...] = selb

    m_sc[...] = jnp.full(m_sc.shape, NEG, F32)
    l_sc[...] = jnp.zeros(l_sc.shape, F32)
    acc_sc[...] = jnp.zeros(acc_sc.shape, F32)
    tk = tq
    bpt = tk // SEL_BLOCK
    krow = lax.broadcasted_iota(I32, (tk, tq), 0)

    def sel_step(j, c):
        kt = ksel_ref[0, pl.ds(pl.multiple_of(j * tk, tk), tk), :]
        s = jnp.dot(kt, qp, preferred_element_type=F32)
        rows = [jnp.broadcast_to(selb_sc[pl.ds(j * bpt + r, 1), :], (SEL_BLOCK, tq)) for r in range(bpt)]
        bias = jnp.where(j * tk + krow <= t1, jnp.concatenate(rows, axis=0), NEG)
        _online_update(0, s + tile4(bias), vselT_ref[0, 0, j], m_sc, l_sc, acc_sc)
        return c

    lax.fori_loop(0, i + 1, sel_step, 0)

    def win_step(j, c):
        kt = kwin_ref[0, pl.ds(pl.multiple_of(j * tk, tk), tk), :]
        s = jnp.dot(kt, qp, preferred_element_type=F32)
        kpos = j * tk + krow
        bias = jnp.where(kpos <= t1, jnp.where(kpos > t1 - WINDOW, 0.0, NEG), NEG)
        _online_update(1, s + tile4(bias), vwinT_ref[0, 0, j], m_sc, l_sc, acc_sc)
        return c

    j_lo = jnp.maximum(i * tq - (WINDOW - 1), 0) // tk
    lax.fori_loop(j_lo, i + 1, win_step, 0)

    def gate_row(br):
        gt = gate_ref[0, 0, br]
        return jnp.concatenate([gt[g:g + 1] for g in range(g4)], axis=1)

    out = (o_c * gate_row(0) + (acc_sc[0] / l_sc[0]) * gate_row(1) + (acc_sc[1] / l_sc[1]) * gate_row(2))
    stacked = jnp.concatenate([out[:, g * tq:(g + 1) * tq] for g in range(g4)], axis=0)
    o_ref[0] = stacked.T.astype(BF16)


def _nsa_attention(zb, qT, vselT, vwinT, kc, vcT, ovl, gatesT, tq):
    b, s, _ = zb.shape
    nt = s // tq
    n_sel = s // SEL_BLOCK
    ncp = kc.shape[1]
    kern = functools.partial(_nsa_kernel, tq=tq, n_top=min(SEL_TOP, n_sel))
    gd = NSA_GROUP * HEAD_DIM
    return pl.pallas_call(
        kern,
        out_shape=jax.ShapeDtypeStruct((b, s, NSA_Q_HEADS * HEAD_DIM), BF16),
        grid=(b, NSA_KV_HEADS, nt),
        in_specs=[
            pl.BlockSpec((1, 1, NSA_GROUP, HEAD_DIM, tq), lambda bi, h, i: (bi, h, 0, 0, i)),
            pl.BlockSpec((1, s, LANES), lambda bi, h, i: (bi, 0, COL_KSEL // LANES)),
            pl.BlockSpec((1, 1, nt, HEAD_DIM, tq), lambda bi, h, i: (bi, h, 0, 0, 0)),
            pl.BlockSpec((1, s, LANES), lambda bi, h, i: (bi, 0, COL_KWIN // LANES)),
            pl.BlockSpec((1, 1, nt, HEAD_DIM, tq), lambda bi, h, i: (bi, h, 0, 0, 0)),
            pl.BlockSpec((1, ncp, LANES), lambda bi, h, i: (bi, 0, 0)),
            pl.BlockSpec((1, 1, HEAD_DIM, ncp), lambda bi, h, i: (bi, h, 0, 0)),
            pl.BlockSpec((n_sel, ncp), lambda bi, h, i: (0, 0)),
            pl.BlockSpec((1, 1, 3, NSA_GROUP, tq), lambda bi, h, i: (bi, h, 0, 0, i)),
        ],
        out_specs=pl.BlockSpec((1, tq, gd), lambda bi, h, i: (bi, i, h)),
        scratch_shapes=[pltpu.VMEM((n_sel, tq), F32),
                        pltpu.VMEM((2, 1, NSA_GROUP * tq), F32), pltpu.VMEM((2, 1, NSA_GROUP * tq), F32),
                        pltpu.VMEM((2, HEAD_DIM, NSA_GROUP * tq), F32)],
        compiler_params=_cparams(("parallel", "parallel", "arbitrary")),
        name="nsa_attention",
    )(qT, zb, vselT, zb, vwinT, kc, vcT, ovl, gatesT)


def _in_conv_kernel(x_ref, mod_ref, g_ref, w_ref, b_ref, u_ref):
    m = mod_ref[0]
    d = x_ref.shape[-1]
    h = _norm_mod(x_ref[0], g_ref[...], m[1:2], m[0:1]).astype(BF16)
    a = jnp.dot(h, w_ref[:, :d], preferred_element_type=F32) + b_ref[:, :d]
    g = jnp.dot(h, w_ref[:, d:], preferred_element_type=F32) + b_ref[:, d:]
    u_ref[0] = a * _sigmoid(g)


def _conv_in_proj(x, mod, gain, w, bias, tm):
    b, s, d = x.shape
    return pl.pallas_call(
        _in_conv_kernel,
        out_shape=jax.ShapeDtypeStruct((b, s, d), F32),
        grid=(b, s // tm),
        in_specs=[
            pl.BlockSpec((1, tm, d), lambda i, j: (i, j, 0)),
            pl.BlockSpec((1, 6, d), lambda i, j: (i, 0, 0)),
            pl.BlockSpec((1, d), lambda i, j: (0, 0)),
            pl.BlockSpec((d, 2 * d), lambda i, j: (0, 0)),
            pl.BlockSpec((1, 2 * d), lambda i, j: (0, 0)),
        ],
        out_specs=pl.BlockSpec((1, tm, d), lambda i, j: (i, j, 0)),
        compiler_params=_cparams(("parallel", "parallel")),
        name="conv_in_proj",
    )(x, mod, gain, w, bias)


HALO = 32


def _dwconv_kernel(u_ref, halo_ref, w_ref, b_ref, lg_ref, lb_ref, o_ref, buf_sc, acc_sc, *, tm):
    i = pl.program_id(1)
    d = u_ref.shape[-1]
    buf_sc[0:HALO, :] = jnp.where(i > 0, halo_ref[0], 0.0)
    buf_sc[HALO:, :] = u_ref[0]
    off = HALO - (CONV_WIDTH - 1)
    cw = 256
    for c in range(d // cw):
        cs = slice(c * cw, (c + 1) * cw)
        acc = jnp.zeros((tm, cw), F32) + b_ref[:, cs]
        for j in range(CONV_WIDTH):
            acc = acc + w_ref[j:j + 1, cs] * buf_sc[off + j:off + j + tm, cs]
        acc_sc[:, cs] = acc
    y = acc_sc[...]
    mu = jnp.mean(y, axis=-1, keepdims=True)
    yc = y - mu
    var = jnp.mean(yc * yc, axis=-1, keepdims=True)
    yn = yc * lax.rsqrt(var + NORM_EPS) * lg_ref[...] + lb_ref[...]
    o_ref[0] = (yn * _sigmoid(yn)).astype(BF16)


def _dwconv_ln_swish(u, dw_w, dw_b, ln_g, ln_b, tm):
    b, s, d = u.shape
    kern = functools.partial(_dwconv_kernel, tm=tm)
    hb = tm // HALO
    return pl.pallas_call(
        kern,
        out_shape=jax.ShapeDtypeStruct((b, s, d), BF16),
        grid=(b, s // tm),
        in_specs=[
            pl.BlockSpec((1, tm, d), lambda bi, i: (bi, i, 0)),
            pl.BlockSpec((1, HALO, d), lambda bi, i: (bi, jnp.maximum(i * hb - 1, 0), 0)),
            pl.BlockSpec((CONV_WIDTH, d), lambda bi, i: (0, 0)),
            pl.BlockSpec((1, d), lambda bi, i: (0, 0)),
            pl.BlockSpec((1, d), lambda bi, i: (0, 0)),
            pl.BlockSpec((1, d), lambda bi, i: (0, 0)),
        ],
        out_specs=pl.BlockSpec((1, tm, d), lambda bi, i: (bi, i, 0)),
        scratch_shapes=[pltpu.VMEM((tm + HALO, d), F32), pltpu.VMEM((tm, d), F32)],
        compiler_params=_cparams(("parallel", "parallel")),
        name="dwconv_ln_swish",
    )(u, u, dw_w, dw_b, ln_g, ln_b)


def _out_router_kernel(a1_ref, a2_ref, w1_ref, w2_ref, b_ref, x_ref, mod_ref, g_ref, rwh_ref, rwl_ref, rb_ref,
                       xo_ref, h_ref, idx_ref, wt_ref):
    m = mod_ref[0]
    y = (jnp.dot(a1_ref[0], w1_ref[...], preferred_element_type=F32)
         + jnp.dot(a2_ref[0], w2_ref[...], preferred_element_type=F32) + b_ref[...])
    x = x_ref[0] + m[2:3] * y
    xo_ref[0] = x
    h = _norm_mod(x, g_ref[...], m[4:5], m[3:4])
    h_ref[0] = h
    h_hi = h.astype(BF16)
    h_lo = (h - h_hi.astype(F32)).astype(BF16)
    logits = (jnp.dot(h_hi, rwh_ref[...], preferred_element_type=F32)
              + jnp.dot(h_lo, rwh_ref[...], preferred_element_type=F32)
              + jnp.dot(h_hi, rwl_ref[...], preferred_element_type=F32)) + rb_ref[...]
    lane = lax.broadcasted_iota(I32, logits.shape, 1)
    lane_f = lane.astype(F32)
    idx_out = jnp.zeros(logits.shape, F32)
    val_out = jnp.full(logits.shape, NEG, F32)
    for k in range(TOP_K):
        best = jnp.max(logits, axis=-1, keepdims=True)
        pick = jnp.min(jnp.where(logits == best, lane_f, float(LANES)), axis=-1, keepdims=True)
        idx_out = jnp.where(lane == k, pick, idx_out)
        val_out = jnp.where(lane == k, best, val_out)
        logits = jnp.where(lane_f == pick, TAKEN, logits)
    e = jnp.where(lane < TOP_K, jnp.exp(val_out - jnp.max(val_out, axis=-1, keepdims=True)), 0.0)
    idx_ref[0] = idx_out.astype(I32)
    wt_ref[0] = e / jnp.sum(e, axis=-1, keepdims=True)


def _out_proj_router(a1, a2, c1, c2, w1, w2, bias, x, mod, gain, rw_hi, rw_lo, rb, tm):
    b, s, d = x.shape
    kw = w1.shape[0]
    tok = lambda i, j: (i, j, 0)
    const = lambda i, j: (0, 0)
    return pl.pallas_call(
        _out_router_kernel,
        out_shape=(jax.ShapeDtypeStruct((b, s, d), F32), jax.ShapeDtypeStruct((b, s, d), F32),
                   jax.ShapeDtypeStruct((b, s, LANES), I32), jax.ShapeDtypeStruct((b, s, LANES), F32)),
        grid=(b, s // tm),
        in_specs=[
            pl.BlockSpec((1, tm, kw), lambda i, j: (i, j, c1)),
            pl.BlockSpec((1, tm, kw), lambda i, j: (i, j, c2)),
            pl.BlockSpec((kw, d), const),
            pl.BlockSpec((kw, d), const),
            pl.BlockSpec((1, d), const),
            pl.BlockSpec((1, tm, d), tok),
            pl.BlockSpec((1, 6, d), lambda i, j: (i, 0, 0)),
            pl.BlockSpec((1, d), const),
            pl.BlockSpec((d, LANES), const),
            pl.BlockSpec((d, LANES), const),
            pl.BlockSpec((1, LANES), const),
        ],
        out_specs=(pl.BlockSpec((1, tm, d), tok), pl.BlockSpec((1, tm, d), tok),
                   pl.BlockSpec((1, tm, LANES), tok), pl.BlockSpec((1, tm, LANES), tok)),
        compiler_params=_cparams(("parallel", "parallel")),
        name="out_proj_router",
    )(a1, a2, w1, w2, bias, x, mod, gain, rw_hi, rw_lo, rb)


def _dispatch_kernel(pos_ref, h_ref, zero_ref, xs_ref, sem, *, tm):
    del zero_ref
    n = tm * TOP_K

    def row_copy(a):
        return pltpu.make_async_copy(h_ref.at[pl.ds(a // TOP_K, 1)], xs_ref.at[pl.ds(pos_ref[0, 0, a], 1)], sem)

    def start(a, c):
        row_copy(a).start()
        return c

    def wait(a, c):
        row_copy(a).wait()
        return c

    lax.fori_loop(0, n, start, 0)
    lax.fori_loop(0, n, wait, 0)


def _dispatch(h, pos, n_slots, tm):
    t, d = h.shape
    kern = functools.partial(_dispatch_kernel, tm=tm)
    return pl.pallas_call(
        kern,
        out_shape=jax.ShapeDtypeStruct((n_slots, d), F32),
        grid=(t // tm,),
        in_specs=[
            pl.BlockSpec((1, 1, tm * TOP_K), lambda i: (i, 0, 0), memory_space=pltpu.SMEM),
            pl.BlockSpec((tm, d), lambda i: (i, 0)),
            pl.BlockSpec(memory_space=pl.ANY),
        ],
        out_specs=pl.BlockSpec(memory_space=pl.ANY),
        scratch_shapes=[pltpu.SemaphoreType.DMA(())],
        input_output_aliases={2: 0},
        compiler_params=_cparams(("arbitrary",)),
        name="moe_dispatch",
    )(pos.reshape(t // tm, 1, tm * TOP_K), h, jnp.zeros((n_slots, d), F32))


def _expert_kernel(be_ref, nu_ref, xs_ref, w1g_ref, w1l_ref, b1g_ref, b1l_ref, w2_ref, b2_ref, ys_ref):
    del be_ref

    @pl.when(pl.program_id(0) < nu_ref[0])
    def _():
        xb = xs_ref[...].astype(BF16)
        glu = jnp.dot(xb, w1g_ref[0], preferred_element_type=F32) + b1g_ref[0]
        lin = jnp.dot(xb, w1l_ref[0], preferred_element_type=F32) + b1l_ref[0]
        glu = jnp.minimum(glu, SWIGLU_LIMIT)
        lin = jnp.clip(lin, -SWIGLU_LIMIT, SWIGLU_LIMIT)
        act = glu * _sigmoid(SWIGLU_ALPHA * glu) * (lin + 1.0)
        ys_ref[...] = jnp.dot(act.astype(BF16), w2_ref[0], preferred_element_type=F32) + b2_ref[0]


def _experts(xs, block_expert, n_used, w1g, w1l, b1g, b1l, w2, b2):
    n_slots, d = xs.shape
    n_blocks = n_slots // MOE_BLOCK
    f = w1g.shape[-1]

    def blk(i, be, nu):
        return (jnp.minimum(i, nu[0] - 1), 0)

    def wsel(i, be, nu):
        return (be[jnp.minimum(i, nu[0] - 1)], 0, 0)

    return pl.pallas_call(
        _expert_kernel,
        out_shape=jax.ShapeDtypeStruct((n_slots, d), F32),
        grid_spec=pltpu.PrefetchScalarGridSpec(
            num_scalar_prefetch=2,
            grid=(n_blocks,),
            in_specs=[
                pl.BlockSpec((MOE_BLOCK, d), blk),
                pl.BlockSpec((1, d, f), wsel),
                pl.BlockSpec((1, d, f), wsel),
                pl.BlockSpec((1, 1, f), wsel),
                pl.BlockSpec((1, 1, f), wsel),
                pl.BlockSpec((1, f, d), wsel),
                pl.BlockSpec((1, 1, d), wsel),
            ],
            out_specs=pl.BlockSpec((MOE_BLOCK, d), blk),
        ),
        compiler_params=_cparams(("arbitrary",)),
        name="moe_experts",
    )(block_expert, n_used, xs, w1g, w1l, b1g, b1l, w2, b2)


def _combine_kernel(pos_ref, ys_ref, wt_ref, x_ref, mod_ref, o_ref, buf, sem, *, tm):
    n = tm * TOP_K

    def row_copy(a):
        return pltpu.make_async_copy(ys_ref.at[pl.ds(pos_ref[0, 0, a], 1)],
                                     buf.at[a % TOP_K, pl.ds(a // TOP_K, 1)], sem)

    def start(a, c):
        row_copy(a).start()
        return c

    def wait(a, c):
        row_copy(a).wait()
        return c

    lax.fori_loop(0, n, start, 0)
    lax.fori_loop(0, n, wait, 0)
    wt = wt_ref[0]
    y = wt[:, 0:1] * buf[0]
    for k in range(1, TOP_K):
        y = y + wt[:, k:k + 1] * buf[k]
    o_ref[0] = x_ref[0] + mod_ref[0][5:6] * y


def _combine(ys, pos, wts, x, mod, tm):
    b, s, d = x.shape
    t = b * s
    nt = s // tm
    kern = functools.partial(_combine_kernel, tm=tm)
    tok = lambda i, j: (i, j, 0)
    return pl.pallas_call(
        kern,
        out_shape=jax.ShapeDtypeStruct((b, s, d), F32),
        grid=(b, nt),
        in_specs=[
            pl.BlockSpec((1, 1, tm * TOP_K), lambda i, j: (i * nt + j, 0, 0), memory_space=pltpu.SMEM),
            pl.BlockSpec(memory_space=pl.ANY),
            pl.BlockSpec((1, tm, LANES), tok),
            pl.BlockSpec((1, tm, d), tok),
            pl.BlockSpec((1, 6, d), lambda i, j: (i, 0, 0)),
        ],
        out_specs=pl.BlockSpec((1, tm, d), tok),
        scratch_shapes=[pltpu.VMEM((TOP_K, tm, d), F32), pltpu.SemaphoreType.DMA(())],
        compiler_params=_cparams(("arbitrary", "arbitrary")),
        name="moe_combine",
    )(pos.reshape(t // tm, 1, tm * TOP_K), ys, wts, x, mod)


def _route_positions(top_idx, n_blocks):
    flat_e = top_idx.reshape(-1)
    onehot = (flat_e[:, None] == jnp.arange(N_EXPERTS, dtype=I32)[None, :]).astype(I32)
    csum = jnp.cumsum(onehot, axis=0)
    rank = jnp.take_along_axis(csum, flat_e[:, None], axis=1)[:, 0] - 1
    counts = csum[-1]
    padded = (counts + MOE_BLOCK - 1) // MOE_BLOCK * MOE_BLOCK
    pad_end = jnp.cumsum(padded)
    pad_start = pad_end - padded
    pos = (pad_start[flat_e] + rank).astype(I32)
    block_expert = jnp.minimum(
        jnp.searchsorted(pad_end, jnp.arange(n_blocks, dtype=I32) * MOE_BLOCK, side='right'), N_EXPERTS - 1).astype(I32)
    n_used = (pad_end[-1] // MOE_BLOCK).astype(I32).reshape(1)
    return pos, block_expert, n_used


def _moe(x_mid, h, idx, wts, mod, w1g, w1l, b1g, b1l, w2, b2, tm):
    b, s, d = x_mid.shape
    t = b * s
    n_blocks = -(-t * TOP_K // MOE_BLOCK) + N_EXPERTS
    top_idx = idx.reshape(t, LANES)[:, :TOP_K]
    pos, block_expert, n_used = _route_positions(top_idx, n_blocks)
    xs = _dispatch(h.reshape(t, d), pos, n_blocks * MOE_BLOCK, tm)
    ys = _experts(xs, block_expert, n_used, w1g, w1l, b1g, b1l, w2, b2)
    return _combine(ys, pos, wts, x_mid, mod, tm)


def _pad_cols(w, n):
    return jnp.pad(w, ((0, 0), (0, n - w.shape[-1])))


def _attn_colgain(diff_qk_gain, nsa_q_gain, nsa_k_gain):
    scale = HEAD_DIM ** -0.5
    ones = jnp.ones((LANES,), F32)
    parts = [jnp.tile(diff_qk_gain[0] * scale, 8), jnp.tile(diff_qk_gain[1], 8), jnp.ones((512,), F32),
             jnp.tile(nsa_q_gain * scale, 8), ones, ones, jnp.tile(nsa_k_gain[1], 2), ones,
             jnp.tile(nsa_k_gain[2], 2), ones, ones]
    return jnp.concatenate(parts).reshape(1, W_IN_PAD)


def _overlap_matrix(s):
    n_sel = s // SEL_BLOCK
    ncp = s // CMP_STRIDE
    c_start = np.arange(ncp) * CMP_STRIDE
    s_start = np.arange(n_sel) * SEL_BLOCK
    ovl = (c_start[None, :] < s_start[:, None] + SEL_BLOCK) & (c_start[None, :] + CMP_BLOCK > s_start[:, None])
    ovl[:, ncp - 1] = False
    return jnp.asarray(ovl.astype(np.float32), dtype=BF16)


def _attention_layer(x, mod, gain, w_in, w_out, diff_qk_gain, diff_lambda, diff_subln, nsa_q_gain, nsa_k_gain,
                     cmp_pos, cmp_w1, cmp_w2, lam_init):
    b, s, d = x.shape
    tq_d, tk_d, tq_n = 256, 512, 256
    zb, gsig = _attn_in_proj(x, mod, gain, _pad_cols(w_in, W_IN_PAD).astype(BF16),
                             _attn_colgain(diff_qk_gain, nsa_q_gain, nsa_k_gain), 512)

    def cols(c0, n):
        return zb[:, :, c0:c0 + n]

    dqT = cols(COL_DQ, 512).reshape(b, s, DIFF_HEADS, 2, HEAD_DIM).transpose(0, 2, 3, 4, 1)
    dvT = cols(COL_DV, 512).reshape(b, s // tk_d, tk_d, DIFF_HEADS, 2 * HEAD_DIM).transpose(0, 3, 1, 4, 2)
    o_diff = _diff_attention(zb, dqT, dvT, diff_lambda, diff_subln, lam_init, tq_d, tk_d)

    nqT = cols(COL_NQ, 512).reshape(b, s, NSA_KV_HEADS, NSA_GROUP, HEAD_DIM).transpose(0, 2, 3, 4, 1)

    def val_t(c0):
        return cols(c0, LANES).reshape(b, s // tq_n, tq_n, NSA_KV_HEADS, HEAD_DIM).transpose(0, 3, 1, 4, 2)

    ncp = s // CMP_STRIDE
    chunks = cols(COL_KCMP, 2 * LANES).reshape(b, ncp, CMP_STRIDE, 2, NSA_KV_HEADS, HEAD_DIM)
    chunks = chunks.transpose(0, 3, 4, 1, 2, 5).reshape(b, 4, ncp, CMP_STRIDE * HEAD_DIM)
    cmp_out = _compress(chunks, cmp_pos.reshape(2, 1, CMP_BLOCK * HEAD_DIM), cmp_w1, cmp_w2, nsa_k_gain[0:1])
    kc = cmp_out[:, 0:2].transpose(0, 2, 1, 3).reshape(b, ncp, LANES).astype(BF16)
    vcT = cmp_out[:, 2:4].transpose(0, 1, 3, 2).astype(BF16)
    gatesT = gsig[:, :, :NSA_Q_HEADS * 3].reshape(b, s, NSA_KV_HEADS, NSA_GROUP, 3).transpose(0, 2, 4, 3, 1)
    o_nsa = _nsa_attention(zb, nqT, val_t(COL_VSEL), val_t(COL_VWIN), kc, vcT, _overlap_matrix(s), gatesT, tq_n)
    return o_diff, o_nsa


def kernel(x, c, mod_w, mod_b, norm_mix, norm_ffn, attn_w_in, attn_w_out, diff_qk_gain, diff_lambda, diff_subln,
           nsa_q_gain, nsa_k_gain, nsa_cmp_pos, nsa_cmp_w1, nsa_cmp_w2, conv_pw1_w, conv_pw1_b, conv_dw_w,
           conv_dw_b, conv_ln_g, conv_ln_b, conv_pw2_w, conv_pw2_b, router_w, router_b, moe_w1, moe_b1, moe_w2,
           moe_b2):
    b, s, d = x.shape
    depth = mod_w.shape[0]
    tm = 512 if s % 512 == 0 else 256
    mods = _modulation(c, mod_w, mod_b).reshape(depth, b, 6, d)
    half = d // 2
    for i in range(depth):
        mod = mods[i]
        j = i // 2
        if i % 2 == 0:
            lam_init = 0.8 - 0.6 * math.exp(-0.3 * i)
            o_diff, o_nsa = _attention_layer(
                x, mod, norm_mix[i:i + 1], attn_w_in[j], attn_w_out[j], diff_qk_gain[j], diff_lambda[j],
                diff_subln[j], nsa_q_gain[j], nsa_k_gain[j], nsa_cmp_pos[j], nsa_cmp_w1[j], nsa_cmp_w2[j], lam_init)
            a1, a2, c1, c2 = o_diff, o_nsa, 0, 0
            w_o = attn_w_out[j].astype(BF16)
            bias = jnp.zeros((1, d), F32)
        else:
            u = _conv_in_proj(x, mod, norm_mix[i:i + 1], conv_pw1_w[j].astype(BF16), conv_pw1_b[j:j + 1], tm)
            v = _dwconv_ln_swish(u, conv_dw_w[j].reshape(CONV_WIDTH, d), conv_dw_b[j:j + 1], conv_ln_g[j:j + 1],
                                 conv_ln_b[j:j + 1], 256)
            a1, a2, c1, c2 = v, v, 0, 1
            w_o = conv_pw2_w[j].astype(BF16)
            bias = conv_pw2_b[j:j + 1]
        rw = _pad_cols(router_w[i], LANES)
        rw_hi = rw.astype(BF16)
        rw_lo = (rw - rw_hi.astype(F32)).astype(BF16)
        rb = jnp.concatenate([router_b[i], jnp.full((LANES - N_EXPERTS,), NEG, F32)]).reshape(1, LANES)
        x_mid, h, idx, wts = _out_proj_router(a1, a2, c1, c2, w_o[:half], w_o[half:], bias, x, mod,
                                              norm_ffn[i:i + 1], rw_hi, rw_lo, rb, tm)
        w1 = moe_w1[i]
        x = _moe(x_mid, h, idx, wts, mod, w1[:, :, 0::2].astype(BF16), w1[:, :, 1::2].astype(BF16),
                 moe_b1[i][:, None, 0::2], moe_b1[i][:, None, 1::2], moe_w2[i].astype(BF16), moe_b2[i][:, None, :], 256)
    return x
```

```python
import functools
import math

import jax
import jax.numpy as jnp
import numpy as np
from jax import lax
from jax.experimental import pallas as pl
from jax.experimental.pallas import tpu as pltpu

F32 = jnp.float32
BF16 = jnp.bfloat16
I32 = jnp.int32

D_MODEL = 1024
HEAD_DIM = 64
DIFF_HEADS = 4
NSA_Q_HEADS = 8
NSA_KV_HEADS = 2
NSA_GROUP = NSA_Q_HEADS // NSA_KV_HEADS
CMP_BLOCK = 32
CMP_STRIDE = 16
CMP_HIDDEN = 256
SEL_BLOCK = 64
SEL_TOP = 16
SEL_FORCED_SCORE = 1.0e4
WINDOW = 512
CONV_WIDTH = 31
N_EXPERTS = 32
TOP_K = 4
SWIGLU_ALPHA = 1.702
SWIGLU_LIMIT = 7.0
MOE_BLOCK = 256
NORM_EPS = 1e-6

LANES = 128
NEG = -1e30
TAKEN = -3e38
VMEM_LIMIT = 56 * 1024 * 1024

COL_DQ, COL_DK, COL_DV, COL_NQ = 0, 512, 1024, 1536
COL_KCMP, COL_VCMP, COL_KSEL, COL_VSEL, COL_KWIN, COL_VWIN = 2048, 2176, 2304, 2432, 2560, 2688
COL_GATE = 2816
W_IN_PAD = 2944
NORM_SLABS = frozenset(list(range(0, 8)) + list(range(12, 16)) + [COL_KSEL // LANES, COL_KWIN // LANES])


def _cparams(sem):
    return pltpu.CompilerParams(dimension_semantics=sem, vmem_limit_bytes=VMEM_LIMIT)


def _sigmoid(v):
    return 1.0 / (1.0 + jnp.exp(-v))


def _norm_mod(x, g, sc, sh):
    ms = jnp.mean(x * x, axis=-1, keepdims=True)
    return (x * lax.rsqrt(ms + NORM_EPS)) * g * (1.0 + sc) + sh


def _mod_kernel(c_ref, w_ref, b_ref, o_ref):
    c = c_ref[...]
    cond = c * _sigmoid(c)
    o_ref[0] = jnp.dot(cond, w_ref[0], preferred_element_type=F32, precision=lax.Precision.HIGHEST) + b_ref[0]


def _modulation(c, mod_w, mod_b):
    depth, d, n = mod_w.shape
    b = c.shape[0]
    tn = 1536
    return pl.pallas_call(
        _mod_kernel,
        out_shape=jax.ShapeDtypeStruct((depth, b, n), F32),
        grid=(depth, n // tn),
        in_specs=[
            pl.BlockSpec((b, d), lambda i, j: (0, 0)),
            pl.BlockSpec((1, d, tn), lambda i, j: (i, 0, j)),
            pl.BlockSpec((1, 1, tn), lambda i, j: (i, 0, j)),
        ],
        out_specs=pl.BlockSpec((1, b, tn), lambda i, j: (i, 0, j)),
        compiler_params=_cparams(("parallel", "parallel")),
        name="modulation",
    )(c, mod_w, mod_b.reshape(depth, 1, n))


def _in_attn_kernel(x_ref, mod_ref, g_ref, w_ref, cg_ref, z_ref, gate_ref):
    m = mod_ref[0]
    h = _norm_mod(x_ref[0], g_ref[...], m[1:2], m[0:1]).astype(BF16)
    lo = lax.broadcasted_iota(I32, (1, LANES), 1) < HEAD_DIM
    groups = ((0, 512), (512, 1024), (1024, 1536), (1536, 2048), (2048, 2816), (2816, 2944))
    for c0, c1 in groups:
        z = jnp.dot(h, w_ref[:, c0:c1], preferred_element_type=F32)
        for s in range((c1 - c0) // LANES):
            a0 = c0 + s * LANES
            zs = z[:, s * LANES:(s + 1) * LANES]
            if a0 == COL_GATE:
                gate_ref[0] = _sigmoid(zs)
                continue
            if a0 // LANES in NORM_SLABS:
                z2 = zs * zs
                s_lo = jnp.sum(jnp.where(lo, z2, 0.0), axis=-1, keepdims=True)
                s_hi = jnp.sum(jnp.where(lo, 0.0, z2), axis=-1, keepdims=True)
                inv = jnp.where(lo, lax.rsqrt(s_lo * (1.0 / HEAD_DIM) + NORM_EPS),
                                lax.rsqrt(s_hi * (1.0 / HEAD_DIM) + NORM_EPS))
                zs = zs * inv
            z_ref[0, :, a0:a0 + LANES] = (zs * cg_ref[:, a0:a0 + LANES]).astype(BF16)


def _attn_in_proj(x, mod, gain, w_pad, colgain, tm):
    b, s, d = x.shape
    return pl.pallas_call(
        _in_attn_kernel,
        out_shape=(jax.ShapeDtypeStruct((b, s, COL_GATE), BF16), jax.ShapeDtypeStruct((b, s, LANES), F32)),
        grid=(b, s // tm),
        in_specs=[
            pl.BlockSpec((1, tm, d), lambda i, j: (i, j, 0)),
            pl.BlockSpec((1, 6, d), lambda i, j: (i, 0, 0)),
            pl.BlockSpec((1, d), lambda i, j: (0, 0)),
            pl.BlockSpec((d, W_IN_PAD), lambda i, j: (0, 0)),
            pl.BlockSpec((1, W_IN_PAD), lambda i, j: (0, 0)),
        ],
        out_specs=(pl.BlockSpec((1, tm, COL_GATE), lambda i, j: (i, j, 0)),
                   pl.BlockSpec((1, tm, LANES), lambda i, j: (i, j, 0))),
        compiler_params=_cparams(("parallel", "parallel")),
        name="attn_in_proj",
    )(x, mod, gain, w_pad, colgain)


FIXED_SHIFT_LIMIT = 57.0
LOG2E = math.log2(math.e)


def _score_bound(gain_q, gain_k):
    return (math.sqrt(HEAD_DIM) * LOG2E) * jnp.max(jnp.abs(gain_q * gain_k))


def _fixed_update(ss, vts, shift, l_ref, acc_ref):
    ps = [jnp.exp2(s - shift) for s in ss]
    l_ref[...] += sum(jnp.sum(p, axis=0, keepdims=True) for p in ps)
    acc_ref[...] += sum(jnp.dot(vt, p.astype(BF16), preferred_element_type=F32) for vt, p in zip(vts, ps))


def _online_update(ss, vts, m_ref, l_ref, acc_ref):
    for s, vt in zip(ss, vts):
        m_old = m_ref[...]
        m_new = jnp.maximum(m_old, jnp.max(s, axis=0, keepdims=True))
        alpha = jnp.exp2(m_old - m_new)
        p = jnp.exp2(s - m_new)
        l_ref[...] = alpha * l_ref[...] + jnp.sum(p, axis=0, keepdims=True)
        acc_ref[...] = alpha * acc_ref[...] + jnp.dot(vt, p.astype(BF16), preferred_element_type=F32)
        m_ref[...] = m_new


def _tile_loops(n_full, n_all, tiles):
    def pair(jj, c):
        tiles((2 * jj, 2 * jj + 1), False)
        return c

    lax.fori_loop(0, n_full // 2, pair, 0)

    @pl.when(n_full % 2 == 1)
    def _():
        tiles((n_full - 1,), False)

    def single(j, c):
        tiles((j,), True)
        return c

    lax.fori_loop(n_full, n_all, single, 0)


def _diff_kernel(mb_ref, lam_ref, qT_ref, k_ref, vT_ref, sub_ref, o_ref, m0, m1, l0, l1, a0, a1, *, tq, tk, lam_init):
    i = pl.program_id(2)
    m_refs, l_refs, acc_refs = (m0, m1), (l0, l1), (a0, a1)
    q = qT_ref[0, 0]
    zero = jnp.zeros((HEAD_DIM, tq), BF16)
    qp = (jnp.concatenate([q[0], zero], axis=0), jnp.concatenate([zero, q[1]], axis=0))
    for mm in range(2):
        m_refs[mm][...] = jnp.full(m_refs[mm].shape, NEG, F32)
        l_refs[mm][...] = jnp.zeros(l_refs[mm].shape, F32)
        acc_refs[mm][...] = jnp.zeros(acc_refs[mm].shape, F32)
    shift = mb_ref[0]

    def tiles(js, masked, fixed):
        kts = [k_ref[0, pl.ds(pl.multiple_of(j * tk, tk), tk), :] for j in js]
        vts = [vT_ref[0, 0, j] for j in js]
        for mm in range(2):
            ss = []
            for j, kt in zip(js, kts):
                s = jnp.dot(kt, qp[mm], preferred_element_type=F32)
                if masked:
                    kpos = j * tk + lax.broadcasted_iota(I32, (tk, tq), 0)
                    t = i * tq + lax.broadcasted_iota(I32, (tk, tq), 1)
                    s = jnp.where(kpos <= t, s, NEG)
                ss.append(s)
            if fixed:
                _fixed_update(ss, vts, shift, l_refs[mm], acc_refs[mm])
            else:
                _online_update(ss, vts, m_refs[mm], l_refs[mm], acc_refs[mm])

    n_full = (i * tq + 1) // tk
    n_kv = ((i + 1) * tq + tk - 1) // tk

    @pl.when(shift <= FIXED_SHIFT_LIMIT)
    def _():
        _tile_loops(n_full, n_kv, functools.partial(tiles, fixed=True))

    @pl.when(shift > FIXED_SHIFT_LIMIT)
    def _():
        _tile_loops(n_full, n_kv, functools.partial(tiles, fixed=False))

    lv = lam_ref[...]
    lam = (jnp.exp(jnp.sum(lv[0:1] * lv[1:2], axis=-1, keepdims=True))
           - jnp.exp(jnp.sum(lv[2:3] * lv[3:4], axis=-1, keepdims=True)) + lam_init)
    o = a0[...] / l0[...] - lam * (a1[...] / l1[...])
    ms = jnp.mean(o * o, axis=0, keepdims=True)
    o = o * lax.rsqrt(ms + NORM_EPS) * sub_ref[...] * (1.0 - lam_init)
    o_ref[0] = o.T.astype(BF16)


def _diff_attention(zb, qT, vT, shift, lam_vecs, subln, lam_init, tq, tk):
    b, s, _ = zb.shape
    kern = functools.partial(_diff_kernel, tq=tq, tk=tk, lam_init=lam_init)
    dv = 2 * HEAD_DIM
    return pl.pallas_call(
        kern,
        out_shape=jax.ShapeDtypeStruct((b, s, DIFF_HEADS * dv), BF16),
        grid=(b, DIFF_HEADS, s // tq),
        in_specs=[
            pl.BlockSpec(memory_space=pltpu.SMEM),
            pl.BlockSpec((4, HEAD_DIM), lambda bi, h, i: (0, 0)),
            pl.BlockSpec((1, 1, 2, HEAD_DIM, tq), lambda bi, h, i: (bi, h, 0, 0, i)),
            pl.BlockSpec((1, s, LANES), lambda bi, h, i: (bi, 0, COL_DK // LANES + h)),
            pl.BlockSpec((1, 1, s // tk, 2 * HEAD_DIM, tk), lambda bi, h, i: (bi, h, 0, 0, 0)),
            pl.BlockSpec((2 * HEAD_DIM, 1), lambda bi, h, i: (0, 0)),
        ],
        out_specs=pl.BlockSpec((1, tq, LANES), lambda bi, h, i: (bi, i, h)),
        scratch_shapes=[pltpu.VMEM((1, tq), F32)] * 4 + [pltpu.VMEM((dv, tq), F32)] * 2,
        compiler_params=_cparams(("parallel", "parallel", "arbitrary")),
        name="diff_attention",
    )(shift.reshape(1), lam_vecs, qT, zb, vT, subln.reshape(-1, 1))


def _gelu_tanh(v):
    return 0.5 * v * (1.0 + jnp.tanh(math.sqrt(2.0 / math.pi) * (v + 0.044715 * (v * v * v))))


def _cmp_kernel(c_ref, pos_ref, w1_ref, w2_ref, kg_ref, o_ref, *, n_cmp):
    jh = pl.program_id(1)
    half = CMP_STRIDE * HEAD_DIM
    c = c_ref[0, 0]
    w1a = w1_ref[0, :half, :].astype(BF16)
    w1b = w1_ref[0, half:, :].astype(BF16)
    pos = jnp.broadcast_to(pos_ref[0], (8, 2 * half)).astype(BF16)
    bias = (jnp.dot(pos[:, :half], w1a, preferred_element_type=F32)
            + jnp.dot(pos[:, half:], w1b, preferred_element_type=F32))[0:1]
    u = jnp.dot(c, w1a, preferred_element_type=F32)
    v = jnp.dot(c, w1b, preferred_element_type=F32)
    ncp = u.shape[0]
    hid = _gelu_tanh(u + pltpu.roll(v, ncp - 1, 0) + bias)
    y = jnp.dot(hid.astype(BF16), w2_ref[0].astype(BF16), preferred_element_type=F32)
    yn = y * lax.rsqrt(jnp.mean(y * y, axis=-1, keepdims=True) + NORM_EPS) * kg_ref[...]
    y = jnp.where(jh < NSA_KV_HEADS, yn, y)
    row = lax.broadcasted_iota(I32, y.shape, 0)
    o_ref[0, 0] = jnp.where(row < n_cmp, y, 0.0)


def _compress(chunks, pos, w1, w2, kgain):
    b, _, ncp, cd = chunks.shape
    kern = functools.partial(_cmp_kernel, n_cmp=ncp - 1)
    return pl.pallas_call(
        kern,
        out_shape=jax.ShapeDtypeStruct((b, 4, ncp, HEAD_DIM), F32),
        grid=(b, 4),
        in_specs=[
            pl.BlockSpec((1, 1, ncp, cd), lambda bi, j: (bi, j, 0, 0)),
            pl.BlockSpec((1, 1, 2 * cd), lambda bi, j: (j // NSA_KV_HEADS, 0, 0)),
            pl.BlockSpec((1, 2 * cd, CMP_HIDDEN), lambda bi, j: (j // NSA_KV_HEADS, 0, 0)),
            pl.BlockSpec((1, CMP_HIDDEN, HEAD_DIM), lambda bi, j: (j // NSA_KV_HEADS, 0, 0)),
            pl.BlockSpec((1, HEAD_DIM), lambda bi, j: (0, 0)),
        ],
        out_specs=pl.BlockSpec((1, 1, ncp, HEAD_DIM), lambda bi, j: (bi, j, 0, 0)),
        compiler_params=_cparams(("parallel", "parallel")),
        name="nsa_compress",
    )(chunks, pos, w1, w2, kgain)


def _nsa_kernel(mb_ref, qT_ref, ksel_ref, vselT_ref, kwin_ref, vwinT_ref, kc_ref, vcT_ref, ovl_ref, gate_ref, o_ref,
                selb_sc, m_s, m_w, l_s, l_w, a_s, a_w, *, tq, n_top):
    hk = pl.program_id(1)
    i = pl.program_id(2)
    g4 = NSA_GROUP
    nq = g4 * tq
    q4 = jnp.concatenate([qT_ref[0, 0, g] for g in range(g4)], axis=1)
    half = lax.broadcasted_iota(I32, (2 * HEAD_DIM, nq), 0) // HEAD_DIM
    qp = jnp.where(half == hk, jnp.concatenate([q4, q4], axis=0), jnp.zeros((), BF16))
    t1 = i * tq + lax.broadcasted_iota(I32, (1, tq), 1)

    def tile4(a):
        return jnp.concatenate([a] * g4, axis=1)

    ncp = kc_ref.shape[1]
    sc = jnp.dot(kc_ref[0], qp, preferred_element_type=F32)
    cend = CMP_STRIDE * lax.broadcasted_iota(I32, (ncp, tq), 0) + (CMP_BLOCK - 1)
    sc = sc + tile4(jnp.where(cend <= t1, 0.0, NEG))
    e = jnp.where(sc > 0.5 * NEG, jnp.exp2(sc - jnp.max(sc, axis=0, keepdims=True)), 0.0)
    p = e / jnp.maximum(jnp.sum(e, axis=0, keepdims=True), 1e-30)
    o_c = jnp.dot(vcT_ref[0, 0], p.astype(BF16), preferred_element_type=F32)

    psum = p[:, 0:tq]
    for g in range(1, g4):
        psum = psum + p[:, g * tq:(g + 1) * tq]
    p_hi = psum.astype(BF16)
    p_lo = (psum - p_hi.astype(F32)).astype(BF16)
    imp = (jnp.dot(ovl_ref[...], p_hi, preferred_element_type=F32)
           + jnp.dot(ovl_ref[...], p_lo, preferred_element_type=F32))
    n_sel = imp.shape[0]
    jrow = lax.broadcasted_iota(I32, (n_sel, tq), 0)
    cur = t1 // SEL_BLOCK
    forced = (jrow == 0) | (jrow == cur) | (jrow == cur - 1)
    score = jnp.where(forced, SEL_FORCED_SCORE, jnp.where(jrow <= cur, imp, NEG))
    selb = jnp.full((n_sel, tq), NEG, F32)
    jrow_f = jrow.astype(F32)
    for _ in range(n_top):
        best = jnp.max(score, axis=0, keepdims=True)
        pick = jnp.min(jnp.where(score == best, jrow_f, float(n_sel)), axis=0, keepdims=True)
        hit = jrow_f == pick
        selb = jnp.where(hit, 0.0, selb)
        score = jnp.where(hit, TAKEN, score)
    selb_sc[...] = selb

    for ref in (m_s, m_w):
        ref[...] = jnp.full(ref.shape, NEG, F32)
    for ref in (l_s, l_w, a_s, a_w):
        ref[...] = jnp.zeros(ref.shape, F32)
    tk = tq
    bpt = tk // SEL_BLOCK
    krow = lax.broadcasted_iota(I32, (tk, tq), 0)
    shift_s, shift_w = mb_ref[0], mb_ref[1]
    fixed = jnp.maximum(shift_s, shift_w) <= FIXED_SHIFT_LIMIT

    def sel_tiles(js, masked, fixed):
        ss = []
        for j in js:
            kt = ksel_ref[0, pl.ds(pl.multiple_of(j * tk, tk), tk), :]
            rows = [jnp.broadcast_to(selb_sc[pl.ds(j * bpt + r, 1), :], (SEL_BLOCK, tq)) for r in range(bpt)]
            bias = jnp.concatenate(rows, axis=0)
            if masked:
                bias = jnp.where(j * tk + krow <= t1, bias, NEG)
            ss.append(jnp.dot(kt, qp, preferred_element_type=F32) + tile4(bias))
        vts = [vselT_ref[0, 0, j] for j in js]
        if fixed:
            _fixed_update(ss, vts, shift_s, l_s, a_s)
        else:
            _online_update(ss, vts, m_s, l_s, a_s)

    def win_tiles(j, fixed):
        kt = kwin_ref[0, pl.ds(pl.multiple_of(j * tk, tk), tk), :]
        kpos = j * tk + krow
        bias = jnp.where(kpos <= t1, jnp.where(kpos > t1 - WINDOW, 0.0, NEG), NEG)
        ss = [jnp.dot(kt, qp, preferred_element_type=F32) + tile4(bias)]
        vts = [vwinT_ref[0, 0, j]]
        if fixed:
            _fixed_update(ss, vts, shift_w, l_w, a_w)
        else:
            _online_update(ss, vts, m_w, l_w, a_w)

    j_lo = jnp.maximum(i * tq - (WINDOW - 1), 0) // tk
    for use_fixed in (True, False):
        @pl.when(fixed == use_fixed)
        def _():
            _tile_loops(i, i + 1, functools.partial(sel_tiles, fixed=use_fixed))
            lax.fori_loop(j_lo, i + 1, lambda j, c: (win_tiles(j, use_fixed), c)[1], 0)

    def gate_row(br):
        gt = gate_ref[0, 0, br]
        return jnp.concatenate([gt[g:g + 1] for g in range(g4)], axis=1)

    out = (o_c * gate_row(0) + (a_s[...] / l_s[...]) * gate_row(1) + (a_w[...] / l_w[...]) * gate_row(2))
    stacked = jnp.concatenate([out[:, g * tq:(g + 1) * tq] for g in range(g4)], axis=0)
    o_ref[0] = stacked.T.astype(BF16)


def _nsa_attention(zb, qT, vselT, vwinT, kc, vcT, ovl, gatesT, shifts, tq):
    b, s, _ = zb.shape
    nt = s // tq
    n_sel = s // SEL_BLOCK
    ncp = kc.shape[1]
    kern = functools.partial(_nsa_kernel, tq=tq, n_top=min(SEL_TOP, n_sel))
    gd = NSA_GROUP * HEAD_DIM
    nq = NSA_GROUP * tq
    return pl.pallas_call(
        kern,
        out_shape=jax.ShapeDtypeStruct((b, s, NSA_Q_HEADS * HEAD_DIM), BF16),
        grid=(b, NSA_KV_HEADS, nt),
        in_specs=[
            pl.BlockSpec(memory_space=pltpu.SMEM),
            pl.BlockSpec((1, 1, NSA_GROUP, HEAD_DIM, tq), lambda bi, h, i: (bi, h, 0, 0, i)),
            pl.BlockSpec((1, s, LANES), lambda bi, h, i: (bi, 0, COL_KSEL // LANES)),
            pl.BlockSpec((1, 1, nt, HEAD_DIM, tq), lambda bi, h, i: (bi, h, 0, 0, 0)),
            pl.BlockSpec((1, s, LANES), lambda bi, h, i: (bi, 0, COL_KWIN // LANES)),
            pl.BlockSpec((1, 1, nt, HEAD_DIM, tq), lambda bi, h, i: (bi, h, 0, 0, 0)),
            pl.BlockSpec((1, ncp, LANES), lambda bi, h, i: (bi, 0, 0)),
            pl.BlockSpec((1, 1, HEAD_DIM, ncp), lambda bi, h, i: (bi, h, 0, 0)),
            pl.BlockSpec((n_sel, ncp), lambda bi, h, i: (0, 0)),
            pl.BlockSpec((1, 1, 3, NSA_GROUP, tq), lambda bi, h, i: (bi, h, 0, 0, i)),
        ],
        out_specs=pl.BlockSpec((1, tq, gd), lambda bi, h, i: (bi, i, h)),
        scratch_shapes=([pltpu.VMEM((n_sel, tq), F32)] + [pltpu.VMEM((1, nq), F32)] * 4
                        + [pltpu.VMEM((HEAD_DIM, nq), F32)] * 2),
        compiler_params=_cparams(("parallel", "parallel", "arbitrary")),
        name="nsa_attention",
    )(shifts, qT, zb, vselT, zb, vwinT, kc, vcT, ovl, gatesT)


def _in_conv_kernel(x_ref, mod_ref, g_ref, w_ref, b_ref, u_ref):
    m = mod_ref[0]
    d = x_ref.shape[-1]
    h = _norm_mod(x_ref[0], g_ref[...], m[1:2], m[0:1]).astype(BF16)
    a = jnp.dot(h, w_ref[:, :d], preferred_element_type=F32) + b_ref[:, :d]
    g = jnp.dot(h, w_ref[:, d:], preferred_element_type=F32) + b_ref[:, d:]
    u_ref[0] = a * _sigmoid(g)


def _conv_in_proj(x, mod, gain, w, bias, tm):
    b, s, d = x.shape
    return pl.pallas_call(
        _in_conv_kernel,
        out_shape=jax.ShapeDtypeStruct((b, s, d), F32),
        grid=(b, s // tm),
        in_specs=[
            pl.BlockSpec((1, tm, d), lambda i, j: (i, j, 0)),
            pl.BlockSpec((1, 6, d), lambda i, j: (i, 0, 0)),
            pl.BlockSpec((1, d), lambda i, j: (0, 0)),
            pl.BlockSpec((d, 2 * d), lambda i, j: (0, 0)),
            pl.BlockSpec((1, 2 * d), lambda i, j: (0, 0)),
        ],
        out_specs=pl.BlockSpec((1, tm, d), lambda i, j: (i, j, 0)),
        compiler_params=_cparams(("parallel", "parallel")),
        name="conv_in_proj",
    )(x, mod, gain, w, bias)


HALO = 32


def _dwconv_kernel(u_ref, halo_ref, w_ref, b_ref, lg_ref, lb_ref, o_ref, buf_sc, acc_sc, *, tm):
    i = pl.program_id(1)
    d = u_ref.shape[-1]
    buf_sc[0:HALO, :] = jnp.where(i > 0, halo_ref[0], 0.0)
    buf_sc[HALO:, :] = u_ref[0]
    off = HALO - (CONV_WIDTH - 1)
    cw = 256
    for c in range(d // cw):
        cs = slice(c * cw, (c + 1) * cw)
        acc = jnp.zeros((tm, cw), F32) + b_ref[:, cs]
        for j in range(CONV_WIDTH):
            acc = acc + w_ref[j:j + 1, cs] * buf_sc[off + j:off + j + tm, cs]
        acc_sc[:, cs] = acc
    y = acc_sc[...]
    mu = jnp.mean(y, axis=-1, keepdims=True)
    yc = y - mu
    var = jnp.mean(yc * yc, axis=-1, keepdims=True)
    yn = yc * lax.rsqrt(var + NORM_EPS) * lg_ref[...] + lb_ref[...]
    o_ref[0] = (yn * _sigmoid(yn)).astype(BF16)


def _dwconv_ln_swish(u, dw_w, dw_b, ln_g, ln_b, tm):
    b, s, d = u.shape
    kern = functools.partial(_dwconv_kernel, tm=tm)
    hb = tm // HALO
    return pl.pallas_call(
        kern,
        out_shape=jax.ShapeDtypeStruct((b, s, d), BF16),
        grid=(b, s // tm),
        in_specs=[
            pl.BlockSpec((1, tm, d), lambda bi, i: (bi, i, 0)),
            pl.BlockSpec((1, HALO, d), lambda bi, i: (bi, jnp.maximum(i * hb - 1, 0), 0)),
            pl.BlockSpec((CONV_WIDTH, d), lambda bi, i: (0, 0)),
            pl.BlockSpec((1, d), lambda bi, i: (0, 0)),
            pl.BlockSpec((1, d), lambda bi, i: (0, 0)),
            pl.BlockSpec((1, d), lambda bi, i: (0, 0)),
        ],
        out_specs=pl.BlockSpec((1, tm, d), lambda bi, i: (bi, i, 0)),
        scratch_shapes=[pltpu.VMEM((tm + HALO, d), F32), pltpu.VMEM((tm, d), F32)],
        compiler_params=_cparams(("parallel", "parallel")),
        name="dwconv_ln_swish",
    )(u, u, dw_w, dw_b, ln_g, ln_b)


def _out_router_kernel(a1_ref, a2_ref, w1_ref, w2_ref, b_ref, x_ref, mod_ref, g_ref, rwh_ref, rwl_ref, rb_ref,
                       xo_ref, h_ref, idx_ref, wt_ref):
    m = mod_ref[0]
    y = (jnp.dot(a1_ref[0], w1_ref[...], preferred_element_type=F32)
         + jnp.dot(a2_ref[0], w2_ref[...], preferred_element_type=F32) + b_ref[...])
    x = x_ref[0] + m[2:3] * y
    xo_ref[0] = x
    h = _norm_mod(x, g_ref[...], m[4:5], m[3:4])
    for c in range(h_ref.shape[2]):
        h_ref[0, :, c, :] = h[:, c * LANES:(c + 1) * LANES]
    h_hi = h.astype(BF16)
    h_lo = (h - h_hi.astype(F32)).astype(BF16)
    logits = (jnp.dot(h_hi, rwh_ref[...], preferred_element_type=F32)
              + jnp.dot(h_lo, rwh_ref[...], preferred_element_type=F32)
              + jnp.dot(h_hi, rwl_ref[...], preferred_element_type=F32)) + rb_ref[...]
    lane = lax.broadcasted_iota(I32, logits.shape, 1)
    lane_f = lane.astype(F32)
    idx_out = jnp.zeros(logits.shape, F32)
    val_out = jnp.full(logits.shape, NEG, F32)
    for k in range(TOP_K):
        best = jnp.max(logits, axis=-1, keepdims=True)
        pick = jnp.min(jnp.where(logits == best, lane_f, float(LANES)), axis=-1, keepdims=True)
        idx_out = jnp.where(lane == k, pick, idx_out)
        val_out = jnp.where(lane == k, best, val_out)
        logits = jnp.where(lane_f == pick, TAKEN, logits)
    e = jnp.where(lane < TOP_K, jnp.exp(val_out - jnp.max(val_out, axis=-1, keepdims=True)), 0.0)
    idx_ref[0] = idx_out.astype(I32)
    wt_ref[0] = e / jnp.sum(e, axis=-1, keepdims=True)


def _out_proj_router(a1, a2, c1, c2, w1, w2, bias, x, mod, gain, rw_hi, rw_lo, rb, tm):
    b, s, d = x.shape
    kw = w1.shape[0]
    tok = lambda i, j: (i, j, 0)
    const = lambda i, j: (0, 0)
    return pl.pallas_call(
        _out_router_kernel,
        out_shape=(jax.ShapeDtypeStruct((b, s, d), F32), jax.ShapeDtypeStruct((b, s, d // LANES, LANES), F32),
                   jax.ShapeDtypeStruct((b, s, LANES), I32), jax.ShapeDtypeStruct((b, s, LANES), F32)),
        grid=(b, s // tm),
        in_specs=[
            pl.BlockSpec((1, tm, kw), lambda i, j: (i, j, c1)),
            pl.BlockSpec((1, tm, kw), lambda i, j: (i, j, c2)),
            pl.BlockSpec((kw, d), const),
            pl.BlockSpec((kw, d), const),
            pl.BlockSpec((1, d), const),
            pl.BlockSpec((1, tm, d), tok),
            pl.BlockSpec((1, 6, d), lambda i, j: (i, 0, 0)),
            pl.BlockSpec((1, d), const),
            pl.BlockSpec((d, LANES), const),
            pl.BlockSpec((d, LANES), const),
            pl.BlockSpec((1, LANES), const),
        ],
        out_specs=(pl.BlockSpec((1, tm, d), tok), pl.BlockSpec((1, tm, d // LANES, LANES), lambda i, j: (i, j, 0, 0)),
                   pl.BlockSpec((1, tm, LANES), tok), pl.BlockSpec((1, tm, LANES), tok)),
        compiler_params=_cparams(("parallel", "parallel")),
        name="out_proj_router",
    )(a1, a2, w1, w2, bias, x, mod, gain, rw_hi, rw_lo, rb)


ROW_CHUNKS = D_MODEL // LANES
DMA_UNROLL = 8


def _dispatch_kernel(pos_ref, h_ref, xs_ref, sem, *, tm):
    def start(a, c):
        pltpu.make_async_copy(h_ref.at[a // TOP_K], xs_ref.at[pos_ref[0, 0, a]], sem).start()
        return c

    lax.fori_loop(0, tm * TOP_K, start, 0, unroll=DMA_UNROLL)
    for _ in range(TOP_K):
        pltpu.make_async_copy(h_ref, xs_ref.at[pl.ds(0, tm)], sem).wait()


def _dispatch(h, pos, n_slots, tm):
    t = h.shape[0]
    kern = functools.partial(_dispatch_kernel, tm=tm)
    return pl.pallas_call(
        kern,
        out_shape=jax.ShapeDtypeStruct((n_slots, ROW_CHUNKS, LANES), F32),
        grid=(t // tm,),
        in_specs=[
            pl.BlockSpec((1, 1, tm * TOP_K), lambda i: (i, 0, 0), memory_space=pltpu.SMEM),
            pl.BlockSpec((tm, ROW_CHUNKS, LANES), lambda i: (i, 0, 0)),
        ],
        out_specs=pl.BlockSpec(memory_space=pl.ANY),
        scratch_shapes=[pltpu.SemaphoreType.DMA(())],
        compiler_params=_cparams(("arbitrary",)),
        name="moe_dispatch",
    )(pos.reshape(t // tm, 1, tm * TOP_K), h)


DEINT = 512


def _expert_kernel(be_ref, nu_ref, nv_ref, xs_ref, w1_ref, b1g_ref, b1l_ref, w2_ref, b2_ref, pe_ref, po_ref, ys_ref,
                   w1g_sc, w1l_sc, w2_sc):
    i = pl.program_id(0)

    @pl.when(i < nu_ref[0])
    def _():
        changed = jnp.logical_or(i == 0, be_ref[i] != be_ref[jnp.maximum(i - 1, 0)])

        @pl.when(changed)
        def _():
            for c in range(w1_ref.shape[2] // DEINT):
                wc = w1_ref[0, :, c * DEINT:(c + 1) * DEINT].astype(BF16)
                cs = slice(c * (DEINT // 2), (c + 1) * (DEINT // 2))
                w1g_sc[:, cs] = jnp.dot(wc, pe_ref[...], preferred_element_type=F32).astype(BF16)
                w1l_sc[:, cs] = jnp.dot(wc, po_ref[...], preferred_element_type=F32).astype(BF16)
            w2_sc[...] = w2_ref[0].astype(BF16)

        x = jnp.concatenate([xs_ref[:, c, :] for c in range(ROW_CHUNKS)], axis=1)
        row = lax.broadcasted_iota(I32, x.shape, 0)
        xb = jnp.where(row < nv_ref[i], x, 0.0).astype(BF16)
        glu = jnp.dot(xb, w1g_sc[...], preferred_element_type=F32) + b1g_ref[0]
        lin = jnp.dot(xb, w1l_sc[...], preferred_element_type=F32) + b1l_ref[0]
        glu = jnp.minimum(glu, SWIGLU_LIMIT)
        lin = jnp.clip(lin, -SWIGLU_LIMIT, SWIGLU_LIMIT)
        act = glu * _sigmoid(SWIGLU_ALPHA * glu) * (lin + 1.0)
        y = jnp.dot(act.astype(BF16), w2_sc[...], preferred_element_type=F32) + b2_ref[0]
        for c in range(ROW_CHUNKS):
            ys_ref[:, c, :] = y[:, c * LANES:(c + 1) * LANES]


def _experts(xs, block_expert, n_used, n_valid, w1, b1g, b1l, w2, b2):
    n_slots = xs.shape[0]
    n_blocks = n_slots // MOE_BLOCK
    d, f2 = w1.shape[1:]
    f = f2 // 2
    sel = np.arange(DEINT)[:, None] - 2 * np.arange(DEINT // 2)[None, :]
    p_even = jnp.asarray((sel == 0).astype(np.float32), dtype=BF16)
    p_odd = jnp.asarray((sel == 1).astype(np.float32), dtype=BF16)

    def blk(i, be, nu, nv):
        return (jnp.minimum(i, nu[0] - 1), 0, 0)

    def wsel(i, be, nu, nv):
        return (be[jnp.minimum(i, nu[0] - 1)], 0, 0)

    const = lambda i, be, nu, nv: (0, 0)
    return pl.pallas_call(
        _expert_kernel,
        out_shape=jax.ShapeDtypeStruct((n_slots, ROW_CHUNKS, LANES), F32),
        grid_spec=pltpu.PrefetchScalarGridSpec(
            num_scalar_prefetch=3,
            grid=(n_blocks,),
            in_specs=[
                pl.BlockSpec((MOE_BLOCK, ROW_CHUNKS, LANES), blk),
                pl.BlockSpec((1, d, f2), wsel),
                pl.BlockSpec((1, 1, f), wsel),
                pl.BlockSpec((1, 1, f), wsel),
                pl.BlockSpec((1, f, d), wsel),
                pl.BlockSpec((1, 1, d), wsel),
                pl.BlockSpec((DEINT, DEINT // 2), const),
                pl.BlockSpec((DEINT, DEINT // 2), const),
            ],
            out_specs=pl.BlockSpec((MOE_BLOCK, ROW_CHUNKS, LANES), blk),
            scratch_shapes=[pltpu.VMEM((d, f), BF16), pltpu.VMEM((d, f), BF16), pltpu.VMEM((f, d), BF16)],
        ),
        compiler_params=_cparams(("arbitrary",)),
        name="moe_experts",
    )(block_expert, n_used, n_valid, xs, w1, b1g, b1l, w2, b2, p_even, p_odd)


def _combine_kernel(pos_ref, ys_ref, wt_ref, x_ref, mod_ref, o_ref, buf, sem, *, tm):
    def start(a, c):
        pltpu.make_async_copy(ys_ref.at[pos_ref[0, 0, a]], buf.at[a % TOP_K, a // TOP_K], sem).start()
        return c

    lax.fori_loop(0, tm * TOP_K, start, 0, unroll=DMA_UNROLL)
    for k in range(TOP_K):
        pltpu.make_async_copy(ys_ref.at[pl.ds(0, tm)], buf.at[k], sem).wait()
    wt = wt_ref[0]
    g2 = mod_ref[0][5:6]
    wk = [wt[:, k:k + 1] for k in range(TOP_K)]
    for c in range(ROW_CHUNKS):
        y = wk[0] * buf[0, :, c, :]
        for k in range(1, TOP_K):
            y = y + wk[k] * buf[k, :, c, :]
        cs = slice(c * LANES, (c + 1) * LANES)
        o_ref[0, :, cs] = x_ref[0, :, cs] + g2[:, cs] * y


def _combine(ys, pos, wts, x, mod, tm):
    b, s, d = x.shape
    t = b * s
    nt = s // tm
    kern = functools.partial(_combine_kernel, tm=tm)
    tok = lambda i, j: (i, j, 0)
    return pl.pallas_call(
        kern,
        out_shape=jax.ShapeDtypeStruct((b, s, d), F32),
        grid=(b, nt),
        in_specs=[
            pl.BlockSpec((1, 1, tm * TOP_K), lambda i, j: (i * nt + j, 0, 0), memory_space=pltpu.SMEM),
            pl.BlockSpec(memory_space=pl.ANY),
            pl.BlockSpec((1, tm, LANES), tok),
            pl.BlockSpec((1, tm, d), tok),
            pl.BlockSpec((1, 6, d), lambda i, j: (i, 0, 0)),
        ],
        out_specs=pl.BlockSpec((1, tm, d), tok),
        scratch_shapes=[pltpu.VMEM((TOP_K, tm, ROW_CHUNKS, LANES), F32), pltpu.SemaphoreType.DMA(())],
        compiler_params=_cparams(("arbitrary", "arbitrary")),
        name="moe_combine",
    )(pos.reshape(t // tm, 1, tm * TOP_K), ys, wts, x, mod)


def _route_positions(top_idx, n_blocks):
    flat_e = top_idx.reshape(-1)
    onehot = (flat_e[:, None] == jnp.arange(N_EXPERTS, dtype=I32)[None, :]).astype(I32)
    csum = jnp.cumsum(onehot, axis=0)
    rank = jnp.take_along_axis(csum, flat_e[:, None], axis=1)[:, 0] - 1
    counts = csum[-1]
    padded = (counts + MOE_BLOCK - 1) // MOE_BLOCK * MOE_BLOCK
    pad_end = jnp.cumsum(padded)
    pad_start = pad_end - padded
    pos = (pad_start[flat_e] + rank).astype(I32)
    blk_start = jnp.arange(n_blocks, dtype=I32) * MOE_BLOCK
    block_expert = jnp.minimum(jnp.searchsorted(pad_end, blk_start, side='right'), N_EXPERTS - 1).astype(I32)
    n_valid = jnp.clip(pad_start[block_expert] + counts[block_expert] - blk_start, 0, MOE_BLOCK).astype(I32)
    n_used = (pad_end[-1] // MOE_BLOCK).astype(I32).reshape(1)
    return pos, block_expert, n_valid, n_used


def _moe(x_mid, h, idx, wts, mod, w1, b1g, b1l, w2, b2, tm):
    b, s, d = x_mid.shape
    t = b * s
    n_blocks = -(-t * TOP_K // MOE_BLOCK) + N_EXPERTS
    top_idx = idx.reshape(t, LANES)[:, :TOP_K]
    pos, block_expert, n_valid, n_used = _route_positions(top_idx, n_blocks)
    xs = _dispatch(h.reshape(t, ROW_CHUNKS, LANES), pos, n_blocks * MOE_BLOCK, tm)
    ys = _experts(xs, block_expert, n_used, n_valid, w1, b1g, b1l, w2, b2)
    return _combine(ys, pos, wts, x_mid, mod, tm)


def _pad_cols(w, n):
    return jnp.pad(w, ((0, 0), (0, n - w.shape[-1])))


def _attn_colgain(diff_qk_gain, nsa_q_gain, nsa_k_gain):
    scale = HEAD_DIM ** -0.5 * LOG2E
    ones = jnp.ones((LANES,), F32)
    parts = [jnp.tile(diff_qk_gain[0] * scale, 8), jnp.tile(diff_qk_gain[1], 8), jnp.ones((512,), F32),
             jnp.tile(nsa_q_gain * scale, 8), ones, ones, jnp.tile(nsa_k_gain[1], 2), ones,
             jnp.tile(nsa_k_gain[2], 2), ones, ones]
    return jnp.concatenate(parts).reshape(1, W_IN_PAD)


def _overlap_matrix(s):
    n_sel = s // SEL_BLOCK
    ncp = s // CMP_STRIDE
    c_start = np.arange(ncp) * CMP_STRIDE
    s_start = np.arange(n_sel) * SEL_BLOCK
    ovl = (c_start[None, :] < s_start[:, None] + SEL_BLOCK) & (c_start[None, :] + CMP_BLOCK > s_start[:, None])
    ovl[:, ncp - 1] = False
    return jnp.asarray(ovl.astype(np.float32), dtype=BF16)


def _attention_layer(x, mod, gain, w_in, w_out, diff_qk_gain, diff_lambda, diff_subln, nsa_q_gain, nsa_k_gain,
                     cmp_pos, cmp_w1, cmp_w2, lam_init):
    b, s, d = x.shape
    tq_d, tk_d, tq_n = 256, 512, 256
    zb, gsig = _attn_in_proj(x, mod, gain, _pad_cols(w_in, W_IN_PAD).astype(BF16),
                             _attn_colgain(diff_qk_gain, nsa_q_gain, nsa_k_gain), 512)

    def cols(c0, n):
        return zb[:, :, c0:c0 + n]

    dqT = cols(COL_DQ, 512).reshape(b, s, DIFF_HEADS, 2, HEAD_DIM).transpose(0, 2, 3, 4, 1)
    dvT = cols(COL_DV, 512).reshape(b, s // tk_d, tk_d, DIFF_HEADS, 2 * HEAD_DIM).transpose(0, 3, 1, 4, 2)
    o_diff = _diff_attention(zb, dqT, dvT, _score_bound(diff_qk_gain[0], diff_qk_gain[1]), diff_lambda, diff_subln,
                             lam_init, tq_d, tk_d)

    nqT = cols(COL_NQ, 512).reshape(b, s, NSA_KV_HEADS, NSA_GROUP, HEAD_DIM).transpose(0, 2, 3, 4, 1)

    def val_t(c0):
        return cols(c0, LANES).reshape(b, s // tq_n, tq_n, NSA_KV_HEADS, HEAD_DIM).transpose(0, 3, 1, 4, 2)

    ncp = s // CMP_STRIDE
    chunks = cols(COL_KCMP, 2 * LANES).reshape(b, ncp, CMP_STRIDE, 2, NSA_KV_HEADS, HEAD_DIM)
    chunks = chunks.transpose(0, 3, 4, 1, 2, 5).reshape(b, 4, ncp, CMP_STRIDE * HEAD_DIM)
    cmp_out = _compress(chunks, cmp_pos.reshape(2, 1, CMP_BLOCK * HEAD_DIM), cmp_w1, cmp_w2, nsa_k_gain[0:1])
    kc = cmp_out[:, 0:2].transpose(0, 2, 1, 3).reshape(b, ncp, LANES).astype(BF16)
    vcT = cmp_out[:, 2:4].transpose(0, 1, 3, 2).astype(BF16)
    gatesT = gsig[:, :, :NSA_Q_HEADS * 3].reshape(b, s, NSA_KV_HEADS, NSA_GROUP, 3).transpose(0, 2, 4, 3, 1)
    shifts = jnp.stack([_score_bound(nsa_q_gain, nsa_k_gain[1]), _score_bound(nsa_q_gain, nsa_k_gain[2])])
    o_nsa = _nsa_attention(zb, nqT, val_t(COL_VSEL), val_t(COL_VWIN), kc, vcT, _overlap_matrix(s), gatesT, shifts,
                           tq_n)
    return o_diff, o_nsa


def kernel(x, c, mod_w, mod_b, norm_mix, norm_ffn, attn_w_in, attn_w_out, diff_qk_gain, diff_lambda, diff_subln,
           nsa_q_gain, nsa_k_gain, nsa_cmp_pos, nsa_cmp_w1, nsa_cmp_w2, conv_pw1_w, conv_pw1_b, conv_dw_w,
           conv_dw_b, conv_ln_g, conv_ln_b, conv_pw2_w, conv_pw2_b, router_w, router_b, moe_w1, moe_b1, moe_w2,
           moe_b2):
    b, s, d = x.shape
    depth = mod_w.shape[0]
    tm = 512 if s % 512 == 0 else 256
    mods = _modulation(c, mod_w, mod_b).reshape(depth, b, 6, d)
    half = d // 2
    for i in range(depth):
        mod = mods[i]
        j = i // 2
        if i % 2 == 0:
            lam_init = 0.8 - 0.6 * math.exp(-0.3 * i)
            o_diff, o_nsa = _attention_layer(
                x, mod, norm_mix[i:i + 1], attn_w_in[j], attn_w_out[j], diff_qk_gain[j], diff_lambda[j],
                diff_subln[j], nsa_q_gain[j], nsa_k_gain[j], nsa_cmp_pos[j], nsa_cmp_w1[j], nsa_cmp_w2[j], lam_init)
            a1, a2, c1, c2 = o_diff, o_nsa, 0, 0
            w_o = attn_w_out[j].astype(BF16)
            bias = jnp.zeros((1, d), F32)
        else:
            u = _conv_in_proj(x, mod, norm_mix[i:i + 1], conv_pw1_w[j].astype(BF16), conv_pw1_b[j:j + 1], tm)
            v = _dwconv_ln_swish(u, conv_dw_w[j].reshape(CONV_WIDTH, d), conv_dw_b[j:j + 1], conv_ln_g[j:j + 1],
                                 conv_ln_b[j:j + 1], 256)
            a1, a2, c1, c2 = v, v, 0, 1
            w_o = conv_pw2_w[j].astype(BF16)
            bias = conv_pw2_b[j:j + 1]
        rw = _pad_cols(router_w[i], LANES)
        rw_hi = rw.astype(BF16)
        rw_lo = (rw - rw_hi.astype(F32)).astype(BF16)
        rb = jnp.concatenate([router_b[i], jnp.full((LANES - N_EXPERTS,), NEG, F32)]).reshape(1, LANES)
        x_mid, h, idx, wts = _out_proj_router(a1, a2, c1, c2, w_o[:half], w_o[half:], bias, x, mod,
                                              norm_ffn[i:i + 1], rw_hi, rw_lo, rb, tm)
        x = _moe(x_mid, h, idx, wts, mod, moe_w1[i], moe_b1[i][:, None, 0::2], moe_b1[i][:, None, 1::2], moe_w2[i],
                 moe_b2[i][:, None, :], 256)
    return x
```

```python
import functools
import math

import jax
import jax.numpy as jnp
import numpy as np
from jax import lax
from jax.experimental import pallas as pl
from jax.experimental.pallas import tpu as pltpu

F32 = jnp.float32
BF16 = jnp.bfloat16
I32 = jnp.int32

D_MODEL = 1024
HEAD_DIM = 64
DIFF_HEADS = 4
NSA_Q_HEADS = 8
NSA_KV_HEADS = 2
NSA_GROUP = NSA_Q_HEADS // NSA_KV_HEADS
CMP_BLOCK = 32
CMP_STRIDE = 16
CMP_HIDDEN = 256
SEL_BLOCK = 64
SEL_TOP = 16
SEL_FORCED_SCORE = 1.0e4
WINDOW = 512
CONV_WIDTH = 31
N_EXPERTS = 32
TOP_K = 4
SWIGLU_ALPHA = 1.702
SWIGLU_LIMIT = 7.0
MOE_BLOCK = 256
NORM_EPS = 1e-6

LANES = 128
NEG = -1e30
TAKEN = -3e38
VMEM_LIMIT = 56 * 1024 * 1024

COL_DQ, COL_DK, COL_DV, COL_NQ = 0, 512, 1024, 1536
COL_KCMP, COL_VCMP, COL_KSEL, COL_VSEL, COL_KWIN, COL_VWIN = 2048, 2176, 2304, 2432, 2560, 2688
COL_GATE = 2816
W_IN_PAD = 2944
NORM_SLABS = frozenset(list(range(0, 8)) + list(range(12, 16)) + [COL_KSEL // LANES, COL_KWIN // LANES])


def _cparams(sem):
    return pltpu.CompilerParams(dimension_semantics=sem, vmem_limit_bytes=VMEM_LIMIT)


def _sigmoid(v):
    return 1.0 / (1.0 + jnp.exp(-v))


def _norm_mod(x, g, sc, sh):
    ms = jnp.mean(x * x, axis=-1, keepdims=True)
    return (x * lax.rsqrt(ms + NORM_EPS)) * g * (1.0 + sc) + sh


def _mod_kernel(c_ref, w_ref, b_ref, o_ref):
    c = c_ref[...]
    cond = c * _sigmoid(c)
    o_ref[0] = jnp.dot(cond, w_ref[0], preferred_element_type=F32, precision=lax.Precision.HIGHEST) + b_ref[0]


def _modulation(c, mod_w, mod_b):
    depth, d, n = mod_w.shape
    b = c.shape[0]
    tn = 1536
    return pl.pallas_call(
        _mod_kernel,
        out_shape=jax.ShapeDtypeStruct((depth, b, n), F32),
        grid=(depth, n // tn),
        in_specs=[
            pl.BlockSpec((b, d), lambda i, j: (0, 0)),
            pl.BlockSpec((1, d, tn), lambda i, j: (i, 0, j)),
            pl.BlockSpec((1, 1, tn), lambda i, j: (i, 0, j)),
        ],
        out_specs=pl.BlockSpec((1, b, tn), lambda i, j: (i, 0, j)),
        compiler_params=_cparams(("parallel", "parallel")),
        name="modulation",
    )(c, mod_w, mod_b.reshape(depth, 1, n))


def _in_attn_kernel(x_ref, mod_ref, g_ref, w_ref, cg_ref, z_ref, gate_ref):
    m = mod_ref[0]
    h = _norm_mod(x_ref[0], g_ref[...], m[1:2], m[0:1]).astype(BF16)
    lo = lax.broadcasted_iota(I32, (1, LANES), 1) < HEAD_DIM
    groups = ((0, 512), (512, 1024), (1024, 1536), (1536, 2048), (2048, 2816), (2816, 2944))
    for c0, c1 in groups:
        z = jnp.dot(h, w_ref[:, c0:c1], preferred_element_type=F32)
        for s in range((c1 - c0) // LANES):
            a0 = c0 + s * LANES
            zs = z[:, s * LANES:(s + 1) * LANES]
            if a0 == COL_GATE:
                gate_ref[0] = _sigmoid(zs)
                continue
            if a0 // LANES in NORM_SLABS:
                z2 = zs * zs
                s_lo = jnp.sum(jnp.where(lo, z2, 0.0), axis=-1, keepdims=True)
                s_hi = jnp.sum(jnp.where(lo, 0.0, z2), axis=-1, keepdims=True)
                inv = jnp.where(lo, lax.rsqrt(s_lo * (1.0 / HEAD_DIM) + NORM_EPS),
                                lax.rsqrt(s_hi * (1.0 / HEAD_DIM) + NORM_EPS))
                zs = zs * inv
            z_ref[0, :, a0:a0 + LANES] = (zs * cg_ref[:, a0:a0 + LANES]).astype(BF16)


def _attn_in_proj(x, mod, gain, w_pad, colgain, tm):
    b, s, d = x.shape
    return pl.pallas_call(
        _in_attn_kernel,
        out_shape=(jax.ShapeDtypeStruct((b, s, COL_GATE), BF16), jax.ShapeDtypeStruct((b, s, LANES), F32)),
        grid=(b, s // tm),
        in_specs=[
            pl.BlockSpec((1, tm, d), lambda i, j: (i, j, 0)),
            pl.BlockSpec((1, 6, d), lambda i, j: (i, 0, 0)),
            pl.BlockSpec((1, d), lambda i, j: (0, 0)),
            pl.BlockSpec((d, W_IN_PAD), lambda i, j: (0, 0)),
            pl.BlockSpec((1, W_IN_PAD), lambda i, j: (0, 0)),
        ],
        out_specs=(pl.BlockSpec((1, tm, COL_GATE), lambda i, j: (i, j, 0)),
                   pl.BlockSpec((1, tm, LANES), lambda i, j: (i, j, 0))),
        compiler_params=_cparams(("parallel", "parallel")),
        name="attn_in_proj",
    )(x, mod, gain, w_pad, colgain)


FIXED_SHIFT_LIMIT = 57.0
FIXED_TILES_PER_STEP = 4
LOG2E = math.log2(math.e)


def _score_bound(gain_q, gain_k):
    return (math.sqrt(HEAD_DIM) * LOG2E) * jnp.max(jnp.abs(gain_q * gain_k))


def _fixed_update(ss, vts, shift, l_ref, acc_ref):
    ps = [jnp.exp2(s - shift) for s in ss]
    l_ref[...] += sum(jnp.sum(p, axis=0, keepdims=True) for p in ps)
    acc_ref[...] += sum(jnp.dot(vt, p.astype(BF16), preferred_element_type=F32) for vt, p in zip(vts, ps))


def _online_update(ss, vts, m_ref, l_ref, acc_ref):
    for s, vt in zip(ss, vts):
        m_old = m_ref[...]
        m_new = jnp.maximum(m_old, jnp.max(s, axis=0, keepdims=True))
        alpha = jnp.exp2(m_old - m_new)
        p = jnp.exp2(s - m_new)
        l_ref[...] = alpha * l_ref[...] + jnp.sum(p, axis=0, keepdims=True)
        acc_ref[...] = alpha * acc_ref[...] + jnp.dot(vt, p.astype(BF16), preferred_element_type=F32)
        m_ref[...] = m_new


def _tile_loops(n_full, n_all, tiles, width):
    def group(jj, c):
        tiles(tuple(width * jj + u for u in range(width)), False)
        return c

    lax.fori_loop(0, n_full // width, group, 0)
    base = n_full - n_full % width
    w = width // 2
    while w >= 1:
        take = (n_full % (2 * w)) >= w

        @pl.when(take)
        def _():
            tiles(tuple(base + u for u in range(w)), False)

        base = base + jnp.where(take, w, 0)
        w //= 2

    def single(j, c):
        tiles((j,), True)
        return c

    lax.fori_loop(n_full, n_all, single, 0)


def _diff_kernel(mb_ref, lam_ref, qT_ref, k_ref, vT_ref, sub_ref, o_ref, m0, m1, l0, l1, a0, a1, *, tq, tk, lam_init):
    i = pl.program_id(2)
    m_refs, l_refs, acc_refs = (m0, m1), (l0, l1), (a0, a1)
    q = qT_ref[0, 0]
    zero = jnp.zeros((HEAD_DIM, tq), BF16)
    qp = (jnp.concatenate([q[0], zero], axis=0), jnp.concatenate([zero, q[1]], axis=0))
    for mm in range(2):
        m_refs[mm][...] = jnp.full(m_refs[mm].shape, NEG, F32)
        l_refs[mm][...] = jnp.zeros(l_refs[mm].shape, F32)
        acc_refs[mm][...] = jnp.zeros(acc_refs[mm].shape, F32)
    shift = mb_ref[0]

    def tiles(js, masked, fixed):
        kts = [k_ref[0, pl.ds(pl.multiple_of(j * tk, tk), tk), :] for j in js]
        vts = [vT_ref[0, 0, j] for j in js]
        for mm in range(2):
            ss = []
            for j, kt in zip(js, kts):
                s = jnp.dot(kt, qp[mm], preferred_element_type=F32)
                if masked:
                    kpos = j * tk + lax.broadcasted_iota(I32, (tk, tq), 0)
                    t = i * tq + lax.broadcasted_iota(I32, (tk, tq), 1)
                    s = jnp.where(kpos <= t, s, NEG)
                ss.append(s)
            if fixed:
                _fixed_update(ss, vts, shift, l_refs[mm], acc_refs[mm])
            else:
                _online_update(ss, vts, m_refs[mm], l_refs[mm], acc_refs[mm])

    n_full = (i * tq + 1) // tk
    n_kv = ((i + 1) * tq + tk - 1) // tk

    @pl.when(shift <= FIXED_SHIFT_LIMIT)
    def _():
        _tile_loops(n_full, n_kv, functools.partial(tiles, fixed=True), FIXED_TILES_PER_STEP)

    @pl.when(shift > FIXED_SHIFT_LIMIT)
    def _():
        _tile_loops(n_full, n_kv, functools.partial(tiles, fixed=False), 1)

    lv = lam_ref[...]
    lam = (jnp.exp(jnp.sum(lv[0:1] * lv[1:2], axis=-1, keepdims=True))
           - jnp.exp(jnp.sum(lv[2:3] * lv[3:4], axis=-1, keepdims=True)) + lam_init)
    o = a0[...] / l0[...] - lam * (a1[...] / l1[...])
    ms = jnp.mean(o * o, axis=0, keepdims=True)
    o = o * lax.rsqrt(ms + NORM_EPS) * sub_ref[...] * (1.0 - lam_init)
    o_ref[0] = o.T.astype(BF16)


def _diff_attention(zb, qT, vT, shift, lam_vecs, subln, lam_init, tq, tk):
    b, s, _ = zb.shape
    kern = functools.partial(_diff_kernel, tq=tq, tk=tk, lam_init=lam_init)
    dv = 2 * HEAD_DIM
    return pl.pallas_call(
        kern,
        out_shape=jax.ShapeDtypeStruct((b, s, DIFF_HEADS * dv), BF16),
        grid=(b, DIFF_HEADS, s // tq),
        in_specs=[
            pl.BlockSpec(memory_space=pltpu.SMEM),
            pl.BlockSpec((4, HEAD_DIM), lambda bi, h, i: (0, 0)),
            pl.BlockSpec((1, 1, 2, HEAD_DIM, tq), lambda bi, h, i: (bi, h, 0, 0, i)),
            pl.BlockSpec((1, s, LANES), lambda bi, h, i: (bi, 0, COL_DK // LANES + h)),
            pl.BlockSpec((1, 1, s // tk, 2 * HEAD_DIM, tk), lambda bi, h, i: (bi, h, 0, 0, 0)),
            pl.BlockSpec((2 * HEAD_DIM, 1), lambda bi, h, i: (0, 0)),
        ],
        out_specs=pl.BlockSpec((1, tq, LANES), lambda bi, h, i: (bi, i, h)),
        scratch_shapes=[pltpu.VMEM((1, tq), F32)] * 4 + [pltpu.VMEM((dv, tq), F32)] * 2,
        compiler_params=_cparams(("parallel", "parallel", "arbitrary")),
        name="diff_attention",
    )(shift.reshape(1), lam_vecs, qT, zb, vT, subln.reshape(-1, 1))


def _gelu_tanh(v):
    return 0.5 * v * (1.0 + jnp.tanh(math.sqrt(2.0 / math.pi) * (v + 0.044715 * (v * v * v))))


def _cmp_kernel(c_ref, pos_ref, w1_ref, w2_ref, kg_ref, o_ref, *, n_cmp):
    jh = pl.program_id(1)
    half = CMP_STRIDE * HEAD_DIM
    c = c_ref[0, 0]
    w1a = w1_ref[0, :half, :].astype(BF16)
    w1b = w1_ref[0, half:, :].astype(BF16)
    pos = jnp.broadcast_to(pos_ref[0], (8, 2 * half)).astype(BF16)
    bias = (jnp.dot(pos[:, :half], w1a, preferred_element_type=F32)
            + jnp.dot(pos[:, half:], w1b, preferred_element_type=F32))[0:1]
    u = jnp.dot(c, w1a, preferred_element_type=F32)
    v = jnp.dot(c, w1b, preferred_element_type=F32)
    ncp = u.shape[0]
    hid = _gelu_tanh(u + pltpu.roll(v, ncp - 1, 0) + bias)
    y = jnp.dot(hid.astype(BF16), w2_ref[0].astype(BF16), preferred_element_type=F32)
    yn = y * lax.rsqrt(jnp.mean(y * y, axis=-1, keepdims=True) + NORM_EPS) * kg_ref[...]
    y = jnp.where(jh < NSA_KV_HEADS, yn, y)
    row = lax.broadcasted_iota(I32, y.shape, 0)
    o_ref[0, 0] = jnp.where(row < n_cmp, y, 0.0)


def _compress(chunks, pos, w1, w2, kgain):
    b, _, ncp, cd = chunks.shape
    kern = functools.partial(_cmp_kernel, n_cmp=ncp - 1)
    return pl.pallas_call(
        kern,
        out_shape=jax.ShapeDtypeStruct((b, 4, ncp, HEAD_DIM), F32),
        grid=(b, 4),
        in_specs=[
            pl.BlockSpec((1, 1, ncp, cd), lambda bi, j: (bi, j, 0, 0)),
            pl.BlockSpec((1, 1, 2 * cd), lambda bi, j: (j // NSA_KV_HEADS, 0, 0)),
            pl.BlockSpec((1, 2 * cd, CMP_HIDDEN), lambda bi, j: (j // NSA_KV_HEADS, 0, 0)),
            pl.BlockSpec((1, CMP_HIDDEN, HEAD_DIM), lambda bi, j: (j // NSA_KV_HEADS, 0, 0)),
            pl.BlockSpec((1, HEAD_DIM), lambda bi, j: (0, 0)),
        ],
        out_specs=pl.BlockSpec((1, 1, ncp, HEAD_DIM), lambda bi, j: (bi, j, 0, 0)),
        compiler_params=_cparams(("parallel", "parallel")),
        name="nsa_compress",
    )(chunks, pos, w1, w2, kgain)


def _nsa_kernel(mb_ref, qT_ref, ksel_ref, vselT_ref, kwin_ref, vwinT_ref, kc_ref, vcT_ref, ovl_ref, gate_ref, o_ref,
                selb_sc, m_s, m_w, l_s, l_w, a_s, a_w, *, tq, n_top):
    hk = pl.program_id(1)
    i = pl.program_id(2)
    g4 = NSA_GROUP
    nq = g4 * tq
    q4 = jnp.concatenate([qT_ref[0, 0, g] for g in range(g4)], axis=1)
    half = lax.broadcasted_iota(I32, (2 * HEAD_DIM, nq), 0) // HEAD_DIM
    qp = jnp.where(half == hk, jnp.concatenate([q4, q4], axis=0), jnp.zeros((), BF16))
    t1 = i * tq + lax.broadcasted_iota(I32, (1, tq), 1)

    def tile4(a):
        return jnp.concatenate([a] * g4, axis=1)

    ncp = kc_ref.shape[1]
    sc = jnp.dot(kc_ref[0], qp, preferred_element_type=F32)
    cend = CMP_STRIDE * lax.broadcasted_iota(I32, (ncp, tq), 0) + (CMP_BLOCK - 1)
    sc = sc + tile4(jnp.where(cend <= t1, 0.0, NEG))
    e = jnp.where(sc > 0.5 * NEG, jnp.exp2(sc - jnp.max(sc, axis=0, keepdims=True)), 0.0)
    p = e / jnp.maximum(jnp.sum(e, axis=0, keepdims=True), 1e-30)
    o_c = jnp.dot(vcT_ref[0, 0], p.astype(BF16), preferred_element_type=F32)

    psum = p[:, 0:tq]
    for g in range(1, g4):
        psum = psum + p[:, g * tq:(g + 1) * tq]
    p_hi = psum.astype(BF16)
    p_lo = (psum - p_hi.astype(F32)).astype(BF16)
    imp = (jnp.dot(ovl_ref[...], p_hi, preferred_element_type=F32)
           + jnp.dot(ovl_ref[...], p_lo, preferred_element_type=F32))
    n_sel = imp.shape[0]
    jrow = lax.broadcasted_iota(I32, (n_sel, tq), 0)
    cur = t1 // SEL_BLOCK
    forced = (jrow == 0) | (jrow == cur) | (jrow == cur - 1)
    score = jnp.where(forced, SEL_FORCED_SCORE, jnp.where(jrow <= cur, imp, NEG))
    selb = jnp.full((n_sel, tq), NEG, F32)
    jrow_f = jrow.astype(F32)
    for _ in range(n_top):
        best = jnp.max(score, axis=0, keepdims=True)
        pick = jnp.min(jnp.where(score == best, jrow_f, float(n_sel)), axis=0, keepdims=True)
        hit = jrow_f == pick
        selb = jnp.where(hit, 0.0, selb)
        score = jnp.where(hit, TAKEN, score)
    selb_sc[...] = selb

    for ref in (m_s, m_w):
        ref[...] = jnp.full(ref.shape, NEG, F32)
    for ref in (l_s, l_w, a_s, a_w):
        ref[...] = jnp.zeros(ref.shape, F32)
    tk = tq
    bpt = tk // SEL_BLOCK
    krow = lax.broadcasted_iota(I32, (tk, tq), 0)
    shift_s, shift_w = mb_ref[0], mb_ref[1]
    fixed = jnp.maximum(shift_s, shift_w) <= FIXED_SHIFT_LIMIT

    def sel_tiles(js, masked, fixed):
        ss = []
        for j in js:
            kt = ksel_ref[0, pl.ds(pl.multiple_of(j * tk, tk), tk), :]
            rows = [jnp.broadcast_to(selb_sc[pl.ds(j * bpt + r, 1), :], (SEL_BLOCK, tq)) for r in range(bpt)]
            bias = jnp.concatenate(rows, axis=0)
            if masked:
                bias = jnp.where(j * tk + krow <= t1, bias, NEG)
            ss.append(jnp.dot(kt, qp, preferred_element_type=F32) + tile4(bias))
        vts = [vselT_ref[0, 0, j] for j in js]
        if fixed:
            _fixed_update(ss, vts, shift_s, l_s, a_s)
        else:
            _online_update(ss, vts, m_s, l_s, a_s)

    def win_tiles(j, fixed):
        kt = kwin_ref[0, pl.ds(pl.multiple_of(j * tk, tk), tk), :]
        kpos = j * tk + krow
        bias = jnp.where(kpos <= t1, jnp.where(kpos > t1 - WINDOW, 0.0, NEG), NEG)
        ss = [jnp.dot(kt, qp, preferred_element_type=F32) + tile4(bias)]
        vts = [vwinT_ref[0, 0, j]]
        if fixed:
            _fixed_update(ss, vts, shift_w, l_w, a_w)
        else:
            _online_update(ss, vts, m_w, l_w, a_w)

    j_lo = jnp.maximum(i * tq - (WINDOW - 1), 0) // tk
    for use_fixed in (True, False):
        @pl.when(fixed == use_fixed)
        def _():
            _tile_loops(i, i + 1, functools.partial(sel_tiles, fixed=use_fixed),
                        FIXED_TILES_PER_STEP if use_fixed else 1)
            lax.fori_loop(j_lo, i + 1, lambda j, c: (win_tiles(j, use_fixed), c)[1], 0)

    def gate_row(br):
        gt = gate_ref[0, 0, br]
        return jnp.concatenate([gt[g:g + 1] for g in range(g4)], axis=1)

    out = (o_c * gate_row(0) + (a_s[...] / l_s[...]) * gate_row(1) + (a_w[...] / l_w[...]) * gate_row(2))
    stacked = jnp.concatenate([out[:, g * tq:(g + 1) * tq] for g in range(g4)], axis=0)
    o_ref[0] = stacked.T.astype(BF16)


def _nsa_attention(zb, qT, vselT, vwinT, kc, vcT, ovl, gatesT, shifts, tq):
    b, s, _ = zb.shape
    nt = s // tq
    n_sel = s // SEL_BLOCK
    ncp = kc.shape[1]
    kern = functools.partial(_nsa_kernel, tq=tq, n_top=min(SEL_TOP, n_sel))
    gd = NSA_GROUP * HEAD_DIM
    nq = NSA_GROUP * tq
    return pl.pallas_call(
        kern,
        out_shape=jax.ShapeDtypeStruct((b, s, NSA_Q_HEADS * HEAD_DIM), BF16),
        grid=(b, NSA_KV_HEADS, nt),
        in_specs=[
            pl.BlockSpec(memory_space=pltpu.SMEM),
            pl.BlockSpec((1, 1, NSA_GROUP, HEAD_DIM, tq), lambda bi, h, i: (bi, h, 0, 0, i)),
            pl.BlockSpec((1, s, LANES), lambda bi, h, i: (bi, 0, COL_KSEL // LANES)),
            pl.BlockSpec((1, 1, nt, HEAD_DIM, tq), lambda bi, h, i: (bi, h, 0, 0, 0)),
            pl.BlockSpec((1, s, LANES), lambda bi, h, i: (bi, 0, COL_KWIN // LANES)),
            pl.BlockSpec((1, 1, nt, HEAD_DIM, tq), lambda bi, h, i: (bi, h, 0, 0, 0)),
            pl.BlockSpec((1, ncp, LANES), lambda bi, h, i: (bi, 0, 0)),
            pl.BlockSpec((1, 1, HEAD_DIM, ncp), lambda bi, h, i: (bi, h, 0, 0)),
            pl.BlockSpec((n_sel, ncp), lambda bi, h, i: (0, 0)),
            pl.BlockSpec((1, 1, 3, NSA_GROUP, tq), lambda bi, h, i: (bi, h, 0, 0, i)),
        ],
        out_specs=pl.BlockSpec((1, tq, gd), lambda bi, h, i: (bi, i, h)),
        scratch_shapes=([pltpu.VMEM((n_sel, tq), F32)] + [pltpu.VMEM((1, nq), F32)] * 4
                        + [pltpu.VMEM((HEAD_DIM, nq), F32)] * 2),
        compiler_params=_cparams(("parallel", "parallel", "arbitrary")),
        name="nsa_attention",
    )(shifts, qT, zb, vselT, zb, vwinT, kc, vcT, ovl, gatesT)


def _in_conv_kernel(x_ref, mod_ref, g_ref, w_ref, b_ref, u_ref):
    m = mod_ref[0]
    d = x_ref.shape[-1]
    h = _norm_mod(x_ref[0], g_ref[...], m[1:2], m[0:1]).astype(BF16)
    a = jnp.dot(h, w_ref[:, :d], preferred_element_type=F32) + b_ref[:, :d]
    g = jnp.dot(h, w_ref[:, d:], preferred_element_type=F32) + b_ref[:, d:]
    u_ref[0] = a * _sigmoid(g)


def _conv_in_proj(x, mod, gain, w, bias, tm):
    b, s, d = x.shape
    return pl.pallas_call(
        _in_conv_kernel,
        out_shape=jax.ShapeDtypeStruct((b, s, d), F32),
        grid=(b, s // tm),
        in_specs=[
            pl.BlockSpec((1, tm, d), lambda i, j: (i, j, 0)),
            pl.BlockSpec((1, 6, d), lambda i, j: (i, 0, 0)),
            pl.BlockSpec((1, d), lambda i, j: (0, 0)),
            pl.BlockSpec((d, 2 * d), lambda i, j: (0, 0)),
            pl.BlockSpec((1, 2 * d), lambda i, j: (0, 0)),
        ],
        out_specs=pl.BlockSpec((1, tm, d), lambda i, j: (i, j, 0)),
        compiler_params=_cparams(("parallel", "parallel")),
        name="conv_in_proj",
    )(x, mod, gain, w, bias)


HALO = 32


def _dwconv_kernel(u_ref, halo_ref, w_ref, b_ref, lg_ref, lb_ref, o_ref, buf_sc, acc_sc, *, tm):
    i = pl.program_id(1)
    d = u_ref.shape[-1]
    buf_sc[0:HALO, :] = jnp.where(i > 0, halo_ref[0], 0.0)
    buf_sc[HALO:, :] = u_ref[0]
    off = HALO - (CONV_WIDTH - 1)
    cw = 256
    for c in range(d // cw):
        cs = slice(c * cw, (c + 1) * cw)
        acc = jnp.zeros((tm, cw), F32) + b_ref[:, cs]
        for j in range(CONV_WIDTH):
            acc = acc + w_ref[j:j + 1, cs] * buf_sc[off + j:off + j + tm, cs]
        acc_sc[:, cs] = acc
    y = acc_sc[...]
    mu = jnp.mean(y, axis=-1, keepdims=True)
    yc = y - mu
    var = jnp.mean(yc * yc, axis=-1, keepdims=True)
    yn = yc * lax.rsqrt(var + NORM_EPS) * lg_ref[...] + lb_ref[...]
    o_ref[0] = (yn * _sigmoid(yn)).astype(BF16)


def _dwconv_ln_swish(u, dw_w, dw_b, ln_g, ln_b, tm):
    b, s, d = u.shape
    kern = functools.partial(_dwconv_kernel, tm=tm)
    hb = tm // HALO
    return pl.pallas_call(
        kern,
        out_shape=jax.ShapeDtypeStruct((b, s, d), BF16),
        grid=(b, s // tm),
        in_specs=[
            pl.BlockSpec((1, tm, d), lambda bi, i: (bi, i, 0)),
            pl.BlockSpec((1, HALO, d), lambda bi, i: (bi, jnp.maximum(i * hb - 1, 0), 0)),
            pl.BlockSpec((CONV_WIDTH, d), lambda bi, i: (0, 0)),
            pl.BlockSpec((1, d), lambda bi, i: (0, 0)),
            pl.BlockSpec((1, d), lambda bi, i: (0, 0)),
            pl.BlockSpec((1, d), lambda bi, i: (0, 0)),
        ],
        out_specs=pl.BlockSpec((1, tm, d), lambda bi, i: (bi, i, 0)),
        scratch_shapes=[pltpu.VMEM((tm + HALO, d), F32), pltpu.VMEM((tm, d), F32)],
        compiler_params=_cparams(("parallel", "parallel")),
        name="dwconv_ln_swish",
    )(u, u, dw_w, dw_b, ln_g, ln_b)


def _out_router_kernel(a1_ref, a2_ref, w1_ref, w2_ref, b_ref, x_ref, mod_ref, g_ref, rwh_ref, rwl_ref, rb_ref,
                       xo_ref, h_ref, idx_ref, wt_ref):
    m = mod_ref[0]
    y = (jnp.dot(a1_ref[0], w1_ref[...], preferred_element_type=F32)
         + jnp.dot(a2_ref[0], w2_ref[...], preferred_element_type=F32) + b_ref[...])
    x = x_ref[0] + m[2:3] * y
    xo_ref[0] = x
    h = _norm_mod(x, g_ref[...], m[4:5], m[3:4])
    for c in range(h_ref.shape[2]):
        h_ref[0, :, c, :] = h[:, c * LANES:(c + 1) * LANES]
    h_hi = h.astype(BF16)
    h_lo = (h - h_hi.astype(F32)).astype(BF16)
    logits = (jnp.dot(h_hi, rwh_ref[...], preferred_element_type=F32)
              + jnp.dot(h_lo, rwh_ref[...], preferred_element_type=F32)
              + jnp.dot(h_hi, rwl_ref[...], preferred_element_type=F32)) + rb_ref[...]
    lane = lax.broadcasted_iota(I32, logits.shape, 1)
    lane_f = lane.astype(F32)
    idx_out = jnp.zeros(logits.shape, F32)
    val_out = jnp.full(logits.shape, NEG, F32)
    for k in range(TOP_K):
        best = jnp.max(logits, axis=-1, keepdims=True)
        pick = jnp.min(jnp.where(logits == best, lane_f, float(LANES)), axis=-1, keepdims=True)
        idx_out = jnp.where(lane == k, pick, idx_out)
        val_out = jnp.where(lane == k, best, val_out)
        logits = jnp.where(lane_f == pick, TAKEN, logits)
    e = jnp.where(lane < TOP_K, jnp.exp(val_out - jnp.max(val_out, axis=-1, keepdims=True)), 0.0)
    idx_ref[0] = idx_out.astype(I32)
    wt_ref[0] = e / jnp.sum(e, axis=-1, keepdims=True)


def _out_proj_router(a1, a2, c1, c2, w1, w2, bias, x, mod, gain, rw_hi, rw_lo, rb, tm):
    b, s, d = x.shape
    kw = w1.shape[0]
    tok = lambda i, j: (i, j, 0)
    const = lambda i, j: (0, 0)
    return pl.pallas_call(
        _out_router_kernel,
        out_shape=(jax.ShapeDtypeStruct((b, s, d), F32), jax.ShapeDtypeStruct((b, s, d // LANES, LANES), F32),
                   jax.ShapeDtypeStruct((b, s, LANES), I32), jax.ShapeDtypeStruct((b, s, LANES), F32)),
        grid=(b, s // tm),
        in_specs=[
            pl.BlockSpec((1, tm, kw), lambda i, j: (i, j, c1)),
            pl.BlockSpec((1, tm, kw), lambda i, j: (i, j, c2)),
            pl.BlockSpec((kw, d), const),
            pl.BlockSpec((kw, d), const),
            pl.BlockSpec((1, d), const),
            pl.BlockSpec((1, tm, d), tok),
            pl.BlockSpec((1, 6, d), lambda i, j: (i, 0, 0)),
            pl.BlockSpec((1, d), const),
            pl.BlockSpec((d, LANES), const),
            pl.BlockSpec((d, LANES), const),
            pl.BlockSpec((1, LANES), const),
        ],
        out_specs=(pl.BlockSpec((1, tm, d), tok), pl.BlockSpec((1, tm, d // LANES, LANES), lambda i, j: (i, j, 0, 0)),
                   pl.BlockSpec((1, tm, LANES), tok), pl.BlockSpec((1, tm, LANES), tok)),
        compiler_params=_cparams(("parallel", "parallel")),
        name="out_proj_router",
    )(a1, a2, w1, w2, bias, x, mod, gain, rw_hi, rw_lo, rb)


ROW_CHUNKS = D_MODEL // LANES
DEINT = 512
DUMP_ROWS = 512


def _expert_kernel(be_ref, nv_ref, gsrc0_ref, gsrc1_ref, gdstp_ref, gdst0_ref, h_ref, w1_ref, b1g_ref, b1l_ref,
                   w2_ref, b2_ref, pe_ref, po_ref, yt_ref, xbuf, ybuf, w1g_sc, w1l_sc, w2_sc, gsem, ssem,
                   *, n_blocks, dump_row):
    i = pl.program_id(0)
    cur = i % 2
    nxt = 1 - cur
    rows = MOE_BLOCK

    def gather(idx_ref, slot, r):
        return pltpu.make_async_copy(h_ref.at[idx_ref[0, 0, r]], xbuf.at[slot, r], gsem.at[slot])

    def scatter(dst_row, slot, r):
        return pltpu.make_async_copy(ybuf.at[slot, r], yt_ref.at[dst_row], ssem.at[slot])

    def wait_gather(slot):
        pltpu.make_async_copy(h_ref.at[pl.ds(0, rows)], xbuf.at[slot], gsem.at[slot]).wait()

    def wait_scatter(slot):
        pltpu.make_async_copy(ybuf.at[slot], yt_ref.at[pl.ds(0, rows)], ssem.at[slot]).wait()

    @pl.when(i == 0)
    def _():
        ybuf[...] = jnp.zeros(ybuf.shape, F32)

        def prime(r, c):
            gather(gsrc0_ref, 0, r).start()
            return c

        lax.fori_loop(0, rows, prime, 0)

    wait_gather(cur)

    @pl.when(i >= 1)
    def _():
        wait_scatter(cur)

    first = i == 0
    changed = jnp.logical_or(first, be_ref[i] != be_ref[jnp.maximum(i - 1, 0)])

    @pl.when(changed)
    def _():
        for c in range(w1_ref.shape[2] // DEINT):
            wc = w1_ref[0, :, c * DEINT:(c + 1) * DEINT].astype(BF16)
            cs = slice(c * (DEINT // 2), (c + 1) * (DEINT // 2))
            w1g_sc[:, cs] = jnp.dot(wc, pe_ref[...], preferred_element_type=F32).astype(BF16)
            w1l_sc[:, cs] = jnp.dot(wc, po_ref[...], preferred_element_type=F32).astype(BF16)
        w2_sc[...] = w2_ref[0].astype(BF16)

    for r in range(rows):
        gather(gsrc1_ref, nxt, r).start(priority=r % 2)
        scatter(jnp.where(first, dump_row + r, gdstp_ref[0, 0, r]), nxt, r).start(priority=(r + 1) % 2)

    x = jnp.concatenate([xbuf[cur, :, c, :] for c in range(ROW_CHUNKS)], axis=1)
    row = lax.broadcasted_iota(I32, x.shape, 0)
    xb = jnp.where(row < nv_ref[i], x, 0.0).astype(BF16)
    glu = jnp.dot(xb, w1g_sc[...], preferred_element_type=F32) + b1g_ref[0]
    lin = jnp.dot(xb, w1l_sc[...], preferred_element_type=F32) + b1l_ref[0]
    glu = jnp.minimum(glu, SWIGLU_LIMIT)
    lin = jnp.clip(lin, -SWIGLU_LIMIT, SWIGLU_LIMIT)
    act = glu * _sigmoid(SWIGLU_ALPHA * glu) * (lin + 1.0)
    y = jnp.dot(act.astype(BF16), w2_sc[...], preferred_element_type=F32) + b2_ref[0]
    for c in range(ROW_CHUNKS):
        ybuf[cur, :, c, :] = y[:, c * LANES:(c + 1) * LANES]

    @pl.when(i == n_blocks - 1)
    def _():
        def last(r, c):
            scatter(gdst0_ref[0, 0, r], cur, r).start()
            return c

        lax.fori_loop(0, rows, last, 0)
        wait_gather(nxt)
        wait_scatter(nxt)
        wait_scatter(cur)


def _experts(h, gsrc, gdst, block_expert, n_valid, w1, b1g, b1l, w2, b2):
    t = h.shape[0]
    n_blocks = gsrc.shape[0]
    d, f2 = w1.shape[1:]
    f = f2 // 2
    sel = np.arange(DEINT)[:, None] - 2 * np.arange(DEINT // 2)[None, :]
    p_even = jnp.asarray((sel == 0).astype(np.float32), dtype=BF16)
    p_odd = jnp.asarray((sel == 1).astype(np.float32), dtype=BF16)
    last = n_blocks - 1

    def wsel(i, be, nv):
        return (be[i], 0, 0)

    const = lambda i, be, nv: (0, 0)
    idx_blk = (1, 1, MOE_BLOCK)
    kern = functools.partial(_expert_kernel, n_blocks=n_blocks, dump_row=TOP_K * t)
    return pl.pallas_call(
        kern,
        out_shape=jax.ShapeDtypeStruct((TOP_K * t + DUMP_ROWS, ROW_CHUNKS, LANES), F32),
        grid_spec=pltpu.PrefetchScalarGridSpec(
            num_scalar_prefetch=2,
            grid=(n_blocks,),
            in_specs=[
                pl.BlockSpec(idx_blk, lambda i, be, nv: (i, 0, 0), memory_space=pltpu.SMEM),
                pl.BlockSpec(idx_blk, lambda i, be, nv: (jnp.minimum(i + 1, last), 0, 0), memory_space=pltpu.SMEM),
                pl.BlockSpec(idx_blk, lambda i, be, nv: (jnp.maximum(i - 1, 0), 0, 0), memory_space=pltpu.SMEM),
                pl.BlockSpec(idx_blk, lambda i, be, nv: (i, 0, 0), memory_space=pltpu.SMEM),
                pl.BlockSpec(memory_space=pl.ANY),
                pl.BlockSpec((1, d, f2), wsel),
                pl.BlockSpec((1, 1, f), wsel),
                pl.BlockSpec((1, 1, f), wsel),
                pl.BlockSpec((1, f, d), wsel),
                pl.BlockSpec((1, 1, d), wsel),
                pl.BlockSpec((DEINT, DEINT // 2), const),
                pl.BlockSpec((DEINT, DEINT // 2), const),
            ],
            out_specs=pl.BlockSpec(memory_space=pl.ANY),
            scratch_shapes=[pltpu.VMEM((2, MOE_BLOCK, ROW_CHUNKS, LANES), F32),
                            pltpu.VMEM((2, MOE_BLOCK, ROW_CHUNKS, LANES), F32),
                            pltpu.VMEM((d, f), BF16), pltpu.VMEM((d, f), BF16), pltpu.VMEM((f, d), BF16),
                            pltpu.SemaphoreType.DMA((2,)), pltpu.SemaphoreType.DMA((2,))],
        ),
        compiler_params=_cparams(("arbitrary",)),
        name="moe_experts",
    )(block_expert, n_valid, gsrc, gsrc, gdst, gdst, h, w1, b1g, b1l, w2, b2, p_even, p_odd)


def _combine_kernel(y0_ref, y1_ref, y2_ref, y3_ref, wt_ref, x_ref, mod_ref, o_ref):
    wt = wt_ref[0]
    g2 = mod_ref[0][5:6]
    y_refs = (y0_ref, y1_ref, y2_ref, y3_ref)
    wk = [wt[:, k:k + 1] for k in range(TOP_K)]
    for c in range(ROW_CHUNKS):
        y = wk[0] * y_refs[0][:, c, :]
        for k in range(1, TOP_K):
            y = y + wk[k] * y_refs[k][:, c, :]
        cs = slice(c * LANES, (c + 1) * LANES)
        o_ref[0, :, cs] = x_ref[0, :, cs] + g2[:, cs] * y


def _combine(yt, wts, x, mod, tm):
    b, s, d = x.shape
    nt = s // tm
    tiles = b * nt
    tok = lambda i, j: (i, j, 0)

    def yspec(k):
        return pl.BlockSpec((tm, ROW_CHUNKS, LANES), lambda i, j: (k * tiles + i * nt + j, 0, 0))

    return pl.pallas_call(
        _combine_kernel,
        out_shape=jax.ShapeDtypeStruct((b, s, d), F32),
        grid=(b, nt),
        in_specs=[yspec(k) for k in range(TOP_K)] + [
            pl.BlockSpec((1, tm, LANES), tok),
            pl.BlockSpec((1, tm, d), tok),
            pl.BlockSpec((1, 6, d), lambda i, j: (i, 0, 0)),
        ],
        out_specs=pl.BlockSpec((1, tm, d), tok),
        compiler_params=_cparams(("arbitrary", "arbitrary")),
        name="moe_combine",
    )(yt, yt, yt, yt, wts, x, mod)


def _route_slots(top_idx, n_blocks):
    t = top_idx.shape[0]
    n_assign = t * TOP_K
    n_slots = n_blocks * MOE_BLOCK
    flat_e = top_idx.reshape(-1)
    onehot = (flat_e[:, None] == jnp.arange(N_EXPERTS, dtype=I32)[None, :]).astype(I32)
    csum = jnp.cumsum(onehot, axis=0)
    rank = jnp.take_along_axis(csum, flat_e[:, None], axis=1)[:, 0] - 1
    counts = csum[-1]
    padded = (counts + MOE_BLOCK - 1) // MOE_BLOCK * MOE_BLOCK
    pad_end = jnp.cumsum(padded)
    pad_start = pad_end - padded
    pos = (pad_start[flat_e] + rank).astype(I32)
    slot_a = jnp.full((n_slots,), -1, I32).at[pos].set(jnp.arange(n_assign, dtype=I32), unique_indices=True)
    slot = jnp.arange(n_slots, dtype=I32)
    valid = slot_a >= 0
    gsrc = jnp.where(valid, slot_a // TOP_K, 0)
    gdst = jnp.where(valid, (slot_a % TOP_K) * t + slot_a // TOP_K, n_assign + slot % DUMP_ROWS)
    blk_start = jnp.arange(n_blocks, dtype=I32) * MOE_BLOCK
    block_expert = jnp.minimum(jnp.searchsorted(pad_end, blk_start, side='right'), N_EXPERTS - 1).astype(I32)
    n_valid = jnp.clip(pad_start[block_expert] + counts[block_expert] - blk_start, 0, MOE_BLOCK).astype(I32)
    shape = (n_blocks, 1, MOE_BLOCK)
    return gsrc.reshape(shape), gdst.reshape(shape), block_expert, n_valid


def _moe(x_mid, h, idx, wts, mod, w1, b1g, b1l, w2, b2, tm):
    b, s, d = x_mid.shape
    t = b * s
    n_blocks = -(-t * TOP_K // MOE_BLOCK) + N_EXPERTS
    top_idx = idx.reshape(t, LANES)[:, :TOP_K]
    gsrc, gdst, block_expert, n_valid = _route_slots(top_idx, n_blocks)
    yt = _experts(h.reshape(t, ROW_CHUNKS, LANES), gsrc, gdst, block_expert, n_valid, w1, b1g, b1l, w2, b2)
    return _combine(yt, wts, x_mid, mod, tm)


def _pad_cols(w, n):
    return jnp.pad(w, ((0, 0), (0, n - w.shape[-1])))


def _attn_colgain(diff_qk_gain, nsa_q_gain, nsa_k_gain):
    scale = HEAD_DIM ** -0.5 * LOG2E
    ones = jnp.ones((LANES,), F32)
    parts = [jnp.tile(diff_qk_gain[0] * scale, 8), jnp.tile(diff_qk_gain[1], 8), jnp.ones((512,), F32),
             jnp.tile(nsa_q_gain * scale, 8), ones, ones, jnp.tile(nsa_k_gain[1], 2), ones,
             jnp.tile(nsa_k_gain[2], 2), ones, ones]
    return jnp.concatenate(parts).reshape(1, W_IN_PAD)


def _overlap_matrix(s):
    n_sel = s // SEL_BLOCK
    ncp = s // CMP_STRIDE
    c_start = np.arange(ncp) * CMP_STRIDE
    s_start = np.arange(n_sel) * SEL_BLOCK
    ovl = (c_start[None, :] < s_start[:, None] + SEL_BLOCK) & (c_start[None, :] + CMP_BLOCK > s_start[:, None])
    ovl[:, ncp - 1] = False
    return jnp.asarray(ovl.astype(np.float32), dtype=BF16)


def _attention_layer(x, mod, gain, w_in, w_out, diff_qk_gain, diff_lambda, diff_subln, nsa_q_gain, nsa_k_gain,
                     cmp_pos, cmp_w1, cmp_w2, lam_init):
    b, s, d = x.shape
    tq_d, tk_d, tq_n = 256, 512, 256
    zb, gsig = _attn_in_proj(x, mod, gain, _pad_cols(w_in, W_IN_PAD).astype(BF16),
                             _attn_colgain(diff_qk_gain, nsa_q_gain, nsa_k_gain), 512)

    def cols(c0, n):
        return zb[:, :, c0:c0 + n]

    dqT = cols(COL_DQ, 512).reshape(b, s, DIFF_HEADS, 2, HEAD_DIM).transpose(0, 2, 3, 4, 1)
    dvT = cols(COL_DV, 512).reshape(b, s // tk_d, tk_d, DIFF_HEADS, 2 * HEAD_DIM).transpose(0, 3, 1, 4, 2)
    o_diff = _diff_attention(zb, dqT, dvT, _score_bound(diff_qk_gain[0], diff_qk_gain[1]), diff_lambda, diff_subln,
                             lam_init, tq_d, tk_d)

    nqT = cols(COL_NQ, 512).reshape(b, s, NSA_KV_HEADS, NSA_GROUP, HEAD_DIM).transpose(0, 2, 3, 4, 1)

    def val_t(c0):
        return cols(c0, LANES).reshape(b, s // tq_n, tq_n, NSA_KV_HEADS, HEAD_DIM).transpose(0, 3, 1, 4, 2)

    ncp = s // CMP_STRIDE
    chunks = cols(COL_KCMP, 2 * LANES).reshape(b, ncp, CMP_STRIDE, 2, NSA_KV_HEADS, HEAD_DIM)
    chunks = chunks.transpose(0, 3, 4, 1, 2, 5).reshape(b, 4, ncp, CMP_STRIDE * HEAD_DIM)
    cmp_out = _compress(chunks, cmp_pos.reshape(2, 1, CMP_BLOCK * HEAD_DIM), cmp_w1, cmp_w2, nsa_k_gain[0:1])
    kc = cmp_out[:, 0:2].transpose(0, 2, 1, 3).reshape(b, ncp, LANES).astype(BF16)
    vcT = cmp_out[:, 2:4].transpose(0, 1, 3, 2).astype(BF16)
    gatesT = gsig[:, :, :NSA_Q_HEADS * 3].reshape(b, s, NSA_KV_HEADS, NSA_GROUP, 3).transpose(0, 2, 4, 3, 1)
    shifts = jnp.stack([_score_bound(nsa_q_gain, nsa_k_gain[1]), _score_bound(nsa_q_gain, nsa_k_gain[2])])
    o_nsa = _nsa_attention(zb, nqT, val_t(COL_VSEL), val_t(COL_VWIN), kc, vcT, _overlap_matrix(s), gatesT, shifts,
                           tq_n)
    return o_diff, o_nsa


def kernel(x, c, mod_w, mod_b, norm_mix, norm_ffn, attn_w_in, attn_w_out, diff_qk_gain, diff_lambda, diff_subln,
           nsa_q_gain, nsa_k_gain, nsa_cmp_pos, nsa_cmp_w1, nsa_cmp_w2, conv_pw1_w, conv_pw1_b, conv_dw_w,
           conv_dw_b, conv_ln_g, conv_ln_b, conv_pw2_w, conv_pw2_b, router_w, router_b, moe_w1, moe_b1, moe_w2,
           moe_b2):
    b, s, d = x.shape
    depth = mod_w.shape[0]
    tm = 512 if s % 512 == 0 else 256
    mods = _modulation(c, mod_w, mod_b).reshape(depth, b, 6, d)
    half = d // 2
    for i in range(depth):
        mod = mods[i]
        j = i // 2
        if i % 2 == 0:
            lam_init = 0.8 - 0.6 * math.exp(-0.3 * i)
            o_diff, o_nsa = _attention_layer(
                x, mod, norm_mix[i:i + 1], attn_w_in[j], attn_w_out[j], diff_qk_gain[j], diff_lambda[j],
                diff_subln[j], nsa_q_gain[j], nsa_k_gain[j], nsa_cmp_pos[j], nsa_cmp_w1[j], nsa_cmp_w2[j], lam_init)
            a1, a2, c1, c2 = o_diff, o_nsa, 0, 0
            w_o = attn_w_out[j].astype(BF16)
            bias = jnp.zeros((1, d), F32)
        else:
            u = _conv_in_proj(x, mod, norm_mix[i:i + 1], conv_pw1_w[j].astype(BF16), conv_pw1_b[j:j + 1], tm)
            v = _dwconv_ln_swish(u, conv_dw_w[j].reshape(CONV_WIDTH, d), conv_dw_b[j:j + 1], conv_ln_g[j:j + 1],
                                 conv_ln_b[j:j + 1], 256)
            a1, a2, c1, c2 = v, v, 0, 1
            w_o = conv_pw2_w[j].astype(BF16)
            bias = conv_pw2_b[j:j + 1]
        rw = _pad_cols(router_w[i], LANES)
        rw_hi = rw.astype(BF16)
        rw_lo = (rw - rw_hi.astype(F32)).astype(BF16)
        rb = jnp.concatenate([router_b[i], jnp.full((LANES - N_EXPERTS,), NEG, F32)]).reshape(1, LANES)
        x_mid, h, idx, wts = _out_proj_router(a1, a2, c1, c2, w_o[:half], w_o[half:], bias, x, mod,
                                              norm_ffn[i:i + 1], rw_hi, rw_lo, rb, tm)
        x = _moe(x_mid, h, idx, wts, mod, moe_w1[i], moe_b1[i][:, None, 0::2], moe_b1[i][:, None, 1::2], moe_w2[i],
                 moe_b2[i][:, None, :], 256)
    return x
```

```python
import functools
import math

import jax
import jax.numpy as jnp
import numpy as np
from jax import lax
from jax.experimental import pallas as pl
from jax.experimental.pallas import tpu as pltpu

F32 = jnp.float32
BF16 = jnp.bfloat16
I32 = jnp.int32

D_MODEL = 1024
HEAD_DIM = 64
DIFF_HEADS = 4
NSA_Q_HEADS = 8
NSA_KV_HEADS = 2
NSA_GROUP = NSA_Q_HEADS // NSA_KV_HEADS
CMP_BLOCK = 32
CMP_STRIDE = 16
CMP_HIDDEN = 256
SEL_BLOCK = 64
SEL_TOP = 16
SEL_FORCED_SCORE = 1.0e4
WINDOW = 512
CONV_WIDTH = 31
N_EXPERTS = 32
TOP_K = 4
SWIGLU_ALPHA = 1.702
SWIGLU_LIMIT = 7.0
MOE_BLOCK = 256
NORM_EPS = 1e-6

LANES = 128
NEG = -1e30
TAKEN = -3e38
VMEM_LIMIT = 56 * 1024 * 1024

COL_DQ, COL_DK, COL_DV, COL_NQ = 0, 512, 1024, 1536
COL_KCMP, COL_VCMP, COL_KSEL, COL_VSEL, COL_KWIN, COL_VWIN = 2048, 2176, 2304, 2432, 2560, 2688
COL_GATE = 2816
W_IN_PAD = 2944
NORM_SLABS = frozenset(list(range(0, 8)) + list(range(12, 16)) + [COL_KSEL // LANES, COL_KWIN // LANES])


def _cparams(sem):
    return pltpu.CompilerParams(dimension_semantics=sem, vmem_limit_bytes=VMEM_LIMIT)


def _sigmoid(v):
    return 1.0 / (1.0 + jnp.exp(-v))


def _norm_mod(x, g, sc, sh):
    ms = jnp.mean(x * x, axis=-1, keepdims=True)
    return (x * lax.rsqrt(ms + NORM_EPS)) * g * (1.0 + sc) + sh


def _mod_kernel(c_ref, w_ref, b_ref, o_ref):
    c = c_ref[...]
    cond = c * _sigmoid(c)
    o_ref[0] = jnp.dot(cond, w_ref[0], preferred_element_type=F32, precision=lax.Precision.HIGHEST) + b_ref[0]


def _modulation(c, mod_w, mod_b):
    depth, d, n = mod_w.shape
    b = c.shape[0]
    tn = 1536
    return pl.pallas_call(
        _mod_kernel,
        out_shape=jax.ShapeDtypeStruct((depth, b, n), F32),
        grid=(depth, n // tn),
        in_specs=[
            pl.BlockSpec((b, d), lambda i, j: (0, 0)),
            pl.BlockSpec((1, d, tn), lambda i, j: (i, 0, j)),
            pl.BlockSpec((1, 1, tn), lambda i, j: (i, 0, j)),
        ],
        out_specs=pl.BlockSpec((1, b, tn), lambda i, j: (i, 0, j)),
        compiler_params=_cparams(("parallel", "parallel")),
        name="modulation",
    )(c, mod_w, mod_b.reshape(depth, 1, n))


def _in_attn_kernel(x_ref, mod_ref, g_ref, w_ref, cg_ref, z_ref, gate_ref):
    m = mod_ref[0]
    h = _norm_mod(x_ref[0], g_ref[...], m[1:2], m[0:1]).astype(BF16)
    lo = lax.broadcasted_iota(I32, (1, LANES), 1) < HEAD_DIM
    groups = ((0, 512), (512, 1024), (1024, 1536), (1536, 2048), (2048, 2816), (2816, 2944))
    for c0, c1 in groups:
        z = jnp.dot(h, w_ref[:, c0:c1], preferred_element_type=F32)
        for s in range((c1 - c0) // LANES):
            a0 = c0 + s * LANES
            zs = z[:, s * LANES:(s + 1) * LANES]
            if a0 == COL_GATE:
                gate_ref[0] = _sigmoid(zs)
                continue
            if a0 // LANES in NORM_SLABS:
                z2 = zs * zs
                s_lo = jnp.sum(jnp.where(lo, z2, 0.0), axis=-1, keepdims=True)
                s_hi = jnp.sum(jnp.where(lo, 0.0, z2), axis=-1, keepdims=True)
                inv = jnp.where(lo, lax.rsqrt(s_lo * (1.0 / HEAD_DIM) + NORM_EPS),
                                lax.rsqrt(s_hi * (1.0 / HEAD_DIM) + NORM_EPS))
                zs = zs * inv
            z_ref[0, :, a0:a0 + LANES] = (zs * cg_ref[:, a0:a0 + LANES]).astype(BF16)


def _attn_in_proj(x, mod, gain, w_pad, colgain, tm):
    b, s, d = x.shape
    return pl.pallas_call(
        _in_attn_kernel,
        out_shape=(jax.ShapeDtypeStruct((b, s, COL_GATE), BF16), jax.ShapeDtypeStruct((b, s, LANES), F32)),
        grid=(b, s // tm),
        in_specs=[
            pl.BlockSpec((1, tm, d), lambda i, j: (i, j, 0)),
            pl.BlockSpec((1, 6, d), lambda i, j: (i, 0, 0)),
            pl.BlockSpec((1, d), lambda i, j: (0, 0)),
            pl.BlockSpec((d, W_IN_PAD), lambda i, j: (0, 0)),
            pl.BlockSpec((1, W_IN_PAD), lambda i, j: (0, 0)),
        ],
        out_specs=(pl.BlockSpec((1, tm, COL_GATE), lambda i, j: (i, j, 0)),
                   pl.BlockSpec((1, tm, LANES), lambda i, j: (i, j, 0))),
        compiler_params=_cparams(("parallel", "parallel")),
        name="attn_in_proj",
    )(x, mod, gain, w_pad, colgain)


FIXED_SHIFT_LIMIT = 57.0
FIXED_TILES_PER_STEP = 4
LOG2E = math.log2(math.e)


def _score_bound(gain_q, gain_k):
    return (math.sqrt(HEAD_DIM) * LOG2E) * jnp.max(jnp.abs(gain_q * gain_k))


def _fixed_update(ss, vts, shift, l_ref, acc_ref):
    ps = [jnp.exp2(s - shift) for s in ss]
    l_ref[...] += sum(jnp.sum(p, axis=0, keepdims=True) for p in ps)
    acc_ref[...] += sum(jnp.dot(vt, p.astype(BF16), preferred_element_type=F32) for vt, p in zip(vts, ps))


def _online_update(ss, vts, m_ref, l_ref, acc_ref):
    for s, vt in zip(ss, vts):
        m_old = m_ref[...]
        m_new = jnp.maximum(m_old, jnp.max(s, axis=0, keepdims=True))
        alpha = jnp.exp2(m_old - m_new)
        p = jnp.exp2(s - m_new)
        l_ref[...] = alpha * l_ref[...] + jnp.sum(p, axis=0, keepdims=True)
        acc_ref[...] = alpha * acc_ref[...] + jnp.dot(vt, p.astype(BF16), preferred_element_type=F32)
        m_ref[...] = m_new


def _tile_loops(n_full, tiles, width):
    def group(jj, c):
        tiles(tuple(width * jj + u for u in range(width)), (False,) * width)
        return c

    lax.fori_loop(0, n_full // width, group, 0)
    rem = n_full % width
    base = n_full - rem
    for r in range(width):
        @pl.when(rem == r)
        def _():
            tiles(tuple(base + u for u in range(r)) + (n_full,), (False,) * r + (True,))


def _diff_kernel(mb_ref, lam_ref, qT_ref, k_ref, vT_ref, sub_ref, o_ref, m0, m1, l0, l1, a0, a1, *, tq, tk, lam_init):
    i = pl.program_id(2)
    m_refs, l_refs, acc_refs = (m0, m1), (l0, l1), (a0, a1)
    q = qT_ref[0, 0]
    zero = jnp.zeros((HEAD_DIM, tq), BF16)
    qp = (jnp.concatenate([q[0], zero], axis=0), jnp.concatenate([zero, q[1]], axis=0))
    for mm in range(2):
        m_refs[mm][...] = jnp.full(m_refs[mm].shape, NEG, F32)
        l_refs[mm][...] = jnp.zeros(l_refs[mm].shape, F32)
        acc_refs[mm][...] = jnp.zeros(acc_refs[mm].shape, F32)
    shift = mb_ref[0]

    def tiles(js, masks, fixed):
        kts = [k_ref[0, pl.ds(pl.multiple_of(j * tk, tk), tk), :] for j in js]
        vts = [vT_ref[0, 0, j] for j in js]
        for mm in range(2):
            ss = []
            for j, kt, masked in zip(js, kts, masks):
                s = jnp.dot(kt, qp[mm], preferred_element_type=F32)
                if masked:
                    kpos = j * tk + lax.broadcasted_iota(I32, (tk, tq), 0)
                    t = i * tq + lax.broadcasted_iota(I32, (tk, tq), 1)
                    s = jnp.where(kpos <= t, s, NEG)
                ss.append(s)
            if fixed:
                _fixed_update(ss, vts, shift, l_refs[mm], acc_refs[mm])
            else:
                _online_update(ss, vts, m_refs[mm], l_refs[mm], acc_refs[mm])

    n_full = (i * tq) // tk

    @pl.when(shift <= FIXED_SHIFT_LIMIT)
    def _():
        _tile_loops(n_full, functools.partial(tiles, fixed=True), FIXED_TILES_PER_STEP)

    @pl.when(shift > FIXED_SHIFT_LIMIT)
    def _():
        _tile_loops(n_full, functools.partial(tiles, fixed=False), 1)

    lv = lam_ref[...]
    lam = (jnp.exp(jnp.sum(lv[0:1] * lv[1:2], axis=-1, keepdims=True))
           - jnp.exp(jnp.sum(lv[2:3] * lv[3:4], axis=-1, keepdims=True)) + lam_init)
    o = a0[...] / l0[...] - lam * (a1[...] / l1[...])
    ms = jnp.mean(o * o, axis=0, keepdims=True)
    o = o * lax.rsqrt(ms + NORM_EPS) * sub_ref[...] * (1.0 - lam_init)
    o_ref[0] = o.T.astype(BF16)


def _diff_attention(zb, qT, vT, shift, lam_vecs, subln, lam_init, tq, tk):
    b, s, _ = zb.shape
    kern = functools.partial(_diff_kernel, tq=tq, tk=tk, lam_init=lam_init)
    dv = 2 * HEAD_DIM
    return pl.pallas_call(
        kern,
        out_shape=jax.ShapeDtypeStruct((b, s, DIFF_HEADS * dv), BF16),
        grid=(b, DIFF_HEADS, s // tq),
        in_specs=[
            pl.BlockSpec(memory_space=pltpu.SMEM),
            pl.BlockSpec((4, HEAD_DIM), lambda bi, h, i: (0, 0)),
            pl.BlockSpec((1, 1, 2, HEAD_DIM, tq), lambda bi, h, i: (bi, h, 0, 0, i)),
            pl.BlockSpec((1, s, LANES), lambda bi, h, i: (bi, 0, COL_DK // LANES + h)),
            pl.BlockSpec((1, 1, s // tk, 2 * HEAD_DIM, tk), lambda bi, h, i: (bi, h, 0, 0, 0)),
            pl.BlockSpec((2 * HEAD_DIM, 1), lambda bi, h, i: (0, 0)),
        ],
        out_specs=pl.BlockSpec((1, tq, LANES), lambda bi, h, i: (bi, i, h)),
        scratch_shapes=[pltpu.VMEM((1, tq), F32)] * 4 + [pltpu.VMEM((dv, tq), F32)] * 2,
        compiler_params=_cparams(("parallel", "parallel", "arbitrary")),
        name="diff_attention",
    )(shift.reshape(1), lam_vecs, qT, zb, vT, subln.reshape(-1, 1))


def _gelu_tanh(v):
    return 0.5 * v * (1.0 + jnp.tanh(math.sqrt(2.0 / math.pi) * (v + 0.044715 * (v * v * v))))


def _cmp_kernel(c_ref, pos_ref, w1_ref, w2_ref, kg_ref, o_ref, *, n_cmp):
    jh = pl.program_id(1)
    half = CMP_STRIDE * HEAD_DIM
    c = c_ref[0, 0]
    w1a = w1_ref[0, :half, :].astype(BF16)
    w1b = w1_ref[0, half:, :].astype(BF16)
    pos = jnp.broadcast_to(pos_ref[0], (8, 2 * half)).astype(BF16)
    bias = (jnp.dot(pos[:, :half], w1a, preferred_element_type=F32)
            + jnp.dot(pos[:, half:], w1b, preferred_element_type=F32))[0:1]
    u = jnp.dot(c, w1a, preferred_element_type=F32)
    v = jnp.dot(c, w1b, preferred_element_type=F32)
    ncp = u.shape[0]
    hid = _gelu_tanh(u + pltpu.roll(v, ncp - 1, 0) + bias)
    y = jnp.dot(hid.astype(BF16), w2_ref[0].astype(BF16), preferred_element_type=F32)
    yn = y * lax.rsqrt(jnp.mean(y * y, axis=-1, keepdims=True) + NORM_EPS) * kg_ref[...]
    y = jnp.where(jh < NSA_KV_HEADS, yn, y)
    row = lax.broadcasted_iota(I32, y.shape, 0)
    o_ref[0, 0] = jnp.where(row < n_cmp, y, 0.0)


def _compress(chunks, pos, w1, w2, kgain):
    b, _, ncp, cd = chunks.shape
    kern = functools.partial(_cmp_kernel, n_cmp=ncp - 1)
    return pl.pallas_call(
        kern,
        out_shape=jax.ShapeDtypeStruct((b, 4, ncp, HEAD_DIM), F32),
        grid=(b, 4),
        in_specs=[
            pl.BlockSpec((1, 1, ncp, cd), lambda bi, j: (bi, j, 0, 0)),
            pl.BlockSpec((1, 1, 2 * cd), lambda bi, j: (j // NSA_KV_HEADS, 0, 0)),
            pl.BlockSpec((1, 2 * cd, CMP_HIDDEN), lambda bi, j: (j // NSA_KV_HEADS, 0, 0)),
            pl.BlockSpec((1, CMP_HIDDEN, HEAD_DIM), lambda bi, j: (j // NSA_KV_HEADS, 0, 0)),
            pl.BlockSpec((1, HEAD_DIM), lambda bi, j: (0, 0)),
        ],
        out_specs=pl.BlockSpec((1, 1, ncp, HEAD_DIM), lambda bi, j: (bi, j, 0, 0)),
        compiler_params=_cparams(("parallel", "parallel")),
        name="nsa_compress",
    )(chunks, pos, w1, w2, kgain)


def _nsa_kernel(mb_ref, qT_ref, ksel_ref, vselT_ref, kwin_ref, vwinT_ref, kc_ref, vcT_ref, ovl_ref, gate_ref, o_ref,
                selb_sc, m_s, m_w, l_s, l_w, a_s, a_w, *, tq, n_top):
    hk = pl.program_id(1)
    i = pl.program_id(2)
    g4 = NSA_GROUP
    nq = g4 * tq
    q4 = jnp.concatenate([qT_ref[0, 0, g] for g in range(g4)], axis=1)
    half = lax.broadcasted_iota(I32, (2 * HEAD_DIM, nq), 0) // HEAD_DIM
    qp = jnp.where(half == hk, jnp.concatenate([q4, q4], axis=0), jnp.zeros((), BF16))
    t1 = i * tq + lax.broadcasted_iota(I32, (1, tq), 1)

    def tile4(a):
        return jnp.concatenate([a] * g4, axis=1)

    ncp = kc_ref.shape[1]
    sc = jnp.dot(kc_ref[0], qp, preferred_element_type=F32)
    cend = CMP_STRIDE * lax.broadcasted_iota(I32, (ncp, tq), 0) + (CMP_BLOCK - 1)
    sc = sc + tile4(jnp.where(cend <= t1, 0.0, NEG))
    e = jnp.where(sc > 0.5 * NEG, jnp.exp2(sc - jnp.max(sc, axis=0, keepdims=True)), 0.0)
    p = e / jnp.maximum(jnp.sum(e, axis=0, keepdims=True), 1e-30)
    o_c = jnp.dot(vcT_ref[0, 0], p.astype(BF16), preferred_element_type=F32)

    psum = p[:, 0:tq]
    for g in range(1, g4):
        psum = psum + p[:, g * tq:(g + 1) * tq]
    p_hi = psum.astype(BF16)
    p_lo = (psum - p_hi.astype(F32)).astype(BF16)
    imp = (jnp.dot(ovl_ref[...], p_hi, preferred_element_type=F32)
           + jnp.dot(ovl_ref[...], p_lo, preferred_element_type=F32))
    n_sel = imp.shape[0]
    jrow = lax.broadcasted_iota(I32, (n_sel, tq), 0)
    cur = t1 // SEL_BLOCK
    forced = (jrow == 0) | (jrow == cur) | (jrow == cur - 1)
    score = jnp.where(forced, SEL_FORCED_SCORE, jnp.where(jrow <= cur, imp, NEG))
    selb = jnp.full((n_sel, tq), NEG, F32)
    jrow_f = jrow.astype(F32)
    for _ in range(n_top):
        best = jnp.max(score, axis=0, keepdims=True)
        pick = jnp.min(jnp.where(score == best, jrow_f, float(n_sel)), axis=0, keepdims=True)
        hit = jrow_f == pick
        selb = jnp.where(hit, 0.0, selb)
        score = jnp.where(hit, TAKEN, score)
    selb_sc[...] = selb

    for ref in (m_s, m_w):
        ref[...] = jnp.full(ref.shape, NEG, F32)
    for ref in (l_s, l_w, a_s, a_w):
        ref[...] = jnp.zeros(ref.shape, F32)
    tk = tq
    bpt = tk // SEL_BLOCK
    krow = lax.broadcasted_iota(I32, (tk, tq), 0)
    shift_s, shift_w = mb_ref[0], mb_ref[1]
    fixed = jnp.maximum(shift_s, shift_w) <= FIXED_SHIFT_LIMIT

    def sel_tiles(js, masks, fixed):
        ss = []
        for j, masked in zip(js, masks):
            kt = ksel_ref[0, pl.ds(pl.multiple_of(j * tk, tk), tk), :]
            rows = [jnp.broadcast_to(selb_sc[pl.ds(j * bpt + r, 1), :], (SEL_BLOCK, tq)) for r in range(bpt)]
            bias = jnp.concatenate(rows, axis=0)
            if masked:
                bias = jnp.where(j * tk + krow <= t1, bias, NEG)
            ss.append(jnp.dot(kt, qp, preferred_element_type=F32) + tile4(bias))
        vts = [vselT_ref[0, 0, j] for j in js]
        if fixed:
            _fixed_update(ss, vts, shift_s, l_s, a_s)
        else:
            _online_update(ss, vts, m_s, l_s, a_s)

    def win_tiles(js, fixed):
        ss = []
        for j in js:
            kt = kwin_ref[0, pl.ds(pl.multiple_of(j * tk, tk), tk), :]
            kpos = j * tk + krow
            bias = jnp.where(kpos <= t1, jnp.where(kpos > t1 - WINDOW, 0.0, NEG), NEG)
            ss.append(jnp.dot(kt, qp, preferred_element_type=F32) + tile4(bias))
        vts = [vwinT_ref[0, 0, j] for j in js]
        if fixed:
            _fixed_update(ss, vts, shift_w, l_w, a_w)
        else:
            _online_update(ss, vts, m_w, l_w, a_w)

    max_win = (WINDOW + tk - 1) // tk + 1
    n_win = jnp.minimum(i + 1, max_win)
    for use_fixed in (True, False):
        @pl.when(fixed == use_fixed)
        def _():
            _tile_loops(i, functools.partial(sel_tiles, fixed=use_fixed), FIXED_TILES_PER_STEP if use_fixed else 1)
            for c in range(1, max_win + 1):
                @pl.when(n_win == c)
                def _():
                    win_tiles(tuple(i - (c - 1) + u for u in range(c)), use_fixed)

    def gate_row(br):
        gt = gate_ref[0, 0, br]
        return jnp.concatenate([gt[g:g + 1] for g in range(g4)], axis=1)

    out = (o_c * gate_row(0) + (a_s[...] / l_s[...]) * gate_row(1) + (a_w[...] / l_w[...]) * gate_row(2))
    stacked = jnp.concatenate([out[:, g * tq:(g + 1) * tq] for g in range(g4)], axis=0)
    o_ref[0] = stacked.T.astype(BF16)


def _nsa_attention(zb, qT, vselT, vwinT, kc, vcT, ovl, gatesT, shifts, tq):
    b, s, _ = zb.shape
    nt = s // tq
    n_sel = s // SEL_BLOCK
    ncp = kc.shape[1]
    kern = functools.partial(_nsa_kernel, tq=tq, n_top=min(SEL_TOP, n_sel))
    gd = NSA_GROUP * HEAD_DIM
    nq = NSA_GROUP * tq
    return pl.pallas_call(
        kern,
        out_shape=jax.ShapeDtypeStruct((b, s, NSA_Q_HEADS * HEAD_DIM), BF16),
        grid=(b, NSA_KV_HEADS, nt),
        in_specs=[
            pl.BlockSpec(memory_space=pltpu.SMEM),
            pl.BlockSpec((1, 1, NSA_GROUP, HEAD_DIM, tq), lambda bi, h, i: (bi, h, 0, 0, i)),
            pl.BlockSpec((1, s, LANES), lambda bi, h, i: (bi, 0, COL_KSEL // LANES)),
            pl.BlockSpec((1, 1, nt, HEAD_DIM, tq), lambda bi, h, i: (bi, h, 0, 0, 0)),
            pl.BlockSpec((1, s, LANES), lambda bi, h, i: (bi, 0, COL_KWIN // LANES)),
            pl.BlockSpec((1, 1, nt, HEAD_DIM, tq), lambda bi, h, i: (bi, h, 0, 0, 0)),
            pl.BlockSpec((1, ncp, LANES), lambda bi, h, i: (bi, 0, 0)),
            pl.BlockSpec((1, 1, HEAD_DIM, ncp), lambda bi, h, i: (bi, h, 0, 0)),
            pl.BlockSpec((n_sel, ncp), lambda bi, h, i: (0, 0)),
            pl.BlockSpec((1, 1, 3, NSA_GROUP, tq), lambda bi, h, i: (bi, h, 0, 0, i)),
        ],
        out_specs=pl.BlockSpec((1, tq, gd), lambda bi, h, i: (bi, i, h)),
        scratch_shapes=([pltpu.VMEM((n_sel, tq), F32)] + [pltpu.VMEM((1, nq), F32)] * 4
                        + [pltpu.VMEM((HEAD_DIM, nq), F32)] * 2),
        compiler_params=_cparams(("parallel", "parallel", "arbitrary")),
        name="nsa_attention",
    )(shifts, qT, zb, vselT, zb, vwinT, kc, vcT, ovl, gatesT)


def _in_conv_kernel(x_ref, mod_ref, g_ref, w_ref, b_ref, u_ref):
    m = mod_ref[0]
    d = x_ref.shape[-1]
    h = _norm_mod(x_ref[0], g_ref[...], m[1:2], m[0:1]).astype(BF16)
    a = jnp.dot(h, w_ref[:, :d], preferred_element_type=F32) + b_ref[:, :d]
    g = jnp.dot(h, w_ref[:, d:], preferred_element_type=F32) + b_ref[:, d:]
    u_ref[0] = a * _sigmoid(g)


def _conv_in_proj(x, mod, gain, w, bias, tm):
    b, s, d = x.shape
    return pl.pallas_call(
        _in_conv_kernel,
        out_shape=jax.ShapeDtypeStruct((b, s, d), F32),
        grid=(b, s // tm),
        in_specs=[
            pl.BlockSpec((1, tm, d), lambda i, j: (i, j, 0)),
            pl.BlockSpec((1, 6, d), lambda i, j: (i, 0, 0)),
            pl.BlockSpec((1, d), lambda i, j: (0, 0)),
            pl.BlockSpec((d, 2 * d), lambda i, j: (0, 0)),
            pl.BlockSpec((1, 2 * d), lambda i, j: (0, 0)),
        ],
        out_specs=pl.BlockSpec((1, tm, d), lambda i, j: (i, j, 0)),
        compiler_params=_cparams(("parallel", "parallel")),
        name="conv_in_proj",
    )(x, mod, gain, w, bias)


HALO = 32


def _dwconv_kernel(u_ref, halo_ref, w_ref, b_ref, lg_ref, lb_ref, o_ref, buf_sc, acc_sc, *, tm):
    i = pl.program_id(1)
    d = u_ref.shape[-1]
    buf_sc[0:HALO, :] = jnp.where(i > 0, halo_ref[0], 0.0)
    buf_sc[HALO:, :] = u_ref[0]
    off = HALO - (CONV_WIDTH - 1)
    cw = 256
    for c in range(d // cw):
        cs = slice(c * cw, (c + 1) * cw)
        acc = jnp.zeros((tm, cw), F32) + b_ref[:, cs]
        for j in range(CONV_WIDTH):
            acc = acc + w_ref[j:j + 1, cs] * buf_sc[off + j:off + j + tm, cs]
        acc_sc[:, cs] = acc
    y = acc_sc[...]
    mu = jnp.mean(y, axis=-1, keepdims=True)
    yc = y - mu
    var = jnp.mean(yc * yc, axis=-1, keepdims=True)
    yn = yc * lax.rsqrt(var + NORM_EPS) * lg_ref[...] + lb_ref[...]
    o_ref[0] = (yn * _sigmoid(yn)).astype(BF16)


def _dwconv_ln_swish(u, dw_w, dw_b, ln_g, ln_b, tm):
    b, s, d = u.shape
    kern = functools.partial(_dwconv_kernel, tm=tm)
    hb = tm // HALO
    return pl.pallas_call(
        kern,
        out_shape=jax.ShapeDtypeStruct((b, s, d), BF16),
        grid=(b, s // tm),
        in_specs=[
            pl.BlockSpec((1, tm, d), lambda bi, i: (bi, i, 0)),
            pl.BlockSpec((1, HALO, d), lambda bi, i: (bi, jnp.maximum(i * hb - 1, 0), 0)),
            pl.BlockSpec((CONV_WIDTH, d), lambda bi, i: (0, 0)),
            pl.BlockSpec((1, d), lambda bi, i: (0, 0)),
            pl.BlockSpec((1, d), lambda bi, i: (0, 0)),
            pl.BlockSpec((1, d), lambda bi, i: (0, 0)),
        ],
        out_specs=pl.BlockSpec((1, tm, d), lambda bi, i: (bi, i, 0)),
        scratch_shapes=[pltpu.VMEM((tm + HALO, d), F32), pltpu.VMEM((tm, d), F32)],
        compiler_params=_cparams(("parallel", "parallel")),
        name="dwconv_ln_swish",
    )(u, u, dw_w, dw_b, ln_g, ln_b)


def _out_router_kernel(a1_ref, a2_ref, w1_ref, w2_ref, b_ref, x_ref, mod_ref, g_ref, rwh_ref, rwl_ref, rb_ref,
                       xo_ref, h_ref, idx_ref, wt_ref):
    m = mod_ref[0]
    y = (jnp.dot(a1_ref[0], w1_ref[...], preferred_element_type=F32)
         + jnp.dot(a2_ref[0], w2_ref[...], preferred_element_type=F32) + b_ref[...])
    x = x_ref[0] + m[2:3] * y
    xo_ref[0] = x
    h = _norm_mod(x, g_ref[...], m[4:5], m[3:4])
    chunks = h.shape[1] // LANES
    for c in range(chunks):
        h_ref[0, pl.ds(c, h.shape[0], stride=chunks), :] = h[:, c * LANES:(c + 1) * LANES]
    h_hi = h.astype(BF16)
    h_lo = (h - h_hi.astype(F32)).astype(BF16)
    logits = (jnp.dot(h_hi, rwh_ref[...], preferred_element_type=F32)
              + jnp.dot(h_lo, rwh_ref[...], preferred_element_type=F32)
              + jnp.dot(h_hi, rwl_ref[...], preferred_element_type=F32)) + rb_ref[...]
    lane = lax.broadcasted_iota(I32, logits.shape, 1)
    lane_f = lane.astype(F32)
    idx_out = jnp.zeros(logits.shape, F32)
    val_out = jnp.full(logits.shape, NEG, F32)
    for k in range(TOP_K):
        best = jnp.max(logits, axis=-1, keepdims=True)
        pick = jnp.min(jnp.where(logits == best, lane_f, float(LANES)), axis=-1, keepdims=True)
        idx_out = jnp.where(lane == k, pick, idx_out)
        val_out = jnp.where(lane == k, best, val_out)
        logits = jnp.where(lane_f == pick, TAKEN, logits)
    e = jnp.where(lane < TOP_K, jnp.exp(val_out - jnp.max(val_out, axis=-1, keepdims=True)), 0.0)
    idx_ref[0] = idx_out.astype(I32)
    wt_ref[0] = e / jnp.sum(e, axis=-1, keepdims=True)


def _out_proj_router(a1, a2, c1, c2, w1, w2, bias, x, mod, gain, rw_hi, rw_lo, rb, tm):
    b, s, d = x.shape
    kw = w1.shape[0]
    tok = lambda i, j: (i, j, 0)
    const = lambda i, j: (0, 0)
    return pl.pallas_call(
        _out_router_kernel,
        out_shape=(jax.ShapeDtypeStruct((b, s, d), F32), jax.ShapeDtypeStruct((b, s * (d // LANES), LANES), F32),
                   jax.ShapeDtypeStruct((b, s, LANES), I32), jax.ShapeDtypeStruct((b, s, LANES), F32)),
        grid=(b, s // tm),
        in_specs=[
            pl.BlockSpec((1, tm, kw), lambda i, j: (i, j, c1)),
            pl.BlockSpec((1, tm, kw), lambda i, j: (i, j, c2)),
            pl.BlockSpec((kw, d), const),
            pl.BlockSpec((kw, d), const),
            pl.BlockSpec((1, d), const),
            pl.BlockSpec((1, tm, d), tok),
            pl.BlockSpec((1, 6, d), lambda i, j: (i, 0, 0)),
            pl.BlockSpec((1, d), const),
            pl.BlockSpec((d, LANES), const),
            pl.BlockSpec((d, LANES), const),
            pl.BlockSpec((1, LANES), const),
        ],
        out_specs=(pl.BlockSpec((1, tm, d), tok), pl.BlockSpec((1, tm * (d // LANES), LANES), tok),
                   pl.BlockSpec((1, tm, LANES), tok), pl.BlockSpec((1, tm, LANES), tok)),
        compiler_params=_cparams(("parallel", "parallel")),
        name="out_proj_router",
    )(a1, a2, w1, w2, bias, x, mod, gain, rw_hi, rw_lo, rb)


ROW_CHUNKS = D_MODEL // LANES
DEINT = 512
DUMP_ROWS = 512


def _expert_kernel(be_ref, nv_ref, gsrc0_ref, gsrc1_ref, gdstp_ref, gdst0_ref, h_ref, w1_ref, b1g_ref, b1l_ref,
                   w2_ref, b2_ref, pe_ref, po_ref, yt_ref, xbuf, ybuf, w1g_sc, w1l_sc, w2_sc, gsem, ssem,
                   *, n_blocks, dump_row):
    i = pl.program_id(0)
    cur = i % 2
    nxt = 1 - cur
    rows = MOE_BLOCK

    def tile_rows(first):
        return pl.ds(pl.multiple_of(first, ROW_CHUNKS), ROW_CHUNKS)

    def buf_rows(r):
        return pl.ds(r * ROW_CHUNKS, ROW_CHUNKS) if isinstance(r, int) else tile_rows(r * ROW_CHUNKS)

    def gather(idx_ref, slot, r):
        return pltpu.make_async_copy(h_ref.at[tile_rows(idx_ref[0, 0, r])], xbuf.at[slot, buf_rows(r)], gsem.at[slot])

    def scatter(dst_first, slot, r):
        return pltpu.make_async_copy(ybuf.at[slot, buf_rows(r)], yt_ref.at[tile_rows(dst_first)], ssem.at[slot])

    def wait_gather(slot):
        pltpu.make_async_copy(h_ref.at[pl.ds(0, rows * ROW_CHUNKS)], xbuf.at[slot], gsem.at[slot]).wait()

    def wait_scatter(slot):
        pltpu.make_async_copy(ybuf.at[slot], yt_ref.at[pl.ds(0, rows * ROW_CHUNKS)], ssem.at[slot]).wait()

    @pl.when(i == 0)
    def _():
        ybuf[...] = jnp.zeros(ybuf.shape, F32)

        def prime(r, c):
            gather(gsrc0_ref, 0, r).start()
            return c

        lax.fori_loop(0, rows, prime, 0)

    wait_gather(cur)

    @pl.when(i >= 1)
    def _():
        wait_scatter(cur)

    first = i == 0
    changed = jnp.logical_or(first, be_ref[i] != be_ref[jnp.maximum(i - 1, 0)])

    @pl.when(changed)
    def _():
        for c in range(w1_ref.shape[2] // DEINT):
            wc = w1_ref[0, :, c * DEINT:(c + 1) * DEINT].astype(BF16)
            cs = slice(c * (DEINT // 2), (c + 1) * (DEINT // 2))
            w1g_sc[:, cs] = jnp.dot(wc, pe_ref[...], preferred_element_type=F32).astype(BF16)
            w1l_sc[:, cs] = jnp.dot(wc, po_ref[...], preferred_element_type=F32).astype(BF16)
        w2_sc[...] = w2_ref[0].astype(BF16)

    for r in range(rows):
        gather(gsrc1_ref, nxt, r).start(priority=r % 2)
        scatter(jnp.where(first, dump_row + r * ROW_CHUNKS, gdstp_ref[0, 0, r]), nxt, r).start(priority=(r + 1) % 2)

    def chunk(c):
        return pl.ds(c, rows, stride=ROW_CHUNKS)

    x = jnp.concatenate([xbuf[cur, chunk(c), :] for c in range(ROW_CHUNKS)], axis=1)
    row = lax.broadcasted_iota(I32, x.shape, 0)
    xb = jnp.where(row < nv_ref[i], x, 0.0).astype(BF16)
    glu = jnp.dot(xb, w1g_sc[...], preferred_element_type=F32) + b1g_ref[0]
    lin = jnp.dot(xb, w1l_sc[...], preferred_element_type=F32) + b1l_ref[0]
    glu = jnp.minimum(glu, SWIGLU_LIMIT)
    lin = jnp.clip(lin, -SWIGLU_LIMIT, SWIGLU_LIMIT)
    act = glu * _sigmoid(SWIGLU_ALPHA * glu) * (lin + 1.0)
    y = jnp.dot(act.astype(BF16), w2_sc[...], preferred_element_type=F32) + b2_ref[0]
    for c in range(ROW_CHUNKS):
        ybuf[cur, chunk(c), :] = y[:, c * LANES:(c + 1) * LANES]

    @pl.when(i == n_blocks - 1)
    def _():
        def last(r, c):
            scatter(gdst0_ref[0, 0, r], cur, r).start()
            return c

        lax.fori_loop(0, rows, last, 0)
        wait_gather(nxt)
        wait_scatter(nxt)
        wait_scatter(cur)


def _experts(h, gsrc, gdst, block_expert, n_valid, w1, b1g, b1l, w2, b2):
    t = h.shape[0] // ROW_CHUNKS
    n_blocks = gsrc.shape[0]
    d, f2 = w1.shape[1:]
    f = f2 // 2
    sel = np.arange(DEINT)[:, None] - 2 * np.arange(DEINT // 2)[None, :]
    p_even = jnp.asarray((sel == 0).astype(np.float32), dtype=BF16)
    p_odd = jnp.asarray((sel == 1).astype(np.float32), dtype=BF16)
    last = n_blocks - 1

    def wsel(i, be, nv):
        return (be[i], 0, 0)

    const = lambda i, be, nv: (0, 0)
    idx_blk = (1, 1, MOE_BLOCK)
    kern = functools.partial(_expert_kernel, n_blocks=n_blocks, dump_row=TOP_K * t * ROW_CHUNKS)
    return pl.pallas_call(
        kern,
        out_shape=jax.ShapeDtypeStruct(((TOP_K * t + DUMP_ROWS) * ROW_CHUNKS, LANES), F32),
        grid_spec=pltpu.PrefetchScalarGridSpec(
            num_scalar_prefetch=2,
            grid=(n_blocks,),
            in_specs=[
                pl.BlockSpec(idx_blk, lambda i, be, nv: (i, 0, 0), memory_space=pltpu.SMEM),
                pl.BlockSpec(idx_blk, lambda i, be, nv: (jnp.minimum(i + 1, last), 0, 0), memory_space=pltpu.SMEM),
                pl.BlockSpec(idx_blk, lambda i, be, nv: (jnp.maximum(i - 1, 0), 0, 0), memory_space=pltpu.SMEM),
                pl.BlockSpec(idx_blk, lambda i, be, nv: (i, 0, 0), memory_space=pltpu.SMEM),
                pl.BlockSpec(memory_space=pl.ANY),
                pl.BlockSpec((1, d, f2), wsel),
                pl.BlockSpec((1, 1, f), wsel),
                pl.BlockSpec((1, 1, f), wsel),
                pl.BlockSpec((1, f, d), wsel),
                pl.BlockSpec((1, 1, d), wsel),
                pl.BlockSpec((DEINT, DEINT // 2), const),
                pl.BlockSpec((DEINT, DEINT // 2), const),
            ],
            out_specs=pl.BlockSpec(memory_space=pl.ANY),
            scratch_shapes=[pltpu.VMEM((2, MOE_BLOCK * ROW_CHUNKS, LANES), F32),
                            pltpu.VMEM((2, MOE_BLOCK * ROW_CHUNKS, LANES), F32),
                            pltpu.VMEM((d, f), BF16), pltpu.VMEM((d, f), BF16), pltpu.VMEM((f, d), BF16),
                            pltpu.SemaphoreType.DMA((2,)), pltpu.SemaphoreType.DMA((2,))],
        ),
        compiler_params=_cparams(("arbitrary",)),
        name="moe_experts",
    )(block_expert, n_valid, gsrc, gsrc, gdst, gdst, h, w1, b1g, b1l, w2, b2, p_even, p_odd)


def _combine_kernel(y0_ref, y1_ref, y2_ref, y3_ref, wt_ref, x_ref, mod_ref, o_ref):
    wt = wt_ref[0]
    g2 = mod_ref[0][5:6]
    y_refs = (y0_ref, y1_ref, y2_ref, y3_ref)
    wk = [wt[:, k:k + 1] for k in range(TOP_K)]
    tm = wt.shape[0]
    for c in range(ROW_CHUNKS):
        chunk = pl.ds(c, tm, stride=ROW_CHUNKS)
        y = wk[0] * y_refs[0][chunk, :]
        for k in range(1, TOP_K):
            y = y + wk[k] * y_refs[k][chunk, :]
        cs = slice(c * LANES, (c + 1) * LANES)
        o_ref[0, :, cs] = x_ref[0, :, cs] + g2[:, cs] * y


def _combine(yt, wts, x, mod, tm):
    b, s, d = x.shape
    nt = s // tm
    tiles = b * nt
    tok = lambda i, j: (i, j, 0)

    def yspec(k):
        return pl.BlockSpec((tm * ROW_CHUNKS, LANES), lambda i, j: (k * tiles + i * nt + j, 0))

    return pl.pallas_call(
        _combine_kernel,
        out_shape=jax.ShapeDtypeStruct((b, s, d), F32),
        grid=(b, nt),
        in_specs=[yspec(k) for k in range(TOP_K)] + [
            pl.BlockSpec((1, tm, LANES), tok),
            pl.BlockSpec((1, tm, d), tok),
            pl.BlockSpec((1, 6, d), lambda i, j: (i, 0, 0)),
        ],
        out_specs=pl.BlockSpec((1, tm, d), tok),
        compiler_params=_cparams(("arbitrary", "arbitrary")),
        name="moe_combine",
    )(yt, yt, yt, yt, wts, x, mod)


def _route_slots(top_idx, n_blocks):
    t = top_idx.shape[0]
    n_assign = t * TOP_K
    n_slots = n_blocks * MOE_BLOCK
    sorted_e, order = lax.sort_key_val(top_idx.reshape(-1), jnp.arange(n_assign, dtype=I32))
    start = jnp.searchsorted(sorted_e, jnp.arange(N_EXPERTS + 1, dtype=I32), side='left').astype(I32)
    counts = start[1:] - start[:-1]
    padded = (counts + MOE_BLOCK - 1) // MOE_BLOCK * MOE_BLOCK
    pad_end = jnp.cumsum(padded)
    pad_start = pad_end - padded
    blk_start = jnp.arange(n_blocks, dtype=I32) * MOE_BLOCK
    block_expert = jnp.minimum(jnp.searchsorted(pad_end, blk_start, side='right'), N_EXPERTS - 1).astype(I32)
    n_valid = jnp.clip(pad_start[block_expert] + counts[block_expert] - blk_start, 0, MOE_BLOCK).astype(I32)
    within = jnp.arange(MOE_BLOCK, dtype=I32)[None, :]
    valid = within < n_valid[:, None]
    rank = (blk_start - pad_start[block_expert] + start[block_expert])[:, None] + within
    slot_a = order[jnp.clip(rank, 0, n_assign - 1)]
    slot = blk_start[:, None] + within
    gsrc = jnp.where(valid, slot_a // TOP_K, 0) * ROW_CHUNKS
    gdst = jnp.where(valid, (slot_a % TOP_K) * t + slot_a // TOP_K, n_assign + slot % DUMP_ROWS) * ROW_CHUNKS
    shape = (n_blocks, 1, MOE_BLOCK)
    return gsrc.reshape(shape), gdst.reshape(shape), block_expert, n_valid


def _moe(x_mid, h, idx, wts, mod, w1, b1g, b1l, w2, b2, tm):
    b, s, d = x_mid.shape
    t = b * s
    n_blocks = -(-t * TOP_K // MOE_BLOCK) + N_EXPERTS
    top_idx = idx.reshape(t, LANES)[:, :TOP_K]
    gsrc, gdst, block_expert, n_valid = _route_slots(top_idx, n_blocks)
    yt = _experts(h.reshape(t * ROW_CHUNKS, LANES), gsrc, gdst, block_expert, n_valid, w1, b1g, b1l, w2, b2)
    return _combine(yt, wts, x_mid, mod, tm)


def _pad_cols(w, n):
    return jnp.pad(w, ((0, 0), (0, n - w.shape[-1])))


def _attn_colgain(diff_qk_gain, nsa_q_gain, nsa_k_gain):
    scale = HEAD_DIM ** -0.5 * LOG2E
    ones = jnp.ones((LANES,), F32)
    parts = [jnp.tile(diff_qk_gain[0] * scale, 8), jnp.tile(diff_qk_gain[1], 8), jnp.ones((512,), F32),
             jnp.tile(nsa_q_gain * scale, 8), ones, ones, jnp.tile(nsa_k_gain[1], 2), ones,
             jnp.tile(nsa_k_gain[2], 2), ones, ones]
    return jnp.concatenate(parts).reshape(1, W_IN_PAD)


def _overlap_matrix(s):
    n_sel = s // SEL_BLOCK
    ncp = s // CMP_STRIDE
    c_start = np.arange(ncp) * CMP_STRIDE
    s_start = np.arange(n_sel) * SEL_BLOCK
    ovl = (c_start[None, :] < s_start[:, None] + SEL_BLOCK) & (c_start[None, :] + CMP_BLOCK > s_start[:, None])
    ovl[:, ncp - 1] = False
    return jnp.asarray(ovl.astype(np.float32), dtype=BF16)


def _attention_layer(x, mod, gain, w_in, w_out, diff_qk_gain, diff_lambda, diff_subln, nsa_q_gain, nsa_k_gain,
                     cmp_pos, cmp_w1, cmp_w2, lam_init):
    b, s, d = x.shape
    tq_d, tk_d, tq_n = 256, 512, 256
    zb, gsig = _attn_in_proj(x, mod, gain, _pad_cols(w_in, W_IN_PAD).astype(BF16),
                             _attn_colgain(diff_qk_gain, nsa_q_gain, nsa_k_gain), 512)

    def cols(c0, n):
        return zb[:, :, c0:c0 + n]

    dqT = cols(COL_DQ, 512).reshape(b, s, DIFF_HEADS, 2, HEAD_DIM).transpose(0, 2, 3, 4, 1)
    dvT = cols(COL_DV, 512).reshape(b, s // tk_d, tk_d, DIFF_HEADS, 2 * HEAD_DIM).transpose(0, 3, 1, 4, 2)
    o_diff = _diff_attention(zb, dqT, dvT, _score_bound(diff_qk_gain[0], diff_qk_gain[1]), diff_lambda, diff_subln,
                             lam_init, tq_d, tk_d)

    nqT = cols(COL_NQ, 512).reshape(b, s, NSA_KV_HEADS, NSA_GROUP, HEAD_DIM).transpose(0, 2, 3, 4, 1)

    def val_t(c0):
        return cols(c0, LANES).reshape(b, s // tq_n, tq_n, NSA_KV_HEADS, HEAD_DIM).transpose(0, 3, 1, 4, 2)

    ncp = s // CMP_STRIDE
    chunks = cols(COL_KCMP, 2 * LANES).reshape(b, ncp, CMP_STRIDE, 2, NSA_KV_HEADS, HEAD_DIM)
    chunks = chunks.transpose(0, 3, 4, 1, 2, 5).reshape(b, 4, ncp, CMP_STRIDE * HEAD_DIM)
    cmp_out = _compress(chunks, cmp_pos.reshape(2, 1, CMP_BLOCK * HEAD_DIM), cmp_w1, cmp_w2, nsa_k_gain[0:1])
    kc = cmp_out[:, 0:2].transpose(0, 2, 1, 3).reshape(b, ncp, LANES).astype(BF16)
    vcT = cmp_out[:, 2:4].transpose(0, 1, 3, 2).astype(BF16)
    gatesT = gsig[:, :, :NSA_Q_HEADS * 3].reshape(b, s, NSA_KV_HEADS, NSA_GROUP, 3).transpose(0, 2, 4, 3, 1)
    shifts = jnp.stack([_score_bound(nsa_q_gain, nsa_k_gain[1]), _score_bound(nsa_q_gain, nsa_k_gain[2])])
    o_nsa = _nsa_attention(zb, nqT, val_t(COL_VSEL), val_t(COL_VWIN), kc, vcT, _overlap_matrix(s), gatesT, shifts,
                           tq_n)
    return o_diff, o_nsa


def kernel(x, c, mod_w, mod_b, norm_mix, norm_ffn, attn_w_in, attn_w_out, diff_qk_gain, diff_lambda, diff_subln,
           nsa_q_gain, nsa_k_gain, nsa_cmp_pos, nsa_cmp_w1, nsa_cmp_w2, conv_pw1_w, conv_pw1_b, conv_dw_w,
           conv_dw_b, conv_ln_g, conv_ln_b, conv_pw2_w, conv_pw2_b, router_w, router_b, moe_w1, moe_b1, moe_w2,
           moe_b2):
    b, s, d = x.shape
    depth = mod_w.shape[0]
    tm = 512 if s % 512 == 0 else 256
    mods = _modulation(c, mod_w, mod_b).reshape(depth, b, 6, d)
    half = d // 2
    for i in range(depth):
        mod = mods[i]
        j = i // 2
        if i % 2 == 0:
            lam_init = 0.8 - 0.6 * math.exp(-0.3 * i)
            o_diff, o_nsa = _attention_layer(
                x, mod, norm_mix[i:i + 1], attn_w_in[j], attn_w_out[j], diff_qk_gain[j], diff_lambda[j],
                diff_subln[j], nsa_q_gain[j], nsa_k_gain[j], nsa_cmp_pos[j], nsa_cmp_w1[j], nsa_cmp_w2[j], lam_init)
            a1, a2, c1, c2 = o_diff, o_nsa, 0, 0
            w_o = attn_w_out[j].astype(BF16)
            bias = jnp.zeros((1, d), F32)
        else:
            u = _conv_in_proj(x, mod, norm_mix[i:i + 1], conv_pw1_w[j].astype(BF16), conv_pw1_b[j:j + 1], tm)
            v = _dwconv_ln_swish(u, conv_dw_w[j].reshape(CONV_WIDTH, d), conv_dw_b[j:j + 1], conv_ln_g[j:j + 1],
                                 conv_ln_b[j:j + 1], 256)
            a1, a2, c1, c2 = v, v, 0, 1
            w_o = conv_pw2_w[j].astype(BF16)
            bias = conv_pw2_b[j:j + 1]
        rw = _pad_cols(router_w[i], LANES)
        rw_hi = rw.astype(BF16)
        rw_lo = (rw - rw_hi.astype(F32)).astype(BF16)
        rb = jnp.concatenate([router_b[i], jnp.full((LANES - N_EXPERTS,), NEG, F32)]).reshape(1, LANES)
        x_mid, h, idx, wts = _out_proj_router(a1, a2, c1, c2, w_o[:half], w_o[half:], bias, x, mod,
                                              norm_ffn[i:i + 1], rw_hi, rw_lo, rb, tm)
        x = _moe(x_mid, h, idx, wts, mod, moe_w1[i], moe_b1[i][:, None, 0::2], moe_b1[i][:, None, 1::2], moe_w2[i],
                 moe_b2[i][:, None, :], 256)
    return x
```

```python
import functools
import math

import jax
import jax.numpy as jnp
import numpy as np
from jax import lax
from jax.experimental import pallas as pl
from jax.experimental.pallas import tpu as pltpu

F32 = jnp.float32
BF16 = jnp.bfloat16
I32 = jnp.int32

D_MODEL = 1024
HEAD_DIM = 64
DIFF_HEADS = 4
NSA_Q_HEADS = 8
NSA_KV_HEADS = 2
NSA_GROUP = NSA_Q_HEADS // NSA_KV_HEADS
CMP_BLOCK = 32
CMP_STRIDE = 16
CMP_HIDDEN = 256
SEL_BLOCK = 64
SEL_TOP = 16
SEL_FORCED_SCORE = 1.0e4
WINDOW = 512
CONV_WIDTH = 31
N_EXPERTS = 32
TOP_K = 4
SWIGLU_ALPHA = 1.702
SWIGLU_LIMIT = 7.0
MOE_BLOCK = 256
NORM_EPS = 1e-6

LANES = 128
NEG = -1e30
TAKEN = -3e38
VMEM_LIMIT = 56 * 1024 * 1024

COL_DQ, COL_DK, COL_DV, COL_NQ = 0, 512, 1024, 1536
COL_KCMP, COL_VCMP, COL_KSEL, COL_VSEL, COL_KWIN, COL_VWIN = 2048, 2176, 2304, 2432, 2560, 2688
COL_GATE = 2816
W_IN_PAD = 2944
NORM_SLABS = frozenset(list(range(0, 8)) + list(range(12, 16)) + [COL_KSEL // LANES, COL_KWIN // LANES])


def _cparams(sem):
    return pltpu.CompilerParams(dimension_semantics=sem, vmem_limit_bytes=VMEM_LIMIT)


def _sigmoid(v):
    return 1.0 / (1.0 + jnp.exp(-v))


def _norm_mod(x, g, sc, sh):
    ms = jnp.mean(x * x, axis=-1, keepdims=True)
    return (x * lax.rsqrt(ms + NORM_EPS)) * g * (1.0 + sc) + sh


def _mod_kernel(c_ref, w_ref, b_ref, o_ref):
    c = c_ref[...]
    cond = c * _sigmoid(c)
    o_ref[0] = jnp.dot(cond, w_ref[0], preferred_element_type=F32, precision=lax.Precision.HIGHEST) + b_ref[0]


def _modulation(c, mod_w, mod_b):
    depth, d, n = mod_w.shape
    b = c.shape[0]
    tn = 1536
    return pl.pallas_call(
        _mod_kernel,
        out_shape=jax.ShapeDtypeStruct((depth, b, n), F32),
        grid=(depth, n // tn),
        in_specs=[
            pl.BlockSpec((b, d), lambda i, j: (0, 0)),
            pl.BlockSpec((1, d, tn), lambda i, j: (i, 0, j)),
            pl.BlockSpec((1, 1, tn), lambda i, j: (i, 0, j)),
        ],
        out_specs=pl.BlockSpec((1, b, tn), lambda i, j: (i, 0, j)),
        compiler_params=_cparams(("parallel", "parallel")),
        name="modulation",
    )(c, mod_w, mod_b.reshape(depth, 1, n))


KZ_DK, KZ_KCMP, KZ_VCMP, KZ_KSEL, KZ_KWIN, KZ_WIDTH = 0, 512, 640, 768, 896, 1024
KEY_COLS = {COL_DK + 128 * u: KZ_DK + 128 * u for u in range(4)}
KEY_COLS.update({COL_KCMP: KZ_KCMP, COL_VCMP: KZ_VCMP, COL_KSEL: KZ_KSEL, COL_KWIN: KZ_KWIN})
ATTN_TM = 512
NSA_TILE = 256
GATE_ROWS = 32


def _in_attn_kernel(x_ref, mod_ref, g_ref, w_ref, cg_ref, z_ref, dqT_ref, dvT_ref, nqT_ref, vsT_ref, vwT_ref, gT_ref):
    m = mod_ref[0]
    h = _norm_mod(x_ref[0], g_ref[...], m[1:2], m[0:1]).astype(BF16)
    lo = lax.broadcasted_iota(I32, (1, LANES), 1) < HEAD_DIM
    groups = ((0, 512), (512, 1024), (1024, 1536), (1536, 2048), (2048, 2816), (2816, 2944))
    halves = ATTN_TM // NSA_TILE
    for c0, c1 in groups:
        z = jnp.dot(h, w_ref[:, c0:c1], preferred_element_type=F32)
        for s in range((c1 - c0) // LANES):
            a0 = c0 + s * LANES
            zs = z[:, s * LANES:(s + 1) * LANES]
            if a0 == COL_GATE:
                gT_ref[0] = _sigmoid(zs).T[:GATE_ROWS]
                continue
            if a0 // LANES in NORM_SLABS:
                z2 = zs * zs
                s_lo = jnp.sum(jnp.where(lo, z2, 0.0), axis=-1, keepdims=True)
                s_hi = jnp.sum(jnp.where(lo, 0.0, z2), axis=-1, keepdims=True)
                inv = jnp.where(lo, lax.rsqrt(s_lo * (1.0 / HEAD_DIM) + NORM_EPS),
                                lax.rsqrt(s_hi * (1.0 / HEAD_DIM) + NORM_EPS))
                zs = zs * inv
            zs = zs * cg_ref[:, a0:a0 + LANES]
            if COL_DQ <= a0 < COL_DK:
                dqT_ref[0, a0 - COL_DQ:a0 - COL_DQ + LANES, :] = zs.T.astype(BF16)
            elif COL_DV <= a0 < COL_NQ:
                dvT_ref[0, 0, a0 - COL_DV:a0 - COL_DV + LANES, :] = zs.T.astype(BF16)
            elif COL_NQ <= a0 < COL_KCMP:
                nqT_ref[0, a0 - COL_NQ:a0 - COL_NQ + LANES, :] = zs.T.astype(BF16)
            elif a0 in (COL_VSEL, COL_VWIN):
                zt = zs.T.astype(BF16)
                ref = vsT_ref if a0 == COL_VSEL else vwT_ref
                for u in range(halves):
                    ref[0, u] = zt[:, u * NSA_TILE:(u + 1) * NSA_TILE]
            else:
                k0 = KEY_COLS[a0]
                z_ref[0, :, k0:k0 + LANES] = zs.astype(BF16)


def _attn_in_proj(x, mod, gain, w_pad, colgain):
    b, s, d = x.shape
    tm = ATTN_TM
    halves = tm // NSA_TILE
    tok = lambda i, j: (i, j, 0)
    featT = lambda i, j: (i, 0, j)
    return pl.pallas_call(
        _in_attn_kernel,
        out_shape=(jax.ShapeDtypeStruct((b, s, KZ_WIDTH), BF16),
                   jax.ShapeDtypeStruct((b, 512, s), BF16),
                   jax.ShapeDtypeStruct((b, s // tm, 512, tm), BF16),
                   jax.ShapeDtypeStruct((b, 512, s), BF16),
                   jax.ShapeDtypeStruct((b, s // NSA_TILE, LANES, NSA_TILE), BF16),
                   jax.ShapeDtypeStruct((b, s // NSA_TILE, LANES, NSA_TILE), BF16),
                   jax.ShapeDtypeStruct((b, GATE_ROWS, s), F32)),
        grid=(b, s // tm),
        in_specs=[
            pl.BlockSpec((1, tm, d), tok),
            pl.BlockSpec((1, 6, d), lambda i, j: (i, 0, 0)),
            pl.BlockSpec((1, d), lambda i, j: (0, 0)),
            pl.BlockSpec((d, W_IN_PAD), lambda i, j: (0, 0)),
            pl.BlockSpec((1, W_IN_PAD), lambda i, j: (0, 0)),
        ],
        out_specs=(pl.BlockSpec((1, tm, KZ_WIDTH), tok),
                   pl.BlockSpec((1, 512, tm), featT),
                   pl.BlockSpec((1, 1, 512, tm), lambda i, j: (i, j, 0, 0)),
                   pl.BlockSpec((1, 512, tm), featT),
                   pl.BlockSpec((1, halves, LANES, NSA_TILE), lambda i, j: (i, j, 0, 0)),
                   pl.BlockSpec((1, halves, LANES, NSA_TILE), lambda i, j: (i, j, 0, 0)),
                   pl.BlockSpec((1, GATE_ROWS, tm), featT)),
        compiler_params=_cparams(("parallel", "parallel")),
        name="attn_in_proj",
    )(x, mod, gain, w_pad, colgain)


FIXED_SHIFT_LIMIT = 57.0
FIXED_TILES_PER_STEP = 4
LOG2E = math.log2(math.e)


def _score_bound(gain_q, gain_k):
    return (math.sqrt(HEAD_DIM) * LOG2E) * jnp.max(jnp.abs(gain_q * gain_k))


def _fixed_update(ss, vts, shift, l_ref, acc_ref):
    ps = [jnp.exp2(s - shift) for s in ss]
    l_ref[...] += sum(jnp.sum(p, axis=0, keepdims=True) for p in ps)
    acc_ref[...] += sum(jnp.dot(vt, p.astype(BF16), preferred_element_type=F32) for vt, p in zip(vts, ps))


def _online_update(ss, vts, m_ref, l_ref, acc_ref):
    for s, vt in zip(ss, vts):
        m_old = m_ref[...]
        m_new = jnp.maximum(m_old, jnp.max(s, axis=0, keepdims=True))
        alpha = jnp.exp2(m_old - m_new)
        p = jnp.exp2(s - m_new)
        l_ref[...] = alpha * l_ref[...] + jnp.sum(p, axis=0, keepdims=True)
        acc_ref[...] = alpha * acc_ref[...] + jnp.dot(vt, p.astype(BF16), preferred_element_type=F32)
        m_ref[...] = m_new


def _tile_loops(n_full, tiles, width):
    def group(jj, c):
        tiles(tuple(width * jj + u for u in range(width)), (False,) * width)
        return c

    lax.fori_loop(0, n_full // width, group, 0)
    rem = n_full % width
    base = n_full - rem
    for r in range(width):
        @pl.when(rem == r)
        def _():
            tiles(tuple(base + u for u in range(r)) + (n_full,), (False,) * r + (True,))


def _diff_kernel(mb_ref, lam_ref, qT_ref, k_ref, vT_ref, sub_ref, o_ref, m0, m1, l0, l1, a0, a1, *, tq, tk, lam_init):
    i = pl.program_id(2)
    m_refs, l_refs, acc_refs = (m0, m1), (l0, l1), (a0, a1)
    q = qT_ref[0, 0]
    zero = jnp.zeros((HEAD_DIM, tq), BF16)
    qp = (jnp.concatenate([q[0], zero], axis=0), jnp.concatenate([zero, q[1]], axis=0))
    for mm in range(2):
        m_refs[mm][...] = jnp.full(m_refs[mm].shape, NEG, F32)
        l_refs[mm][...] = jnp.zeros(l_refs[mm].shape, F32)
        acc_refs[mm][...] = jnp.zeros(acc_refs[mm].shape, F32)
    shift = mb_ref[0]

    def tiles(js, masks, fixed):
        kts = [k_ref[0, pl.ds(pl.multiple_of(j * tk, tk), tk), :] for j in js]
        vts = [vT_ref[0, j] for j in js]
        for mm in range(2):
            ss = []
            for j, kt, masked in zip(js, kts, masks):
                s = jnp.dot(kt, qp[mm], preferred_element_type=F32)
                if masked:
                    kpos = j * tk + lax.broadcasted_iota(I32, (tk, tq), 0)
                    t = i * tq + lax.broadcasted_iota(I32, (tk, tq), 1)
                    s = jnp.where(kpos <= t, s, NEG)
                ss.append(s)
            if fixed:
                _fixed_update(ss, vts, shift, l_refs[mm], acc_refs[mm])
            else:
                _online_update(ss, vts, m_refs[mm], l_refs[mm], acc_refs[mm])

    n_full = (i * tq) // tk

    @pl.when(shift <= FIXED_SHIFT_LIMIT)
    def _():
        _tile_loops(n_full, functools.partial(tiles, fixed=True), FIXED_TILES_PER_STEP)

    @pl.when(shift > FIXED_SHIFT_LIMIT)
    def _():
        _tile_loops(n_full, functools.partial(tiles, fixed=False), 1)

    lv = lam_ref[...]
    lam = (jnp.exp(jnp.sum(lv[0:1] * lv[1:2], axis=-1, keepdims=True))
           - jnp.exp(jnp.sum(lv[2:3] * lv[3:4], axis=-1, keepdims=True)) + lam_init)
    o = a0[...] / l0[...] - lam * (a1[...] / l1[...])
    ms = jnp.mean(o * o, axis=0, keepdims=True)
    o = o * lax.rsqrt(ms + NORM_EPS) * sub_ref[...] * (1.0 - lam_init)
    o_ref[0] = o.T.astype(BF16)


def _diff_attention(zb, qT, vT, shift, lam_vecs, subln, lam_init, tq, tk):
    b, s, _ = zb.shape
    kern = functools.partial(_diff_kernel, tq=tq, tk=tk, lam_init=lam_init)
    dv = 2 * HEAD_DIM
    return pl.pallas_call(
        kern,
        out_shape=jax.ShapeDtypeStruct((b, s, DIFF_HEADS * dv), BF16),
        grid=(b, DIFF_HEADS, s // tq),
        in_specs=[
            pl.BlockSpec(memory_space=pltpu.SMEM),
            pl.BlockSpec((4, HEAD_DIM), lambda bi, h, i: (0, 0)),
            pl.BlockSpec((1, 1, 2, HEAD_DIM, tq), lambda bi, h, i: (bi, h, 0, 0, i)),
            pl.BlockSpec((1, s, LANES), lambda bi, h, i: (bi, 0, KZ_DK // LANES + h)),
            pl.BlockSpec((1, s // tk, 2 * HEAD_DIM, tk), lambda bi, h, i: (bi, 0, h, 0)),
            pl.BlockSpec((2 * HEAD_DIM, 1), lambda bi, h, i: (0, 0)),
        ],
        out_specs=pl.BlockSpec((1, tq, LANES), lambda bi, h, i: (bi, i, h)),
        scratch_shapes=[pltpu.VMEM((1, tq), F32)] * 4 + [pltpu.VMEM((dv, tq), F32)] * 2,
        compiler_params=_cparams(("parallel", "parallel", "arbitrary")),
        name="diff_attention",
    )(shift.reshape(1), lam_vecs, qT, zb, vT, subln.reshape(-1, 1))


def _gelu_tanh(v):
    return 0.5 * v * (1.0 + jnp.tanh(math.sqrt(2.0 / math.pi) * (v + 0.044715 * (v * v * v))))


def _cmp_kernel(c_ref, pos_ref, w1_ref, w2_ref, kg_ref, o_ref, *, n_cmp):
    jh = pl.program_id(1)
    half = CMP_STRIDE * HEAD_DIM
    c = c_ref[0, 0]
    w1a = w1_ref[0, :half, :].astype(BF16)
    w1b = w1_ref[0, half:, :].astype(BF16)
    pos = jnp.broadcast_to(pos_ref[0], (8, 2 * half)).astype(BF16)
    bias = (jnp.dot(pos[:, :half], w1a, preferred_element_type=F32)
            + jnp.dot(pos[:, half:], w1b, preferred_element_type=F32))[0:1]
    u = jnp.dot(c, w1a, preferred_element_type=F32)
    v = jnp.dot(c, w1b, preferred_element_type=F32)
    ncp = u.shape[0]
    hid = _gelu_tanh(u + pltpu.roll(v, ncp - 1, 0) + bias)
    y = jnp.dot(hid.astype(BF16), w2_ref[0].astype(BF16), preferred_element_type=F32)
    yn = y * lax.rsqrt(jnp.mean(y * y, axis=-1, keepdims=True) + NORM_EPS) * kg_ref[...]
    y = jnp.where(jh < NSA_KV_HEADS, yn, y)
    row = lax.broadcasted_iota(I32, y.shape, 0)
    o_ref[0, 0] = jnp.where(row < n_cmp, y, 0.0)


def _compress(chunks, pos, w1, w2, kgain):
    b, _, ncp, cd = chunks.shape
    kern = functools.partial(_cmp_kernel, n_cmp=ncp - 1)
    return pl.pallas_call(
        kern,
        out_shape=jax.ShapeDtypeStruct((b, 4, ncp, HEAD_DIM), F32),
        grid=(b, 4),
        in_specs=[
            pl.BlockSpec((1, 1, ncp, cd), lambda bi, j: (bi, j, 0, 0)),
            pl.BlockSpec((1, 1, 2 * cd), lambda bi, j: (j // NSA_KV_HEADS, 0, 0)),
            pl.BlockSpec((1, 2 * cd, CMP_HIDDEN), lambda bi, j: (j // NSA_KV_HEADS, 0, 0)),
            pl.BlockSpec((1, CMP_HIDDEN, HEAD_DIM), lambda bi, j: (j // NSA_KV_HEADS, 0, 0)),
            pl.BlockSpec((1, HEAD_DIM), lambda bi, j: (0, 0)),
        ],
        out_specs=pl.BlockSpec((1, 1, ncp, HEAD_DIM), lambda bi, j: (bi, j, 0, 0)),
        compiler_params=_cparams(("parallel", "parallel")),
        name="nsa_compress",
    )(chunks, pos, w1, w2, kgain)


def _nsa_kernel(mb_ref, qT_ref, ksel_ref, vselT_ref, kwin_ref, vwinT_ref, kc_ref, vcT_ref, ovl_ref, gate_ref, o_ref,
                selb_sc, m_s, m_w, l_s, l_w, a_s, a_w, *, tq, n_top):
    hk = pl.program_id(1)
    i = pl.program_id(2)
    g4 = NSA_GROUP
    nq = g4 * tq
    q4 = jnp.concatenate([qT_ref[0, 0, g] for g in range(g4)], axis=1)
    half = lax.broadcasted_iota(I32, (2 * HEAD_DIM, nq), 0) // HEAD_DIM
    qp = jnp.where(half == hk, jnp.concatenate([q4, q4], axis=0), jnp.zeros((), BF16))
    t1 = i * tq + lax.broadcasted_iota(I32, (1, tq), 1)

    def tile4(a):
        return jnp.concatenate([a] * g4, axis=1)

    ncp = kc_ref.shape[1]
    sc = jnp.dot(kc_ref[0], qp, preferred_element_type=F32)
    cend = CMP_STRIDE * lax.broadcasted_iota(I32, (ncp, tq), 0) + (CMP_BLOCK - 1)
    sc = sc + tile4(jnp.where(cend <= t1, 0.0, NEG))
    e = jnp.where(sc > 0.5 * NEG, jnp.exp2(sc - jnp.max(sc, axis=0, keepdims=True)), 0.0)
    p = e / jnp.maximum(jnp.sum(e, axis=0, keepdims=True), 1e-30)
    o_c = jnp.dot(vcT_ref[0, 0], p.astype(BF16), preferred_element_type=F32)

    psum = p[:, 0:tq]
    for g in range(1, g4):
        psum = psum + p[:, g * tq:(g + 1) * tq]
    p_hi = psum.astype(BF16)
    p_lo = (psum - p_hi.astype(F32)).astype(BF16)
    imp = (jnp.dot(ovl_ref[...], p_hi, preferred_element_type=F32)
           + jnp.dot(ovl_ref[...], p_lo, preferred_element_type=F32))
    n_sel = imp.shape[0]
    jrow = lax.broadcasted_iota(I32, (n_sel, tq), 0)
    cur = t1 // SEL_BLOCK
    forced = (jrow == 0) | (jrow == cur) | (jrow == cur - 1)
    score = jnp.where(forced, SEL_FORCED_SCORE, jnp.where(jrow <= cur, imp, NEG))
    selb = jnp.full((n_sel, tq), NEG, F32)
    jrow_f = jrow.astype(F32)
    for _ in range(n_top):
        best = jnp.max(score, axis=0, keepdims=True)
        pick = jnp.min(jnp.where(score == best, jrow_f, float(n_sel)), axis=0, keepdims=True)
        hit = jrow_f == pick
        selb = jnp.where(hit, 0.0, selb)
        score = jnp.where(hit, TAKEN, score)
    selb_sc[...] = selb

    for ref in (m_s, m_w):
        ref[...] = jnp.full(ref.shape, NEG, F32)
    for ref in (l_s, l_w, a_s, a_w):
        ref[...] = jnp.zeros(ref.shape, F32)
    tk = tq
    bpt = tk // SEL_BLOCK
    krow = lax.broadcasted_iota(I32, (tk, tq), 0)
    shift_s, shift_w = mb_ref[0], mb_ref[1]
    fixed = jnp.maximum(shift_s, shift_w) <= FIXED_SHIFT_LIMIT

    def sel_tiles(js, masks, fixed):
        ss = []
        for j, masked in zip(js, masks):
            kt = ksel_ref[0, pl.ds(pl.multiple_of(j * tk, tk), tk), :]
            rows = [jnp.broadcast_to(selb_sc[pl.ds(j * bpt + r, 1), :], (SEL_BLOCK, tq)) for r in range(bpt)]
            bias = jnp.concatenate(rows, axis=0)
            if masked:
                bias = jnp.where(j * tk + krow <= t1, bias, NEG)
            ss.append(jnp.dot(kt, qp, preferred_element_type=F32) + tile4(bias))
        vts = [vselT_ref[0, j] for j in js]
        if fixed:
            _fixed_update(ss, vts, shift_s, l_s, a_s)
        else:
            _online_update(ss, vts, m_s, l_s, a_s)

    def win_tiles(js, fixed):
        ss = []
        for j in js:
            kt = kwin_ref[0, pl.ds(pl.multiple_of(j * tk, tk), tk), :]
            kpos = j * tk + krow
            bias = jnp.where(kpos <= t1, jnp.where(kpos > t1 - WINDOW, 0.0, NEG), NEG)
            ss.append(jnp.dot(kt, qp, preferred_element_type=F32) + tile4(bias))
        vts = [vwinT_ref[0, j] for j in js]
        if fixed:
            _fixed_update(ss, vts, shift_w, l_w, a_w)
        else:
            _online_update(ss, vts, m_w, l_w, a_w)

    max_win = (WINDOW + tk - 1) // tk + 1
    n_win = jnp.minimum(i + 1, max_win)
    for use_fixed in (True, False):
        @pl.when(fixed == use_fixed)
        def _():
            _tile_loops(i, functools.partial(sel_tiles, fixed=use_fixed), FIXED_TILES_PER_STEP if use_fixed else 1)
            for c in range(1, max_win + 1):
                @pl.when(n_win == c)
                def _():
                    win_tiles(tuple(i - (c - 1) + u for u in range(c)), use_fixed)

    def gate_row(br):
        gt = gate_ref[0, 0, br]
        return jnp.concatenate([gt[g:g + 1] for g in range(g4)], axis=1)

    out = (o_c * gate_row(0) + (a_s[...] / l_s[...]) * gate_row(1) + (a_w[...] / l_w[...]) * gate_row(2))
    stacked = jnp.concatenate([out[:, g * tq:(g + 1) * tq] for g in range(g4)], axis=0)
    o_ref[0] = stacked.T.astype(BF16)


def _nsa_attention(zb, qT, vselT, vwinT, kc, vcT, ovl, gatesT, shifts, tq):
    b, s, _ = zb.shape
    nt = s // tq
    n_sel = s // SEL_BLOCK
    ncp = kc.shape[1]
    kern = functools.partial(_nsa_kernel, tq=tq, n_top=min(SEL_TOP, n_sel))
    gd = NSA_GROUP * HEAD_DIM
    nq = NSA_GROUP * tq
    return pl.pallas_call(
        kern,
        out_shape=jax.ShapeDtypeStruct((b, s, NSA_Q_HEADS * HEAD_DIM), BF16),
        grid=(b, NSA_KV_HEADS, nt),
        in_specs=[
            pl.BlockSpec(memory_space=pltpu.SMEM),
            pl.BlockSpec((1, 1, NSA_GROUP, HEAD_DIM, tq), lambda bi, h, i: (bi, h, 0, 0, i)),
            pl.BlockSpec((1, s, LANES), lambda bi, h, i: (bi, 0, KZ_KSEL // LANES)),
            pl.BlockSpec((1, nt, HEAD_DIM, tq), lambda bi, h, i: (bi, 0, h, 0)),
            pl.BlockSpec((1, s, LANES), lambda bi, h, i: (bi, 0, KZ_KWIN // LANES)),
            pl.BlockSpec((1, nt, HEAD_DIM, tq), lambda bi, h, i: (bi, 0, h, 0)),
            pl.BlockSpec((1, ncp, LANES), lambda bi, h, i: (bi, 0, 0)),
            pl.BlockSpec((1, 1, HEAD_DIM, ncp), lambda bi, h, i: (bi, h, 0, 0)),
            pl.BlockSpec((n_sel, ncp), lambda bi, h, i: (0, 0)),
            pl.BlockSpec((1, 1, 3, NSA_GROUP, tq), lambda bi, h, i: (bi, h, 0, 0, i)),
        ],
        out_specs=pl.BlockSpec((1, tq, gd), lambda bi, h, i: (bi, i, h)),
        scratch_shapes=([pltpu.VMEM((n_sel, tq), F32)] + [pltpu.VMEM((1, nq), F32)] * 4
                        + [pltpu.VMEM((HEAD_DIM, nq), F32)] * 2),
        compiler_params=_cparams(("parallel", "parallel", "arbitrary")),
        name="nsa_attention",
    )(shifts, qT, zb, vselT, zb, vwinT, kc, vcT, ovl, gatesT)


def _in_conv_kernel(x_ref, mod_ref, g_ref, w_ref, b_ref, u_ref):
    m = mod_ref[0]
    d = x_ref.shape[-1]
    h = _norm_mod(x_ref[0], g_ref[...], m[1:2], m[0:1]).astype(BF16)
    a = jnp.dot(h, w_ref[:, :d], preferred_element_type=F32) + b_ref[:, :d]
    g = jnp.dot(h, w_ref[:, d:], preferred_element_type=F32) + b_ref[:, d:]
    u_ref[0] = a * _sigmoid(g)


def _conv_in_proj(x, mod, gain, w, bias, tm):
    b, s, d = x.shape
    return pl.pallas_call(
        _in_conv_kernel,
        out_shape=jax.ShapeDtypeStruct((b, s, d), F32),
        grid=(b, s // tm),
        in_specs=[
            pl.BlockSpec((1, tm, d), lambda i, j: (i, j, 0)),
            pl.BlockSpec((1, 6, d), lambda i, j: (i, 0, 0)),
            pl.BlockSpec((1, d), lambda i, j: (0, 0)),
            pl.BlockSpec((d, 2 * d), lambda i, j: (0, 0)),
            pl.BlockSpec((1, 2 * d), lambda i, j: (0, 0)),
        ],
        out_specs=pl.BlockSpec((1, tm, d), lambda i, j: (i, j, 0)),
        compiler_params=_cparams(("parallel", "parallel")),
        name="conv_in_proj",
    )(x, mod, gain, w, bias)


HALO = 32


def _dwconv_kernel(u_ref, halo_ref, w_ref, b_ref, lg_ref, lb_ref, o_ref, buf_sc, acc_sc, *, tm):
    i = pl.program_id(1)
    d = u_ref.shape[-1]
    buf_sc[0:HALO, :] = jnp.where(i > 0, halo_ref[0], 0.0)
    buf_sc[HALO:, :] = u_ref[0]
    off = HALO - (CONV_WIDTH - 1)
    cw = 256
    for c in range(d // cw):
        cs = slice(c * cw, (c + 1) * cw)
        acc = jnp.zeros((tm, cw), F32) + b_ref[:, cs]
        for sh in range(8):
            taps = [q for q in range(off, off + CONV_WIDTH) if q % 8 == sh]
            shifted = buf_sc[sh:max(taps) + tm, cs]
            for q in taps:
                acc = acc + w_ref[q - off:q - off + 1, cs] * shifted[q - sh:q - sh + tm]
        acc_sc[:, cs] = acc
    y = acc_sc[...]
    mu = jnp.mean(y, axis=-1, keepdims=True)
    yc = y - mu
    var = jnp.mean(yc * yc, axis=-1, keepdims=True)
    yn = yc * lax.rsqrt(var + NORM_EPS) * lg_ref[...] + lb_ref[...]
    o_ref[0] = (yn * _sigmoid(yn)).astype(BF16)


def _dwconv_ln_swish(u, dw_w, dw_b, ln_g, ln_b, tm):
    b, s, d = u.shape
    kern = functools.partial(_dwconv_kernel, tm=tm)
    hb = tm // HALO
    return pl.pallas_call(
        kern,
        out_shape=jax.ShapeDtypeStruct((b, s, d), BF16),
        grid=(b, s // tm),
        in_specs=[
            pl.BlockSpec((1, tm, d), lambda bi, i: (bi, i, 0)),
            pl.BlockSpec((1, HALO, d), lambda bi, i: (bi, jnp.maximum(i * hb - 1, 0), 0)),
            pl.BlockSpec((CONV_WIDTH, d), lambda bi, i: (0, 0)),
            pl.BlockSpec((1, d), lambda bi, i: (0, 0)),
            pl.BlockSpec((1, d), lambda bi, i: (0, 0)),
            pl.BlockSpec((1, d), lambda bi, i: (0, 0)),
        ],
        out_specs=pl.BlockSpec((1, tm, d), lambda bi, i: (bi, i, 0)),
        scratch_shapes=[pltpu.VMEM((tm + HALO, d), F32), pltpu.VMEM((tm, d), F32)],
        compiler_params=_cparams(("parallel", "parallel")),
        name="dwconv_ln_swish",
    )(u, u, dw_w, dw_b, ln_g, ln_b)


def _out_router_kernel(a1_ref, a2_ref, w1_ref, w2_ref, b_ref, x_ref, mod_ref, g_ref, rwh_ref, rwl_ref, rb_ref,
                       xo_ref, h_ref, idx_ref, wt_ref):
    m = mod_ref[0]
    y = (jnp.dot(a1_ref[0], w1_ref[...], preferred_element_type=F32)
         + jnp.dot(a2_ref[0], w2_ref[...], preferred_element_type=F32) + b_ref[...])
    x = x_ref[0] + m[2:3] * y
    xo_ref[0] = x
    h = _norm_mod(x, g_ref[...], m[4:5], m[3:4])
    chunks = h.shape[1] // LANES
    for c in range(chunks):
        h_ref[0, pl.ds(c, h.shape[0], stride=chunks), :] = h[:, c * LANES:(c + 1) * LANES]
    h_hi = h.astype(BF16)
    h_lo = (h - h_hi.astype(F32)).astype(BF16)
    logits = (jnp.dot(h_hi, rwh_ref[...], preferred_element_type=F32)
              + jnp.dot(h_lo, rwh_ref[...], preferred_element_type=F32)
              + jnp.dot(h_hi, rwl_ref[...], preferred_element_type=F32)) + rb_ref[...]
    lane = lax.broadcasted_iota(I32, logits.shape, 1)
    lane_f = lane.astype(F32)
    idx_out = jnp.zeros(logits.shape, F32)
    val_out = jnp.full(logits.shape, NEG, F32)
    for k in range(TOP_K):
        best = jnp.max(logits, axis=-1, keepdims=True)
        pick = jnp.min(jnp.where(logits == best, lane_f, float(LANES)), axis=-1, keepdims=True)
        idx_out = jnp.where(lane == k, pick, idx_out)
        val_out = jnp.where(lane == k, best, val_out)
        logits = jnp.where(lane_f == pick, TAKEN, logits)
    e = jnp.where(lane < TOP_K, jnp.exp(val_out - jnp.max(val_out, axis=-1, keepdims=True)), 0.0)
    idx_ref[0] = idx_out.astype(I32)
    wt_ref[0] = e / jnp.sum(e, axis=-1, keepdims=True)


def _out_proj_router(a1, a2, c1, c2, w1, w2, bias, x, mod, gain, rw_hi, rw_lo, rb, tm):
    b, s, d = x.shape
    kw = w1.shape[0]
    tok = lambda i, j: (i, j, 0)
    const = lambda i, j: (0, 0)
    return pl.pallas_call(
        _out_router_kernel,
        out_shape=(jax.ShapeDtypeStruct((b, s, d), F32), jax.ShapeDtypeStruct((b, s * (d // LANES), LANES), F32),
                   jax.ShapeDtypeStruct((b, s, LANES), I32), jax.ShapeDtypeStruct((b, s, LANES), F32)),
        grid=(b, s // tm),
        in_specs=[
            pl.BlockSpec((1, tm, kw), lambda i, j: (i, j, c1)),
            pl.BlockSpec((1, tm, kw), lambda i, j: (i, j, c2)),
            pl.BlockSpec((kw, d), const),
            pl.BlockSpec((kw, d), const),
            pl.BlockSpec((1, d), const),
            pl.BlockSpec((1, tm, d), tok),
            pl.BlockSpec((1, 6, d), lambda i, j: (i, 0, 0)),
            pl.BlockSpec((1, d), const),
            pl.BlockSpec((d, LANES), const),
            pl.BlockSpec((d, LANES), const),
            pl.BlockSpec((1, LANES), const),
        ],
        out_specs=(pl.BlockSpec((1, tm, d), tok), pl.BlockSpec((1, tm * (d // LANES), LANES), tok),
                   pl.BlockSpec((1, tm, LANES), tok), pl.BlockSpec((1, tm, LANES), tok)),
        compiler_params=_cparams(("parallel", "parallel")),
        name="out_proj_router",
    )(a1, a2, w1, w2, bias, x, mod, gain, rw_hi, rw_lo, rb)


ROW_CHUNKS = D_MODEL // LANES
DEINT = 512
DUMP_ROWS = 512


def _expert_kernel(be_ref, nv_ref, gsrc0_ref, gsrc1_ref, gdstp_ref, gdst0_ref, h_ref, w1_ref, b1g_ref, b1l_ref,
                   w2_ref, b2_ref, pe_ref, po_ref, yt_ref, xbuf, ybuf, w1g_sc, w1l_sc, w2_sc, gsem, ssem,
                   *, n_blocks, dump_row):
    i = pl.program_id(0)
    cur = i % 2
    nxt = 1 - cur
    rows = MOE_BLOCK

    def tile_rows(first):
        return pl.ds(pl.multiple_of(first, ROW_CHUNKS), ROW_CHUNKS)

    def buf_rows(r):
        return pl.ds(r * ROW_CHUNKS, ROW_CHUNKS) if isinstance(r, int) else tile_rows(r * ROW_CHUNKS)

    def gather(idx_ref, slot, r):
        return pltpu.make_async_copy(h_ref.at[tile_rows(idx_ref[0, 0, r])], xbuf.at[slot, buf_rows(r)], gsem.at[slot])

    def scatter(dst_first, slot, r):
        return pltpu.make_async_copy(ybuf.at[slot, buf_rows(r)], yt_ref.at[tile_rows(dst_first)], ssem.at[slot])

    def wait_gather(slot):
        pltpu.make_async_copy(h_ref.at[pl.ds(0, rows * ROW_CHUNKS)], xbuf.at[slot], gsem.at[slot]).wait()

    def wait_scatter(slot):
        pltpu.make_async_copy(ybuf.at[slot], yt_ref.at[pl.ds(0, rows * ROW_CHUNKS)], ssem.at[slot]).wait()

    @pl.when(i == 0)
    def _():
        ybuf[...] = jnp.zeros(ybuf.shape, F32)

        def prime(r, c):
            gather(gsrc0_ref, 0, r).start()
            return c

        lax.fori_loop(0, rows, prime, 0)

    wait_gather(cur)

    @pl.when(i >= 1)
    def _():
        wait_scatter(cur)

    first = i == 0
    changed = jnp.logical_or(first, be_ref[i] != be_ref[jnp.maximum(i - 1, 0)])

    @pl.when(changed)
    def _():
        for c in range(w1_ref.shape[2] // DEINT):
            wc = w1_ref[0, :, c * DEINT:(c + 1) * DEINT].astype(BF16)
            cs = slice(c * (DEINT // 2), (c + 1) * (DEINT // 2))
            w1g_sc[:, cs] = jnp.dot(wc, pe_ref[...], preferred_element_type=F32).astype(BF16)
            w1l_sc[:, cs] = jnp.dot(wc, po_ref[...], preferred_element_type=F32).astype(BF16)
        w2_sc[...] = w2_ref[0].astype(BF16)

    for r in range(rows):
        gather(gsrc1_ref, nxt, r).start(priority=r % 2)
        scatter(jnp.where(first, dump_row + r * ROW_CHUNKS, gdstp_ref[0, 0, r]), nxt, r).start(priority=(r + 1) % 2)

    def chunk(c):
        return pl.ds(c, rows, stride=ROW_CHUNKS)

    x = jnp.concatenate([xbuf[cur, chunk(c), :] for c in range(ROW_CHUNKS)], axis=1)
    row = lax.broadcasted_iota(I32, x.shape, 0)
    xb = jnp.where(row < nv_ref[i], x, 0.0).astype(BF16)
    glu = jnp.dot(xb, w1g_sc[...], preferred_element_type=F32) + b1g_ref[0]
    lin = jnp.dot(xb, w1l_sc[...], preferred_element_type=F32) + b1l_ref[0]
    glu = jnp.minimum(glu, SWIGLU_LIMIT)
    lin = jnp.clip(lin, -SWIGLU_LIMIT, SWIGLU_LIMIT)
    act = glu * _sigmoid(SWIGLU_ALPHA * glu) * (lin + 1.0)
    y = jnp.dot(act.astype(BF16), w2_sc[...], preferred_element_type=F32) + b2_ref[0]
    for c in range(ROW_CHUNKS):
        ybuf[cur, chunk(c), :] = y[:, c * LANES:(c + 1) * LANES]

    @pl.when(i == n_blocks - 1)
    def _():
        def last(r, c):
            scatter(gdst0_ref[0, 0, r], cur, r).start()
            return c

        lax.fori_loop(0, rows, last, 0)
        wait_gather(nxt)
        wait_scatter(nxt)
        wait_scatter(cur)


def _experts(h, gsrc, gdst, block_expert, n_valid, w1, b1g, b1l, w2, b2):
    t = h.shape[0] // ROW_CHUNKS
    n_blocks = gsrc.shape[0]
    d, f2 = w1.shape[1:]
    f = f2 // 2
    sel = np.arange(DEINT)[:, None] - 2 * np.arange(DEINT // 2)[None, :]
    p_even = jnp.asarray((sel == 0).astype(np.float32), dtype=BF16)
    p_odd = jnp.asarray((sel == 1).astype(np.float32), dtype=BF16)
    last = n_blocks - 1

    def wsel(i, be, nv):
        return (be[i], 0, 0)

    const = lambda i, be, nv: (0, 0)
    idx_blk = (1, 1, MOE_BLOCK)
    kern = functools.partial(_expert_kernel, n_blocks=n_blocks, dump_row=TOP_K * t * ROW_CHUNKS)
    return pl.pallas_call(
        kern,
        out_shape=jax.ShapeDtypeStruct(((TOP_K * t + DUMP_ROWS) * ROW_CHUNKS, LANES), F32),
        grid_spec=pltpu.PrefetchScalarGridSpec(
            num_scalar_prefetch=2,
            grid=(n_blocks,),
            in_specs=[
                pl.BlockSpec(idx_blk, lambda i, be, nv: (i, 0, 0), memory_space=pltpu.SMEM),
                pl.BlockSpec(idx_blk, lambda i, be, nv: (jnp.minimum(i + 1, last), 0, 0), memory_space=pltpu.SMEM),
                pl.BlockSpec(idx_blk, lambda i, be, nv: (jnp.maximum(i - 1, 0), 0, 0), memory_space=pltpu.SMEM),
                pl.BlockSpec(idx_blk, lambda i, be, nv: (i, 0, 0), memory_space=pltpu.SMEM),
                pl.BlockSpec(memory_space=pl.ANY),
                pl.BlockSpec((1, d, f2), wsel),
                pl.BlockSpec((1, 1, f), wsel),
                pl.BlockSpec((1, 1, f), wsel),
                pl.BlockSpec((1, f, d), wsel),
                pl.BlockSpec((1, 1, d), wsel),
                pl.BlockSpec((DEINT, DEINT // 2), const),
                pl.BlockSpec((DEINT, DEINT // 2), const),
            ],
            out_specs=pl.BlockSpec(memory_space=pl.ANY),
            scratch_shapes=[pltpu.VMEM((2, MOE_BLOCK * ROW_CHUNKS, LANES), F32),
                            pltpu.VMEM((2, MOE_BLOCK * ROW_CHUNKS, LANES), F32),
                            pltpu.VMEM((d, f), BF16), pltpu.VMEM((d, f), BF16), pltpu.VMEM((f, d), BF16),
                            pltpu.SemaphoreType.DMA((2,)), pltpu.SemaphoreType.DMA((2,))],
        ),
        compiler_params=_cparams(("arbitrary",)),
        name="moe_experts",
    )(block_expert, n_valid, gsrc, gsrc, gdst, gdst, h, w1, b1g, b1l, w2, b2, p_even, p_odd)


def _combine_kernel(y0_ref, y1_ref, y2_ref, y3_ref, wt_ref, x_ref, mod_ref, o_ref):
    wt = wt_ref[0]
    g2 = mod_ref[0][5:6]
    y_refs = (y0_ref, y1_ref, y2_ref, y3_ref)
    wk = [wt[:, k:k + 1] for k in range(TOP_K)]
    tm = wt.shape[0]
    for c in range(ROW_CHUNKS):
        chunk = pl.ds(c, tm, stride=ROW_CHUNKS)
        y = wk[0] * y_refs[0][chunk, :]
        for k in range(1, TOP_K):
            y = y + wk[k] * y_refs[k][chunk, :]
        cs = slice(c * LANES, (c + 1) * LANES)
        o_ref[0, :, cs] = x_ref[0, :, cs] + g2[:, cs] * y


def _combine(yt, wts, x, mod, tm):
    b, s, d = x.shape
    nt = s // tm
    tiles = b * nt
    tok = lambda i, j: (i, j, 0)

    def yspec(k):
        return pl.BlockSpec((tm * ROW_CHUNKS, LANES), lambda i, j: (k * tiles + i * nt + j, 0))

    return pl.pallas_call(
        _combine_kernel,
        out_shape=jax.ShapeDtypeStruct((b, s, d), F32),
        grid=(b, nt),
        in_specs=[yspec(k) for k in range(TOP_K)] + [
            pl.BlockSpec((1, tm, LANES), tok),
            pl.BlockSpec((1, tm, d), tok),
            pl.BlockSpec((1, 6, d), lambda i, j: (i, 0, 0)),
        ],
        out_specs=pl.BlockSpec((1, tm, d), tok),
        compiler_params=_cparams(("arbitrary", "arbitrary")),
        name="moe_combine",
    )(yt, yt, yt, yt, wts, x, mod)


def _route_slots(top_idx, n_blocks):
    t = top_idx.shape[0]
    n_assign = t * TOP_K
    n_slots = n_blocks * MOE_BLOCK
    sorted_e, order = lax.sort_key_val(top_idx.reshape(-1), jnp.arange(n_assign, dtype=I32))
    experts = jnp.arange(N_EXPERTS + 1, dtype=I32)
    start = jnp.sum((sorted_e[None, :] < experts[:, None]).astype(I32), axis=1)
    counts = start[1:] - start[:-1]
    padded = (counts + MOE_BLOCK - 1) // MOE_BLOCK * MOE_BLOCK
    pad_end = jnp.cumsum(padded)
    pad_start = pad_end - padded
    blk_start = jnp.arange(n_blocks, dtype=I32) * MOE_BLOCK
    block_expert = jnp.minimum(jnp.sum((pad_end[None, :] <= blk_start[:, None]).astype(I32), axis=1), N_EXPERTS - 1)
    n_valid = jnp.clip(pad_start[block_expert] + counts[block_expert] - blk_start, 0, MOE_BLOCK).astype(I32)
    within = jnp.arange(MOE_BLOCK, dtype=I32)[None, :]
    valid = within < n_valid[:, None]
    rank = (blk_start - pad_start[block_expert] + start[block_expert])[:, None] + within
    slot_a = order[jnp.clip(rank, 0, n_assign - 1)]
    slot = blk_start[:, None] + within
    gsrc = jnp.where(valid, slot_a // TOP_K, 0) * ROW_CHUNKS
    gdst = jnp.where(valid, (slot_a % TOP_K) * t + slot_a // TOP_K, n_assign + slot % DUMP_ROWS) * ROW_CHUNKS
    shape = (n_blocks, 1, MOE_BLOCK)
    return gsrc.reshape(shape), gdst.reshape(shape), block_expert, n_valid


def _moe(x_mid, h, idx, wts, mod, w1, b1g, b1l, w2, b2, tm):
    b, s, d = x_mid.shape
    t = b * s
    n_blocks = -(-t * TOP_K // MOE_BLOCK) + N_EXPERTS
    top_idx = idx.reshape(t, LANES)[:, :TOP_K]
    gsrc, gdst, block_expert, n_valid = _route_slots(top_idx, n_blocks)
    yt = _experts(h.reshape(t * ROW_CHUNKS, LANES), gsrc, gdst, block_expert, n_valid, w1, b1g, b1l, w2, b2)
    return _combine(yt, wts, x_mid, mod, tm)


def _pad_cols(w, n):
    return jnp.pad(w, ((0, 0), (0, n - w.shape[-1])))


def _attn_colgain(diff_qk_gain, nsa_q_gain, nsa_k_gain):
    scale = HEAD_DIM ** -0.5 * LOG2E
    ones = jnp.ones((LANES,), F32)
    parts = [jnp.tile(diff_qk_gain[0] * scale, 8), jnp.tile(diff_qk_gain[1], 8), jnp.ones((512,), F32),
             jnp.tile(nsa_q_gain * scale, 8), ones, ones, jnp.tile(nsa_k_gain[1], 2), ones,
             jnp.tile(nsa_k_gain[2], 2), ones, ones]
    return jnp.concatenate(parts).reshape(1, W_IN_PAD)


def _overlap_matrix(s):
    n_sel = s // SEL_BLOCK
    ncp = s // CMP_STRIDE
    c_start = np.arange(ncp) * CMP_STRIDE
    s_start = np.arange(n_sel) * SEL_BLOCK
    ovl = (c_start[None, :] < s_start[:, None] + SEL_BLOCK) & (c_start[None, :] + CMP_BLOCK > s_start[:, None])
    ovl[:, ncp - 1] = False
    return jnp.asarray(ovl.astype(np.float32), dtype=BF16)


def _attention_layer(x, mod, gain, w_in, w_out, diff_qk_gain, diff_lambda, diff_subln, nsa_q_gain, nsa_k_gain,
                     cmp_pos, cmp_w1, cmp_w2, lam_init):
    b, s, d = x.shape
    tq_d = 256
    n_gate = NSA_Q_HEADS * 3
    gate_perm = np.arange(n_gate).reshape(NSA_KV_HEADS, NSA_GROUP, 3).transpose(0, 2, 1).reshape(-1)
    w_pad = jnp.concatenate([w_in[:, :COL_GATE], w_in[:, COL_GATE + gate_perm],
                             jnp.zeros((d, W_IN_PAD - COL_GATE - n_gate), w_in.dtype)], axis=1).astype(BF16)
    zb, dqT, dvT, nqT, vselT, vwinT, gT = _attn_in_proj(
        x, mod, gain, w_pad, _attn_colgain(diff_qk_gain, nsa_q_gain, nsa_k_gain))

    o_diff = _diff_attention(zb, dqT.reshape(b, DIFF_HEADS, 2, HEAD_DIM, s), dvT,
                             _score_bound(diff_qk_gain[0], diff_qk_gain[1]), diff_lambda, diff_subln, lam_init,
                             tq_d, ATTN_TM)

    ncp = s // CMP_STRIDE
    chunks = zb[:, :, KZ_KCMP:KZ_KCMP + 2 * LANES].reshape(b, ncp, CMP_STRIDE, 2, NSA_KV_HEADS, HEAD_DIM)
    chunks = chunks.transpose(0, 3, 4, 1, 2, 5).reshape(b, 4, ncp, CMP_STRIDE * HEAD_DIM)
    cmp_out = _compress(chunks, cmp_pos.reshape(2, 1, CMP_BLOCK * HEAD_DIM), cmp_w1, cmp_w2, nsa_k_gain[0:1])
    kc = cmp_out[:, 0:2].transpose(0, 2, 1, 3).reshape(b, ncp, LANES).astype(BF16)
    vcT = cmp_out[:, 2:4].transpose(0, 1, 3, 2).astype(BF16)
    gatesT = gT[:, :n_gate].reshape(b, NSA_KV_HEADS, 3, NSA_GROUP, s)
    shifts = jnp.stack([_score_bound(nsa_q_gain, nsa_k_gain[1]), _score_bound(nsa_q_gain, nsa_k_gain[2])])
    o_nsa = _nsa_attention(zb, nqT.reshape(b, NSA_KV_HEADS, NSA_GROUP, HEAD_DIM, s), vselT, vwinT, kc, vcT,
                           _overlap_matrix(s), gatesT, shifts, NSA_TILE)
    return o_diff, o_nsa


def kernel(x, c, mod_w, mod_b, norm_mix, norm_ffn, attn_w_in, attn_w_out, diff_qk_gain, diff_lambda, diff_subln,
           nsa_q_gain, nsa_k_gain, nsa_cmp_pos, nsa_cmp_w1, nsa_cmp_w2, conv_pw1_w, conv_pw1_b, conv_dw_w,
           conv_dw_b, conv_ln_g, conv_ln_b, conv_pw2_w, conv_pw2_b, router_w, router_b, moe_w1, moe_b1, moe_w2,
           moe_b2):
    b, s, d = x.shape
    depth = mod_w.shape[0]
    tm = 512 if s % 512 == 0 else 256
    mods = _modulation(c, mod_w, mod_b).reshape(depth, b, 6, d)
    half = d // 2
    for i in range(depth):
        mod = mods[i]
        j = i // 2
        if i % 2 == 0:
            lam_init = 0.8 - 0.6 * math.exp(-0.3 * i)
            o_diff, o_nsa = _attention_layer(
                x, mod, norm_mix[i:i + 1], attn_w_in[j], attn_w_out[j], diff_qk_gain[j], diff_lambda[j],
                diff_subln[j], nsa_q_gain[j], nsa_k_gain[j], nsa_cmp_pos[j], nsa_cmp_w1[j], nsa_cmp_w2[j], lam_init)
            a1, a2, c1, c2 = o_diff, o_nsa, 0, 0
            w_o = attn_w_out[j].astype(BF16)
            bias = jnp.zeros((1, d), F32)
        else:
            u = _conv_in_proj(x, mod, norm_mix[i:i + 1], conv_pw1_w[j].astype(BF16), conv_pw1_b[j:j + 1], tm)
            v = _dwconv_ln_swish(u, conv_dw_w[j].reshape(CONV_WIDTH, d), conv_dw_b[j:j + 1], conv_ln_g[j:j + 1],
                                 conv_ln_b[j:j + 1], 256)
            a1, a2, c1, c2 = v, v, 0, 1
            w_o = conv_pw2_w[j].astype(BF16)
            bias = conv_pw2_b[j:j + 1]
        rw = _pad_cols(router_w[i], LANES)
        rw_hi = rw.astype(BF16)
        rw_lo = (rw - rw_hi.astype(F32)).astype(BF16)
        rb = jnp.concatenate([router_b[i], jnp.full((LANES - N_EXPERTS,), NEG, F32)]).reshape(1, LANES)
        x_mid, h, idx, wts = _out_proj_router(a1, a2, c1, c2, w_o[:half], w_o[half:], bias, x, mod,
                                              norm_ffn[i:i + 1], rw_hi, rw_lo, rb, tm)
        x = _moe(x_mid, h, idx, wts, mod, moe_w1[i], moe_b1[i][:, None, 0::2], moe_b1[i][:, None, 1::2], moe_w2[i],
                 moe_b2[i][:, None, :], 256)
    return x
```

```python
import functools
import math

import jax
import jax.numpy as jnp
import numpy as np
from jax import lax
from jax.experimental import pallas as pl
from jax.experimental.pallas import tpu as pltpu

F32 = jnp.float32
BF16 = jnp.bfloat16
I32 = jnp.int32

D_MODEL = 1024
HEAD_DIM = 64
DIFF_HEADS = 4
NSA_Q_HEADS = 8
NSA_KV_HEADS = 2
NSA_GROUP = NSA_Q_HEADS // NSA_KV_HEADS
CMP_BLOCK = 32
CMP_STRIDE = 16
CMP_HIDDEN = 256
SEL_BLOCK = 64
SEL_TOP = 16
SEL_FORCED_SCORE = 1.0e4
WINDOW = 512
CONV_WIDTH = 31
N_EXPERTS = 32
TOP_K = 4
SWIGLU_ALPHA = 1.702
SWIGLU_LIMIT = 7.0
MOE_BLOCK = 256
NORM_EPS = 1e-6

LANES = 128
NEG = -1e30
TAKEN = -3e38
VMEM_LIMIT = 56 * 1024 * 1024

COL_DQ, COL_DK, COL_DV, COL_NQ = 0, 512, 1024, 1536
COL_KCMP, COL_VCMP, COL_KSEL, COL_VSEL, COL_KWIN, COL_VWIN = 2048, 2176, 2304, 2432, 2560, 2688
COL_GATE = 2816
W_IN_PAD = 2944
NORM_SLABS = frozenset(list(range(0, 8)) + list(range(12, 16)) + [COL_KSEL // LANES, COL_KWIN // LANES])


def _cparams(sem):
    return pltpu.CompilerParams(dimension_semantics=sem, vmem_limit_bytes=VMEM_LIMIT)


def _sigmoid(v):
    return 1.0 / (1.0 + jnp.exp(-v))


def _norm_mod(x, g, sc, sh):
    ms = jnp.mean(x * x, axis=-1, keepdims=True)
    return (x * lax.rsqrt(ms + NORM_EPS)) * g * (1.0 + sc) + sh


def _mod_kernel(c_ref, w_ref, b_ref, o_ref):
    c = c_ref[...]
    cond = c * _sigmoid(c)
    o_ref[0] = jnp.dot(cond, w_ref[0], preferred_element_type=F32, precision=lax.Precision.HIGHEST) + b_ref[0]


def _modulation(c, mod_w, mod_b):
    depth, d, n = mod_w.shape
    b = c.shape[0]
    tn = 1536
    return pl.pallas_call(
        _mod_kernel,
        out_shape=jax.ShapeDtypeStruct((depth, b, n), F32),
        grid=(depth, n // tn),
        in_specs=[
            pl.BlockSpec((b, d), lambda i, j: (0, 0)),
            pl.BlockSpec((1, d, tn), lambda i, j: (i, 0, j)),
            pl.BlockSpec((1, 1, tn), lambda i, j: (i, 0, j)),
        ],
        out_specs=pl.BlockSpec((1, b, tn), lambda i, j: (i, 0, j)),
        compiler_params=_cparams(("parallel", "parallel")),
        name="modulation",
    )(c, mod_w, mod_b.reshape(depth, 1, n))


KZ_DK, KZ_KCMP, KZ_VCMP, KZ_KSEL, KZ_KWIN, KZ_WIDTH = 0, 512, 640, 768, 896, 1024
KEY_COLS = {COL_DK + 128 * u: KZ_DK + 128 * u for u in range(4)}
KEY_COLS.update({COL_KCMP: KZ_KCMP, COL_VCMP: KZ_VCMP, COL_KSEL: KZ_KSEL, COL_KWIN: KZ_KWIN})
ATTN_TM = 512
DIFF_TQ = 512
NSA_TILE = 256
GATE_ROWS = 32


def _in_attn_kernel(x_ref, mod_ref, g_ref, w_ref, cg_ref, z_ref, dqT_ref, dvT_ref, nqT_ref, vsT_ref, vwT_ref, gT_ref):
    m = mod_ref[0]
    h = _norm_mod(x_ref[0], g_ref[...], m[1:2], m[0:1]).astype(BF16)
    lo = lax.broadcasted_iota(I32, (1, LANES), 1) < HEAD_DIM
    groups = ((0, 512), (512, 1024), (1024, 1536), (1536, 2048), (2048, 2816), (2816, 2944))
    halves = ATTN_TM // NSA_TILE
    for c0, c1 in groups:
        z = jnp.dot(h, w_ref[:, c0:c1], preferred_element_type=F32)
        for s in range((c1 - c0) // LANES):
            a0 = c0 + s * LANES
            zs = z[:, s * LANES:(s + 1) * LANES]
            if a0 == COL_GATE:
                gT_ref[0] = _sigmoid(zs).T[:GATE_ROWS]
                continue
            if a0 // LANES in NORM_SLABS:
                z2 = zs * zs
                s_lo = jnp.sum(jnp.where(lo, z2, 0.0), axis=-1, keepdims=True)
                s_hi = jnp.sum(jnp.where(lo, 0.0, z2), axis=-1, keepdims=True)
                inv = jnp.where(lo, lax.rsqrt(s_lo * (1.0 / HEAD_DIM) + NORM_EPS),
                                lax.rsqrt(s_hi * (1.0 / HEAD_DIM) + NORM_EPS))
                zs = zs * inv
            zs = zs * cg_ref[:, a0:a0 + LANES]
            if COL_DQ <= a0 < COL_DK:
                dqT_ref[0, a0 - COL_DQ:a0 - COL_DQ + LANES, :] = zs.T.astype(BF16)
            elif COL_DV <= a0 < COL_NQ:
                dvT_ref[0, 0, a0 - COL_DV:a0 - COL_DV + LANES, :] = zs.T.astype(BF16)
            elif COL_NQ <= a0 < COL_KCMP:
                nqT_ref[0, a0 - COL_NQ:a0 - COL_NQ + LANES, :] = zs.T.astype(BF16)
            elif a0 in (COL_VSEL, COL_VWIN):
                zt = zs.T.astype(BF16)
                ref = vsT_ref if a0 == COL_VSEL else vwT_ref
                for u in range(halves):
                    ref[0, u] = zt[:, u * NSA_TILE:(u + 1) * NSA_TILE]
            else:
                k0 = KEY_COLS[a0]
                z_ref[0, :, k0:k0 + LANES] = zs.astype(BF16)


def _attn_in_proj(x, mod, gain, w_pad, colgain):
    b, s, d = x.shape
    tm = ATTN_TM
    halves = tm // NSA_TILE
    tok = lambda i, j: (i, j, 0)
    featT = lambda i, j: (i, 0, j)
    return pl.pallas_call(
        _in_attn_kernel,
        out_shape=(jax.ShapeDtypeStruct((b, s, KZ_WIDTH), BF16),
                   jax.ShapeDtypeStruct((b, 512, s), BF16),
                   jax.ShapeDtypeStruct((b, s // tm, 512, tm), BF16),
                   jax.ShapeDtypeStruct((b, 512, s), BF16),
                   jax.ShapeDtypeStruct((b, s // NSA_TILE, LANES, NSA_TILE), BF16),
                   jax.ShapeDtypeStruct((b, s // NSA_TILE, LANES, NSA_TILE), BF16),
                   jax.ShapeDtypeStruct((b, GATE_ROWS, s), F32)),
        grid=(b, s // tm),
        in_specs=[
            pl.BlockSpec((1, tm, d), tok),
            pl.BlockSpec((1, 6, d), lambda i, j: (i, 0, 0)),
            pl.BlockSpec((1, d), lambda i, j: (0, 0)),
            pl.BlockSpec((d, W_IN_PAD), lambda i, j: (0, 0)),
            pl.BlockSpec((1, W_IN_PAD), lambda i, j: (0, 0)),
        ],
        out_specs=(pl.BlockSpec((1, tm, KZ_WIDTH), tok),
                   pl.BlockSpec((1, 512, tm), featT),
                   pl.BlockSpec((1, 1, 512, tm), lambda i, j: (i, j, 0, 0)),
                   pl.BlockSpec((1, 512, tm), featT),
                   pl.BlockSpec((1, halves, LANES, NSA_TILE), lambda i, j: (i, j, 0, 0)),
                   pl.BlockSpec((1, halves, LANES, NSA_TILE), lambda i, j: (i, j, 0, 0)),
                   pl.BlockSpec((1, GATE_ROWS, tm), featT)),
        compiler_params=_cparams(("parallel", "parallel")),
        name="attn_in_proj",
    )(x, mod, gain, w_pad, colgain)


FIXED_SHIFT_LIMIT = 57.0
FIXED_TILES_PER_STEP = 4
LOG2E = math.log2(math.e)


def _score_bound(gain_q, gain_k):
    return (math.sqrt(HEAD_DIM) * LOG2E) * jnp.max(jnp.abs(gain_q * gain_k))


SUM_ROWS = 8


def _with_ones(vt):
    return jnp.concatenate([vt, jnp.ones((SUM_ROWS, vt.shape[1]), vt.dtype)], axis=0)


def _fixed_update(ss, vts, shift, acc_ref):
    ps = [jnp.exp2(s - shift).astype(BF16) for s in ss]
    acc_ref[...] += sum(jnp.dot(_with_ones(vt), p, preferred_element_type=F32) for vt, p in zip(vts, ps))


def _online_update(ss, vts, m_ref, acc_ref):
    for s, vt in zip(ss, vts):
        m_old = m_ref[...]
        m_new = jnp.maximum(m_old, jnp.max(s, axis=0, keepdims=True))
        alpha = jnp.exp2(m_old - m_new)
        p = jnp.exp2(s - m_new).astype(BF16)
        acc_ref[...] = alpha * acc_ref[...] + jnp.dot(_with_ones(vt), p, preferred_element_type=F32)
        m_ref[...] = m_new


def _tile_loops(n_full, tiles, width):
    def group(jj, c):
        tiles(tuple(width * jj + u for u in range(width)), (False,) * width)
        return c

    lax.fori_loop(0, n_full // width, group, 0)
    rem = n_full % width
    base = n_full - rem
    for r in range(width):
        @pl.when(rem == r)
        def _():
            tiles(tuple(base + u for u in range(r)) + (n_full,), (False,) * r + (True,))


def _diff_kernel(mb_ref, lam_ref, qT_ref, k_ref, vT_ref, sub_ref, o_ref, m0, m1, a0, a1, *, tq, tk, lam_init):
    i = pl.program_id(2)
    m_refs, acc_refs = (m0, m1), (a0, a1)
    q = qT_ref[0, 0]
    zero = jnp.zeros((HEAD_DIM, tq), BF16)
    qp = (jnp.concatenate([q[0], zero], axis=0), jnp.concatenate([zero, q[1]], axis=0))
    for mm in range(2):
        m_refs[mm][...] = jnp.full(m_refs[mm].shape, NEG, F32)
        acc_refs[mm][...] = jnp.zeros(acc_refs[mm].shape, F32)
    shift = mb_ref[0]
    dv = 2 * HEAD_DIM

    def tiles(js, masks, fixed):
        kts = [k_ref[0, pl.ds(pl.multiple_of(j * tk, tk), tk), :] for j in js]
        vts = [vT_ref[0, j] for j in js]
        for mm in range(2):
            ss = []
            for j, kt, masked in zip(js, kts, masks):
                s = jnp.dot(kt, qp[mm], preferred_element_type=F32)
                if masked:
                    kpos = j * tk + lax.broadcasted_iota(I32, (tk, tq), 0)
                    t = i * tq + lax.broadcasted_iota(I32, (tk, tq), 1)
                    s = jnp.where(kpos <= t, s, NEG)
                ss.append(s)
            if fixed:
                _fixed_update(ss, vts, shift, acc_refs[mm])
            else:
                _online_update(ss, vts, m_refs[mm], acc_refs[mm])

    n_full = (i * tq) // tk

    @pl.when(shift <= FIXED_SHIFT_LIMIT)
    def _():
        _tile_loops(n_full, functools.partial(tiles, fixed=True), FIXED_TILES_PER_STEP)

    @pl.when(shift > FIXED_SHIFT_LIMIT)
    def _():
        _tile_loops(n_full, functools.partial(tiles, fixed=False), 1)

    lv = lam_ref[...]
    lam = (jnp.exp(jnp.sum(lv[0:1] * lv[1:2], axis=-1, keepdims=True))
           - jnp.exp(jnp.sum(lv[2:3] * lv[3:4], axis=-1, keepdims=True)) + lam_init)
    o = a0[:dv] / a0[dv:dv + 1] - lam * (a1[:dv] / a1[dv:dv + 1])
    ms = jnp.mean(o * o, axis=0, keepdims=True)
    o = o * lax.rsqrt(ms + NORM_EPS) * sub_ref[...] * (1.0 - lam_init)
    o_ref[0] = o.T.astype(BF16)


def _diff_attention(zb, qT, vT, shift, lam_vecs, subln, lam_init, tq, tk):
    b, s, _ = zb.shape
    kern = functools.partial(_diff_kernel, tq=tq, tk=tk, lam_init=lam_init)
    dv = 2 * HEAD_DIM
    return pl.pallas_call(
        kern,
        out_shape=jax.ShapeDtypeStruct((b, s, DIFF_HEADS * dv), BF16),
        grid=(b, DIFF_HEADS, s // tq),
        in_specs=[
            pl.BlockSpec(memory_space=pltpu.SMEM),
            pl.BlockSpec((4, HEAD_DIM), lambda bi, h, i: (0, 0)),
            pl.BlockSpec((1, 1, 2, HEAD_DIM, tq), lambda bi, h, i: (bi, h, 0, 0, i)),
            pl.BlockSpec((1, s, LANES), lambda bi, h, i: (bi, 0, KZ_DK // LANES + h)),
            pl.BlockSpec((1, s // tk, 2 * HEAD_DIM, tk), lambda bi, h, i: (bi, 0, h, 0)),
            pl.BlockSpec((2 * HEAD_DIM, 1), lambda bi, h, i: (0, 0)),
        ],
        out_specs=pl.BlockSpec((1, tq, LANES), lambda bi, h, i: (bi, i, h)),
        scratch_shapes=[pltpu.VMEM((1, tq), F32)] * 2 + [pltpu.VMEM((dv + SUM_ROWS, tq), F32)] * 2,
        compiler_params=_cparams(("parallel", "parallel", "arbitrary")),
        name="diff_attention",
    )(shift.reshape(1), lam_vecs, qT, zb, vT, subln.reshape(-1, 1))


def _gelu_tanh(v):
    return 0.5 * v * (1.0 + jnp.tanh(math.sqrt(2.0 / math.pi) * (v + 0.044715 * (v * v * v))))


def _cmp_kernel(c_ref, pos_ref, w1_ref, w2_ref, kg_ref, o_ref, *, n_cmp):
    jh = pl.program_id(1)
    half = CMP_STRIDE * HEAD_DIM
    c = c_ref[0, 0]
    w1a = w1_ref[0, :half, :].astype(BF16)
    w1b = w1_ref[0, half:, :].astype(BF16)
    pos = jnp.broadcast_to(pos_ref[0], (8, 2 * half)).astype(BF16)
    bias = (jnp.dot(pos[:, :half], w1a, preferred_element_type=F32)
            + jnp.dot(pos[:, half:], w1b, preferred_element_type=F32))[0:1]
    u = jnp.dot(c, w1a, preferred_element_type=F32)
    v = jnp.dot(c, w1b, preferred_element_type=F32)
    ncp = u.shape[0]
    hid = _gelu_tanh(u + pltpu.roll(v, ncp - 1, 0) + bias)
    y = jnp.dot(hid.astype(BF16), w2_ref[0].astype(BF16), preferred_element_type=F32)
    yn = y * lax.rsqrt(jnp.mean(y * y, axis=-1, keepdims=True) + NORM_EPS) * kg_ref[...]
    y = jnp.where(jh < NSA_KV_HEADS, yn, y)
    row = lax.broadcasted_iota(I32, y.shape, 0)
    o_ref[0, 0] = jnp.where(row < n_cmp, y, 0.0)


def _compress(chunks, pos, w1, w2, kgain):
    b, _, ncp, cd = chunks.shape
    kern = functools.partial(_cmp_kernel, n_cmp=ncp - 1)
    return pl.pallas_call(
        kern,
        out_shape=jax.ShapeDtypeStruct((b, 4, ncp, HEAD_DIM), F32),
        grid=(b, 4),
        in_specs=[
            pl.BlockSpec((1, 1, ncp, cd), lambda bi, j: (bi, j, 0, 0)),
            pl.BlockSpec((1, 1, 2 * cd), lambda bi, j: (j // NSA_KV_HEADS, 0, 0)),
            pl.BlockSpec((1, 2 * cd, CMP_HIDDEN), lambda bi, j: (j // NSA_KV_HEADS, 0, 0)),
            pl.BlockSpec((1, CMP_HIDDEN, HEAD_DIM), lambda bi, j: (j // NSA_KV_HEADS, 0, 0)),
            pl.BlockSpec((1, HEAD_DIM), lambda bi, j: (0, 0)),
        ],
        out_specs=pl.BlockSpec((1, 1, ncp, HEAD_DIM), lambda bi, j: (bi, j, 0, 0)),
        compiler_params=_cparams(("parallel", "parallel")),
        name="nsa_compress",
    )(chunks, pos, w1, w2, kgain)


def _nsa_kernel(mb_ref, qT_ref, ksel_ref, vselT_ref, kwin_ref, vwinT_ref, kc_ref, vcT_ref, ovl_ref, gate_ref, o_ref,
                selb_sc, m_s, m_w, a_s, a_w, *, tq, n_top):
    hk = pl.program_id(1)
    i = pl.program_id(2)
    g4 = NSA_GROUP
    nq = g4 * tq
    q4 = jnp.concatenate([qT_ref[0, 0, g] for g in range(g4)], axis=1)
    half = lax.broadcasted_iota(I32, (2 * HEAD_DIM, nq), 0) // HEAD_DIM
    qp = jnp.where(half == hk, jnp.concatenate([q4, q4], axis=0), jnp.zeros((), BF16))
    t1 = i * tq + lax.broadcasted_iota(I32, (1, tq), 1)

    def tile4(a):
        return jnp.concatenate([a] * g4, axis=1)

    ncp = kc_ref.shape[1]
    sc = jnp.dot(kc_ref[0], qp, preferred_element_type=F32)
    cend = CMP_STRIDE * lax.broadcasted_iota(I32, (ncp, tq), 0) + (CMP_BLOCK - 1)
    sc = sc + tile4(jnp.where(cend <= t1, 0.0, NEG))
    e = jnp.where(sc > 0.5 * NEG, jnp.exp2(sc - jnp.max(sc, axis=0, keepdims=True)), 0.0)
    p = e / jnp.maximum(jnp.sum(e, axis=0, keepdims=True), 1e-30)
    o_c = jnp.dot(vcT_ref[0, 0], p.astype(BF16), preferred_element_type=F32)

    psum = p[:, 0:tq]
    for g in range(1, g4):
        psum = psum + p[:, g * tq:(g + 1) * tq]
    p_hi = psum.astype(BF16)
    p_lo = (psum - p_hi.astype(F32)).astype(BF16)
    imp = (jnp.dot(ovl_ref[...], p_hi, preferred_element_type=F32)
           + jnp.dot(ovl_ref[...], p_lo, preferred_element_type=F32))
    n_sel = imp.shape[0]
    jrow = lax.broadcasted_iota(I32, (n_sel, tq), 0)
    cur = t1 // SEL_BLOCK
    forced = (jrow == 0) | (jrow == cur) | (jrow == cur - 1)
    score = jnp.where(forced, SEL_FORCED_SCORE, jnp.where(jrow <= cur, imp, NEG))
    selb = jnp.full((n_sel, tq), NEG, F32)
    jrow_f = jrow.astype(F32)
    for _ in range(n_top):
        best = jnp.max(score, axis=0, keepdims=True)
        pick = jnp.min(jnp.where(score == best, jrow_f, float(n_sel)), axis=0, keepdims=True)
        hit = jrow_f == pick
        selb = jnp.where(hit, 0.0, selb)
        score = jnp.where(hit, TAKEN, score)
    selb_sc[...] = selb

    for ref in (m_s, m_w):
        ref[...] = jnp.full(ref.shape, NEG, F32)
    for ref in (a_s, a_w):
        ref[...] = jnp.zeros(ref.shape, F32)
    tk = tq
    bpt = tk // SEL_BLOCK
    krow = lax.broadcasted_iota(I32, (tk, tq), 0)
    shift_s, shift_w = mb_ref[0], mb_ref[1]
    fixed = jnp.maximum(shift_s, shift_w) <= FIXED_SHIFT_LIMIT

    def sel_tiles(js, masks, fixed):
        ss = []
        for j, masked in zip(js, masks):
            kt = ksel_ref[0, pl.ds(pl.multiple_of(j * tk, tk), tk), :]
            rows = [jnp.broadcast_to(selb_sc[pl.ds(j * bpt + r, 1), :], (SEL_BLOCK, tq)) for r in range(bpt)]
            bias = jnp.concatenate(rows, axis=0)
            if masked:
                bias = jnp.where(j * tk + krow <= t1, bias, NEG)
            ss.append(jnp.dot(kt, qp, preferred_element_type=F32) + tile4(bias))
        vts = [vselT_ref[0, j] for j in js]
        if fixed:
            _fixed_update(ss, vts, shift_s, a_s)
        else:
            _online_update(ss, vts, m_s, a_s)

    def win_tiles(js, fixed):
        ss = []
        for j in js:
            kt = kwin_ref[0, pl.ds(pl.multiple_of(j * tk, tk), tk), :]
            kpos = j * tk + krow
            bias = jnp.where(kpos <= t1, jnp.where(kpos > t1 - WINDOW, 0.0, NEG), NEG)
            ss.append(jnp.dot(kt, qp, preferred_element_type=F32) + tile4(bias))
        vts = [vwinT_ref[0, j] for j in js]
        if fixed:
            _fixed_update(ss, vts, shift_w, a_w)
        else:
            _online_update(ss, vts, m_w, a_w)

    max_win = (WINDOW + tk - 1) // tk + 1
    n_win = jnp.minimum(i + 1, max_win)
    for use_fixed in (True, False):
        @pl.when(fixed == use_fixed)
        def _():
            _tile_loops(i, functools.partial(sel_tiles, fixed=use_fixed), FIXED_TILES_PER_STEP if use_fixed else 1)
            for c in range(1, max_win + 1):
                @pl.when(n_win == c)
                def _():
                    win_tiles(tuple(i - (c - 1) + u for u in range(c)), use_fixed)

    def gate_row(br):
        gt = gate_ref[0, 0, br]
        return jnp.concatenate([gt[g:g + 1] for g in range(g4)], axis=1)

    dv = HEAD_DIM
    out = (o_c * gate_row(0) + (a_s[:dv] / a_s[dv:dv + 1]) * gate_row(1) + (a_w[:dv] / a_w[dv:dv + 1]) * gate_row(2))
    stacked = jnp.concatenate([out[:, g * tq:(g + 1) * tq] for g in range(g4)], axis=0)
    o_ref[0] = stacked.T.astype(BF16)


def _nsa_attention(zb, qT, vselT, vwinT, kc, vcT, ovl, gatesT, shifts, tq):
    b, s, _ = zb.shape
    nt = s // tq
    n_sel = s // SEL_BLOCK
    ncp = kc.shape[1]
    kern = functools.partial(_nsa_kernel, tq=tq, n_top=min(SEL_TOP, n_sel))
    gd = NSA_GROUP * HEAD_DIM
    nq = NSA_GROUP * tq
    return pl.pallas_call(
        kern,
        out_shape=jax.ShapeDtypeStruct((b, s, NSA_Q_HEADS * HEAD_DIM), BF16),
        grid=(b, NSA_KV_HEADS, nt),
        in_specs=[
            pl.BlockSpec(memory_space=pltpu.SMEM),
            pl.BlockSpec((1, 1, NSA_GROUP, HEAD_DIM, tq), lambda bi, h, i: (bi, h, 0, 0, i)),
            pl.BlockSpec((1, s, LANES), lambda bi, h, i: (bi, 0, KZ_KSEL // LANES)),
            pl.BlockSpec((1, nt, HEAD_DIM, tq), lambda bi, h, i: (bi, 0, h, 0)),
            pl.BlockSpec((1, s, LANES), lambda bi, h, i: (bi, 0, KZ_KWIN // LANES)),
            pl.BlockSpec((1, nt, HEAD_DIM, tq), lambda bi, h, i: (bi, 0, h, 0)),
            pl.BlockSpec((1, ncp, LANES), lambda bi, h, i: (bi, 0, 0)),
            pl.BlockSpec((1, 1, HEAD_DIM, ncp), lambda bi, h, i: (bi, h, 0, 0)),
            pl.BlockSpec((n_sel, ncp), lambda bi, h, i: (0, 0)),
            pl.BlockSpec((1, 1, 3, NSA_GROUP, tq), lambda bi, h, i: (bi, h, 0, 0, i)),
        ],
        out_specs=pl.BlockSpec((1, tq, gd), lambda bi, h, i: (bi, i, h)),
        scratch_shapes=([pltpu.VMEM((n_sel, tq), F32)] + [pltpu.VMEM((1, nq), F32)] * 2
                        + [pltpu.VMEM((HEAD_DIM + SUM_ROWS, nq), F32)] * 2),
        compiler_params=_cparams(("parallel", "parallel", "arbitrary")),
        name="nsa_attention",
    )(shifts, qT, zb, vselT, zb, vwinT, kc, vcT, ovl, gatesT)


def _in_conv_kernel(x_ref, mod_ref, g_ref, w_ref, b_ref, u_ref):
    m = mod_ref[0]
    d = x_ref.shape[-1]
    h = _norm_mod(x_ref[0], g_ref[...], m[1:2], m[0:1]).astype(BF16)
    a = jnp.dot(h, w_ref[:, :d], preferred_element_type=F32) + b_ref[:, :d]
    g = jnp.dot(h, w_ref[:, d:], preferred_element_type=F32) + b_ref[:, d:]
    u_ref[0] = a * _sigmoid(g)


def _conv_in_proj(x, mod, gain, w, bias, tm):
    b, s, d = x.shape
    return pl.pallas_call(
        _in_conv_kernel,
        out_shape=jax.ShapeDtypeStruct((b, s, d), F32),
        grid=(b, s // tm),
        in_specs=[
            pl.BlockSpec((1, tm, d), lambda i, j: (i, j, 0)),
            pl.BlockSpec((1, 6, d), lambda i, j: (i, 0, 0)),
            pl.BlockSpec((1, d), lambda i, j: (0, 0)),
            pl.BlockSpec((d, 2 * d), lambda i, j: (0, 0)),
            pl.BlockSpec((1, 2 * d), lambda i, j: (0, 0)),
        ],
        out_specs=pl.BlockSpec((1, tm, d), lambda i, j: (i, j, 0)),
        compiler_params=_cparams(("parallel", "parallel")),
        name="conv_in_proj",
    )(x, mod, gain, w, bias)


HALO = 32


def _dwconv_kernel(u_ref, halo_ref, w_ref, b_ref, lg_ref, lb_ref, o_ref, buf_sc, acc_sc, *, tm):
    i = pl.program_id(1)
    d = u_ref.shape[-1]
    buf_sc[0:HALO, :] = jnp.where(i > 0, halo_ref[0], 0.0)
    buf_sc[HALO:, :] = u_ref[0]
    off = HALO - (CONV_WIDTH - 1)
    cw = 256
    for c in range(d // cw):
        cs = slice(c * cw, (c + 1) * cw)
        acc = jnp.zeros((tm, cw), F32) + b_ref[:, cs]
        for j in range(CONV_WIDTH):
            acc = acc + w_ref[j:j + 1, cs] * buf_sc[off + j:off + j + tm, cs]
        acc_sc[:, cs] = acc
    y = acc_sc[...]
    mu = jnp.mean(y, axis=-1, keepdims=True)
    yc = y - mu
    var = jnp.mean(yc * yc, axis=-1, keepdims=True)
    yn = yc * lax.rsqrt(var + NORM_EPS) * lg_ref[...] + lb_ref[...]
    o_ref[0] = (yn * _sigmoid(yn)).astype(BF16)


def _dwconv_ln_swish(u, dw_w, dw_b, ln_g, ln_b, tm):
    b, s, d = u.shape
    kern = functools.partial(_dwconv_kernel, tm=tm)
    hb = tm // HALO
    return pl.pallas_call(
        kern,
        out_shape=jax.ShapeDtypeStruct((b, s, d), BF16),
        grid=(b, s // tm),
        in_specs=[
            pl.BlockSpec((1, tm, d), lambda bi, i: (bi, i, 0)),
            pl.BlockSpec((1, HALO, d), lambda bi, i: (bi, jnp.maximum(i * hb - 1, 0), 0)),
            pl.BlockSpec((CONV_WIDTH, d), lambda bi, i: (0, 0)),
            pl.BlockSpec((1, d), lambda bi, i: (0, 0)),
            pl.BlockSpec((1, d), lambda bi, i: (0, 0)),
            pl.BlockSpec((1, d), lambda bi, i: (0, 0)),
        ],
        out_specs=pl.BlockSpec((1, tm, d), lambda bi, i: (bi, i, 0)),
        scratch_shapes=[pltpu.VMEM((tm + HALO, d), F32), pltpu.VMEM((tm, d), F32)],
        compiler_params=_cparams(("parallel", "parallel")),
        name="dwconv_ln_swish",
    )(u, u, dw_w, dw_b, ln_g, ln_b)


def _out_router_kernel(a1_ref, a2_ref, w1_ref, w2_ref, b_ref, x_ref, mod_ref, g_ref, rwh_ref, rwl_ref, rb_ref,
                       xo_ref, h_ref, idx_ref, wt_ref):
    m = mod_ref[0]
    y = (jnp.dot(a1_ref[0], w1_ref[...], preferred_element_type=F32)
         + jnp.dot(a2_ref[0], w2_ref[...], preferred_element_type=F32) + b_ref[...])
    x = x_ref[0] + m[2:3] * y
    xo_ref[0] = x
    h = _norm_mod(x, g_ref[...], m[4:5], m[3:4])
    chunks = h.shape[1] // LANES
    for c in range(chunks):
        h_ref[0, pl.ds(c, h.shape[0], stride=chunks), :] = h[:, c * LANES:(c + 1) * LANES]
    h_hi = h.astype(BF16)
    h_lo = (h - h_hi.astype(F32)).astype(BF16)
    logits = (jnp.dot(h_hi, rwh_ref[...], preferred_element_type=F32)
              + jnp.dot(h_lo, rwh_ref[...], preferred_element_type=F32)
              + jnp.dot(h_hi, rwl_ref[...], preferred_element_type=F32)) + rb_ref[...]
    lane = lax.broadcasted_iota(I32, logits.shape, 1)
    lane_f = lane.astype(F32)
    idx_out = jnp.zeros(logits.shape, F32)
    val_out = jnp.full(logits.shape, NEG, F32)
    for k in range(TOP_K):
        best = jnp.max(logits, axis=-1, keepdims=True)
        pick = jnp.min(jnp.where(logits == best, lane_f, float(LANES)), axis=-1, keepdims=True)
        idx_out = jnp.where(lane == k, pick, idx_out)
        val_out = jnp.where(lane == k, best, val_out)
        logits = jnp.where(lane_f == pick, TAKEN, logits)
    e = jnp.where(lane < TOP_K, jnp.exp(val_out - jnp.max(val_out, axis=-1, keepdims=True)), 0.0)
    idx_ref[0] = idx_out.astype(I32)
    wt_ref[0] = e / jnp.sum(e, axis=-1, keepdims=True)


def _out_proj_router(a1, a2, c1, c2, w1, w2, bias, x, mod, gain, rw_hi, rw_lo, rb, tm):
    b, s, d = x.shape
    kw = w1.shape[0]
    tok = lambda i, j: (i, j, 0)
    const = lambda i, j: (0, 0)
    return pl.pallas_call(
        _out_router_kernel,
        out_shape=(jax.ShapeDtypeStruct((b, s, d), F32), jax.ShapeDtypeStruct((b, s * (d // LANES), LANES), F32),
                   jax.ShapeDtypeStruct((b, s, LANES), I32), jax.ShapeDtypeStruct((b, s, LANES), F32)),
        grid=(b, s // tm),
        in_specs=[
            pl.BlockSpec((1, tm, kw), lambda i, j: (i, j, c1)),
            pl.BlockSpec((1, tm, kw), lambda i, j: (i, j, c2)),
            pl.BlockSpec((kw, d), const),
            pl.BlockSpec((kw, d), const),
            pl.BlockSpec((1, d), const),
            pl.BlockSpec((1, tm, d), tok),
            pl.BlockSpec((1, 6, d), lambda i, j: (i, 0, 0)),
            pl.BlockSpec((1, d), const),
            pl.BlockSpec((d, LANES), const),
            pl.BlockSpec((d, LANES), const),
            pl.BlockSpec((1, LANES), const),
        ],
        out_specs=(pl.BlockSpec((1, tm, d), tok), pl.BlockSpec((1, tm * (d // LANES), LANES), tok),
                   pl.BlockSpec((1, tm, LANES), tok), pl.BlockSpec((1, tm, LANES), tok)),
        compiler_params=_cparams(("parallel", "parallel")),
        name="out_proj_router",
    )(a1, a2, w1, w2, bias, x, mod, gain, rw_hi, rw_lo, rb)


ROW_CHUNKS = D_MODEL // LANES
DEINT = 512
DUMP_ROWS = 512


def _expert_kernel(be_ref, nv_ref, gsrc0_ref, gsrc1_ref, gdstp_ref, gdst0_ref, h_ref, w1_ref, b1g_ref, b1l_ref,
                   w2_ref, b2_ref, pe_ref, po_ref, yt_ref, xbuf, ybuf, w1g_sc, w1l_sc, w2_sc, gsem, ssem,
                   *, n_blocks, dump_row):
    i = pl.program_id(0)
    cur = i % 2
    nxt = 1 - cur
    rows = MOE_BLOCK

    def tile_rows(first):
        return pl.ds(pl.multiple_of(first, ROW_CHUNKS), ROW_CHUNKS)

    def buf_rows(r):
        return pl.ds(r * ROW_CHUNKS, ROW_CHUNKS) if isinstance(r, int) else tile_rows(r * ROW_CHUNKS)

    def gather(idx_ref, slot, r):
        return pltpu.make_async_copy(h_ref.at[tile_rows(idx_ref[0, 0, r])], xbuf.at[slot, buf_rows(r)], gsem.at[slot])

    def scatter(dst_first, slot, r):
        return pltpu.make_async_copy(ybuf.at[slot, buf_rows(r)], yt_ref.at[tile_rows(dst_first)], ssem.at[slot])

    def wait_gather(slot):
        pltpu.make_async_copy(h_ref.at[pl.ds(0, rows * ROW_CHUNKS)], xbuf.at[slot], gsem.at[slot]).wait()

    def wait_scatter(slot):
        pltpu.make_async_copy(ybuf.at[slot], yt_ref.at[pl.ds(0, rows * ROW_CHUNKS)], ssem.at[slot]).wait()

    @pl.when(i == 0)
    def _():
        ybuf[...] = jnp.zeros(ybuf.shape, F32)

        def prime(r, c):
            gather(gsrc0_ref, 0, r).start()
            return c

        lax.fori_loop(0, rows, prime, 0)

    wait_gather(cur)

    @pl.when(i >= 1)
    def _():
        wait_scatter(cur)

    first = i == 0
    changed = jnp.logical_or(first, be_ref[i] != be_ref[jnp.maximum(i - 1, 0)])

    @pl.when(changed)
    def _():
        for c in range(w1_ref.shape[3] // DEINT):
            wc = w1_ref[0, 0, :, c * DEINT:(c + 1) * DEINT].astype(BF16)
            cs = slice(c * (DEINT // 2), (c + 1) * (DEINT // 2))
            w1g_sc[:, cs] = jnp.dot(wc, pe_ref[...], preferred_element_type=F32).astype(BF16)
            w1l_sc[:, cs] = jnp.dot(wc, po_ref[...], preferred_element_type=F32).astype(BF16)
        w2_sc[...] = w2_ref[0, 0].astype(BF16)

    for r in range(rows):
        gather(gsrc1_ref, nxt, r).start(priority=r % 2)
        scatter(jnp.where(first, dump_row + r * ROW_CHUNKS, gdstp_ref[0, 0, r]), nxt, r).start(priority=(r + 1) % 2)

    def chunk(c):
        return pl.ds(c, rows, stride=ROW_CHUNKS)

    x = jnp.concatenate([xbuf[cur, chunk(c), :] for c in range(ROW_CHUNKS)], axis=1)
    row = lax.broadcasted_iota(I32, x.shape, 0)
    xb = jnp.where(row < nv_ref[i], x, 0.0).astype(BF16)
    glu = jnp.dot(xb, w1g_sc[...], preferred_element_type=F32) + b1g_ref[0]
    lin = jnp.dot(xb, w1l_sc[...], preferred_element_type=F32) + b1l_ref[0]
    glu = jnp.minimum(glu, SWIGLU_LIMIT)
    lin = jnp.clip(lin, -SWIGLU_LIMIT, SWIGLU_LIMIT)
    act = glu * _sigmoid(SWIGLU_ALPHA * glu) * (lin + 1.0)
    y = jnp.dot(act.astype(BF16), w2_sc[...], preferred_element_type=F32) + b2_ref[0]
    for c in range(ROW_CHUNKS):
        ybuf[cur, chunk(c), :] = y[:, c * LANES:(c + 1) * LANES]

    @pl.when(i == n_blocks - 1)
    def _():
        def last(r, c):
            scatter(gdst0_ref[0, 0, r], cur, r).start()
            return c

        lax.fori_loop(0, rows, last, 0)
        wait_gather(nxt)
        wait_scatter(nxt)
        wait_scatter(cur)


def _experts(h, gsrc, gdst, block_expert, n_valid, layer, w1, b1g, b1l, w2, b2):
    t = h.shape[0] // ROW_CHUNKS
    n_blocks = gsrc.shape[0]
    d, f2 = w1.shape[2:]
    f = f2 // 2
    sel = np.arange(DEINT)[:, None] - 2 * np.arange(DEINT // 2)[None, :]
    p_even = jnp.asarray((sel == 0).astype(np.float32), dtype=BF16)
    p_odd = jnp.asarray((sel == 1).astype(np.float32), dtype=BF16)
    last = n_blocks - 1

    def wsel(i, be, nv):
        return (be[i], 0, 0)

    def wsel_layer(i, be, nv):
        return (layer, be[i], 0, 0)

    const = lambda i, be, nv: (0, 0)
    idx_blk = (1, 1, MOE_BLOCK)
    kern = functools.partial(_expert_kernel, n_blocks=n_blocks, dump_row=TOP_K * t * ROW_CHUNKS)
    return pl.pallas_call(
        kern,
        out_shape=jax.ShapeDtypeStruct(((TOP_K * t + DUMP_ROWS) * ROW_CHUNKS, LANES), F32),
        grid_spec=pltpu.PrefetchScalarGridSpec(
            num_scalar_prefetch=2,
            grid=(n_blocks,),
            in_specs=[
                pl.BlockSpec(idx_blk, lambda i, be, nv: (i, 0, 0), memory_space=pltpu.SMEM),
                pl.BlockSpec(idx_blk, lambda i, be, nv: (jnp.minimum(i + 1, last), 0, 0), memory_space=pltpu.SMEM),
                pl.BlockSpec(idx_blk, lambda i, be, nv: (jnp.maximum(i - 1, 0), 0, 0), memory_space=pltpu.SMEM),
                pl.BlockSpec(idx_blk, lambda i, be, nv: (i, 0, 0), memory_space=pltpu.SMEM),
                pl.BlockSpec(memory_space=pl.ANY),
                pl.BlockSpec((1, 1, d, f2), wsel_layer),
                pl.BlockSpec((1, 1, f), wsel),
                pl.BlockSpec((1, 1, f), wsel),
                pl.BlockSpec((1, 1, f, d), wsel_layer),
                pl.BlockSpec((1, 1, d), wsel),
                pl.BlockSpec((DEINT, DEINT // 2), const),
                pl.BlockSpec((DEINT, DEINT // 2), const),
            ],
            out_specs=pl.BlockSpec(memory_space=pl.ANY),
            scratch_shapes=[pltpu.VMEM((2, MOE_BLOCK * ROW_CHUNKS, LANES), F32),
                            pltpu.VMEM((2, MOE_BLOCK * ROW_CHUNKS, LANES), F32),
                            pltpu.VMEM((d, f), BF16), pltpu.VMEM((d, f), BF16), pltpu.VMEM((f, d), BF16),
                            pltpu.SemaphoreType.DMA((2,)), pltpu.SemaphoreType.DMA((2,))],
        ),
        compiler_params=_cparams(("arbitrary",)),
        name="moe_experts",
    )(block_expert, n_valid, gsrc, gsrc, gdst, gdst, h, w1, b1g, b1l, w2, b2, p_even, p_odd)


def _combine_kernel(y0_ref, y1_ref, y2_ref, y3_ref, wt_ref, x_ref, mod_ref, o_ref):
    wt = wt_ref[0]
    g2 = mod_ref[0][5:6]
    y_refs = (y0_ref, y1_ref, y2_ref, y3_ref)
    wk = [wt[:, k:k + 1] for k in range(TOP_K)]
    tm = wt.shape[0]
    for c in range(ROW_CHUNKS):
        chunk = pl.ds(c, tm, stride=ROW_CHUNKS)
        y = wk[0] * y_refs[0][chunk, :]
        for k in range(1, TOP_K):
            y = y + wk[k] * y_refs[k][chunk, :]
        cs = slice(c * LANES, (c + 1) * LANES)
        o_ref[0, :, cs] = x_ref[0, :, cs] + g2[:, cs] * y


def _combine(yt, wts, x, mod, tm):
    b, s, d = x.shape
    nt = s // tm
    tiles = b * nt
    tok = lambda i, j: (i, j, 0)

    def yspec(k):
        return pl.BlockSpec((tm * ROW_CHUNKS, LANES), lambda i, j: (k * tiles + i * nt + j, 0))

    return pl.pallas_call(
        _combine_kernel,
        out_shape=jax.ShapeDtypeStruct((b, s, d), F32),
        grid=(b, nt),
        in_specs=[yspec(k) for k in range(TOP_K)] + [
            pl.BlockSpec((1, tm, LANES), tok),
            pl.BlockSpec((1, tm, d), tok),
            pl.BlockSpec((1, 6, d), lambda i, j: (i, 0, 0)),
        ],
        out_specs=pl.BlockSpec((1, tm, d), tok),
        compiler_params=_cparams(("arbitrary", "arbitrary")),
        name="moe_combine",
    )(yt, yt, yt, yt, wts, x, mod)


def _route_slots(top_idx, n_blocks):
    t = top_idx.shape[0]
    n_assign = t * TOP_K
    n_slots = n_blocks * MOE_BLOCK
    sorted_e, order = lax.sort_key_val(top_idx.reshape(-1), jnp.arange(n_assign, dtype=I32))
    experts = jnp.arange(N_EXPERTS + 1, dtype=I32)
    start = jnp.sum((sorted_e[None, :] < experts[:, None]).astype(I32), axis=1)
    counts = start[1:] - start[:-1]
    padded = (counts + MOE_BLOCK - 1) // MOE_BLOCK * MOE_BLOCK
    pad_end = jnp.cumsum(padded)
    pad_start = pad_end - padded
    blk_start = jnp.arange(n_blocks, dtype=I32) * MOE_BLOCK
    block_expert = jnp.minimum(jnp.sum((pad_end[None, :] <= blk_start[:, None]).astype(I32), axis=1), N_EXPERTS - 1)
    n_valid = jnp.clip(pad_start[block_expert] + counts[block_expert] - blk_start, 0, MOE_BLOCK).astype(I32)
    within = jnp.arange(MOE_BLOCK, dtype=I32)[None, :]
    valid = within < n_valid[:, None]
    rank = (blk_start - pad_start[block_expert] + start[block_expert])[:, None] + within
    slot_a = order[jnp.clip(rank, 0, n_assign - 1)]
    slot = blk_start[:, None] + within
    gsrc = jnp.where(valid, slot_a // TOP_K, 0) * ROW_CHUNKS
    gdst = jnp.where(valid, (slot_a % TOP_K) * t + slot_a // TOP_K, n_assign + slot % DUMP_ROWS) * ROW_CHUNKS
    shape = (n_blocks, 1, MOE_BLOCK)
    return gsrc.reshape(shape), gdst.reshape(shape), block_expert, n_valid


def _moe(x_mid, h, idx, wts, mod, layer, w1, b1, w2, b2, tm):
    b, s, d = x_mid.shape
    t = b * s
    n_blocks = -(-t * TOP_K // MOE_BLOCK) + N_EXPERTS
    top_idx = idx.reshape(t, LANES)[:, :TOP_K]
    gsrc, gdst, block_expert, n_valid = _route_slots(top_idx, n_blocks)
    yt = _experts(h.reshape(t * ROW_CHUNKS, LANES), gsrc, gdst, block_expert, n_valid, layer, w1,
                  b1[:, None, 0::2], b1[:, None, 1::2], w2, b2[:, None, :])
    return _combine(yt, wts, x_mid, mod, tm)


def _pad_cols(w, n):
    return jnp.pad(w, ((0, 0), (0, n - w.shape[-1])))


def _attn_colgain(diff_qk_gain, nsa_q_gain, nsa_k_gain):
    scale = HEAD_DIM ** -0.5 * LOG2E
    ones = jnp.ones((LANES,), F32)
    parts = [jnp.tile(diff_qk_gain[0] * scale, 8), jnp.tile(diff_qk_gain[1], 8), jnp.ones((512,), F32),
             jnp.tile(nsa_q_gain * scale, 8), ones, ones, jnp.tile(nsa_k_gain[1], 2), ones,
             jnp.tile(nsa_k_gain[2], 2), ones, ones]
    return jnp.concatenate(parts).reshape(1, W_IN_PAD)


def _overlap_matrix(s):
    n_sel = s // SEL_BLOCK
    ncp = s // CMP_STRIDE
    c_start = np.arange(ncp) * CMP_STRIDE
    s_start = np.arange(n_sel) * SEL_BLOCK
    ovl = (c_start[None, :] < s_start[:, None] + SEL_BLOCK) & (c_start[None, :] + CMP_BLOCK > s_start[:, None])
    ovl[:, ncp - 1] = False
    return jnp.asarray(ovl.astype(np.float32), dtype=BF16)


def _attention_layer(x, mod, gain, w_in, w_out, diff_qk_gain, diff_lambda, diff_subln, nsa_q_gain, nsa_k_gain,
                     cmp_pos, cmp_w1, cmp_w2, lam_init):
    b, s, d = x.shape
    n_gate = NSA_Q_HEADS * 3
    gate_perm = np.arange(n_gate).reshape(NSA_KV_HEADS, NSA_GROUP, 3).transpose(0, 2, 1).reshape(-1)
    w_pad = jnp.concatenate([w_in[:, :COL_GATE], w_in[:, COL_GATE + gate_perm],
                             jnp.zeros((d, W_IN_PAD - COL_GATE - n_gate), w_in.dtype)], axis=1).astype(BF16)
    zb, dqT, dvT, nqT, vselT, vwinT, gT = _attn_in_proj(
        x, mod, gain, w_pad, _attn_colgain(diff_qk_gain, nsa_q_gain, nsa_k_gain))

    o_diff = _diff_attention(zb, dqT.reshape(b, DIFF_HEADS, 2, HEAD_DIM, s), dvT,
                             _score_bound(diff_qk_gain[0], diff_qk_gain[1]), diff_lambda, diff_subln, lam_init,
                             DIFF_TQ, ATTN_TM)

    ncp = s // CMP_STRIDE
    chunks = zb[:, :, KZ_KCMP:KZ_KCMP + 2 * LANES].reshape(b, ncp, CMP_STRIDE, 2, NSA_KV_HEADS, HEAD_DIM)
    chunks = chunks.transpose(0, 3, 4, 1, 2, 5).reshape(b, 4, ncp, CMP_STRIDE * HEAD_DIM)
    cmp_out = _compress(chunks, cmp_pos.reshape(2, 1, CMP_BLOCK * HEAD_DIM), cmp_w1, cmp_w2, nsa_k_gain[0:1])
    kc = cmp_out[:, 0:2].transpose(0, 2, 1, 3).reshape(b, ncp, LANES).astype(BF16)
    vcT = cmp_out[:, 2:4].transpose(0, 1, 3, 2).astype(BF16)
    gatesT = gT[:, :n_gate].reshape(b, NSA_KV_HEADS, 3, NSA_GROUP, s)
    shifts = jnp.stack([_score_bound(nsa_q_gain, nsa_k_gain[1]), _score_bound(nsa_q_gain, nsa_k_gain[2])])
    o_nsa = _nsa_attention(zb, nqT.reshape(b, NSA_KV_HEADS, NSA_GROUP, HEAD_DIM, s), vselT, vwinT, kc, vcT,
                           _overlap_matrix(s), gatesT, shifts, NSA_TILE)
    return o_diff, o_nsa


def kernel(x, c, mod_w, mod_b, norm_mix, norm_ffn, attn_w_in, attn_w_out, diff_qk_gain, diff_lambda, diff_subln,
           nsa_q_gain, nsa_k_gain, nsa_cmp_pos, nsa_cmp_w1, nsa_cmp_w2, conv_pw1_w, conv_pw1_b, conv_dw_w,
           conv_dw_b, conv_ln_g, conv_ln_b, conv_pw2_w, conv_pw2_b, router_w, router_b, moe_w1, moe_b1, moe_w2,
           moe_b2):
    b, s, d = x.shape
    depth = mod_w.shape[0]
    tm = 512 if s % 512 == 0 else 256
    mods = _modulation(c, mod_w, mod_b).reshape(depth, b, 6, d)
    half = d // 2
    for i in range(depth):
        mod = mods[i]
        j = i // 2
        if i % 2 == 0:
            lam_init = 0.8 - 0.6 * math.exp(-0.3 * i)
            o_diff, o_nsa = _attention_layer(
                x, mod, norm_mix[i:i + 1], attn_w_in[j], attn_w_out[j], diff_qk_gain[j], diff_lambda[j],
                diff_subln[j], nsa_q_gain[j], nsa_k_gain[j], nsa_cmp_pos[j], nsa_cmp_w1[j], nsa_cmp_w2[j], lam_init)
            a1, a2, c1, c2 = o_diff, o_nsa, 0, 0
            w_o = attn_w_out[j].astype(BF16)
            bias = jnp.zeros((1, d), F32)
        else:
            u = _conv_in_proj(x, mod, norm_mix[i:i + 1], conv_pw1_w[j].astype(BF16), conv_pw1_b[j:j + 1], tm)
            v = _dwconv_ln_swish(u, conv_dw_w[j].reshape(CONV_WIDTH, d), conv_dw_b[j:j + 1], conv_ln_g[j:j + 1],
                                 conv_ln_b[j:j + 1], 256)
            a1, a2, c1, c2 = v, v, 0, 1
            w_o = conv_pw2_w[j].astype(BF16)
            bias = conv_pw2_b[j:j + 1]
        rw = _pad_cols(router_w[i], LANES)
        rw_hi = rw.astype(BF16)
        rw_lo = (rw - rw_hi.astype(F32)).astype(BF16)
        rb = jnp.concatenate([router_b[i], jnp.full((LANES - N_EXPERTS,), NEG, F32)]).reshape(1, LANES)
        x_mid, h, idx, wts = _out_proj_router(a1, a2, c1, c2, w_o[:half], w_o[half:], bias, x, mod,
                                              norm_ffn[i:i + 1], rw_hi, rw_lo, rb, tm)
        x = _moe(x_mid, h, idx, wts, mod, i, moe_w1, moe_b1[i], moe_w2, moe_b2[i], 256)
    return x
```

```python
import functools
import math

import jax
import jax.numpy as jnp
import numpy as np
from jax import lax
from jax.experimental import pallas as pl
from jax.experimental.pallas import tpu as pltpu

F32 = jnp.float32
BF16 = jnp.bfloat16
I32 = jnp.int32

D_MODEL = 1024
HEAD_DIM = 64
DIFF_HEADS = 4
NSA_Q_HEADS = 8
NSA_KV_HEADS = 2
NSA_GROUP = NSA_Q_HEADS // NSA_KV_HEADS
CMP_BLOCK = 32
CMP_STRIDE = 16
CMP_HIDDEN = 256
SEL_BLOCK = 64
SEL_TOP = 16
SEL_FORCED_SCORE = 1.0e4
WINDOW = 512
CONV_WIDTH = 31
N_EXPERTS = 32
TOP_K = 4
SWIGLU_ALPHA = 1.702
SWIGLU_LIMIT = 7.0
MOE_BLOCK = 256
NORM_EPS = 1e-6

LANES = 128
NEG = -1e30
TAKEN = -3e38
VMEM_LIMIT = 56 * 1024 * 1024

COL_DQ, COL_DK, COL_DV, COL_NQ = 0, 512, 1024, 1536
COL_KCMP, COL_VCMP, COL_KSEL, COL_VSEL, COL_KWIN, COL_VWIN = 2048, 2176, 2304, 2432, 2560, 2688
COL_GATE = 2816
W_IN_PAD = 2944
NORM_SLABS = frozenset(list(range(0, 8)) + list(range(12, 16)) + [COL_KSEL // LANES, COL_KWIN // LANES])


def _cparams(sem):
    return pltpu.CompilerParams(dimension_semantics=sem, vmem_limit_bytes=VMEM_LIMIT)


def _sigmoid(v):
    return 1.0 / (1.0 + jnp.exp(-v))


def _norm_mod(x, g, sc, sh):
    ms = jnp.mean(x * x, axis=-1, keepdims=True)
    return (x * lax.rsqrt(ms + NORM_EPS)) * g * (1.0 + sc) + sh


def _mod_kernel(c_ref, w_ref, b_ref, o_ref):
    c = c_ref[...]
    cond = c * _sigmoid(c)
    o_ref[0] = jnp.dot(cond, w_ref[0], preferred_element_type=F32, precision=lax.Precision.HIGHEST) + b_ref[0]


def _modulation(c, mod_w, mod_b):
    depth, d, n = mod_w.shape
    b = c.shape[0]
    tn = 1536
    return pl.pallas_call(
        _mod_kernel,
        out_shape=jax.ShapeDtypeStruct((depth, b, n), F32),
        grid=(depth, n // tn),
        in_specs=[
            pl.BlockSpec((b, d), lambda i, j: (0, 0)),
            pl.BlockSpec((1, d, tn), lambda i, j: (i, 0, j)),
            pl.BlockSpec((1, 1, tn), lambda i, j: (i, 0, j)),
        ],
        out_specs=pl.BlockSpec((1, b, tn), lambda i, j: (i, 0, j)),
        compiler_params=_cparams(("parallel", "parallel")),
        name="modulation",
    )(c, mod_w, mod_b.reshape(depth, 1, n))


KZ_DK, KZ_KCMP, KZ_VCMP, KZ_KSEL, KZ_KWIN, KZ_WIDTH = 0, 512, 640, 768, 896, 1024
KEY_COLS = {COL_DK + 128 * u: KZ_DK + 128 * u for u in range(4)}
KEY_COLS.update({COL_KCMP: KZ_KCMP, COL_VCMP: KZ_VCMP, COL_KSEL: KZ_KSEL, COL_KWIN: KZ_KWIN})
ATTN_TM = 512
DIFF_TQ = 1024
NSA_TILE = 512
GATE_ROWS = 32


def _in_attn_kernel(x_ref, mod_ref, g_ref, w_ref, cg_ref, z_ref, dqT_ref, dvT_ref, nqT_ref, vsT_ref, vwT_ref, gT_ref):
    m = mod_ref[0]
    h = _norm_mod(x_ref[0], g_ref[...], m[1:2], m[0:1]).astype(BF16)
    lo = lax.broadcasted_iota(I32, (1, LANES), 1) < HEAD_DIM
    groups = ((0, 512), (512, 1024), (1024, 1536), (1536, 2048), (2048, 2816), (2816, 2944))
    halves = ATTN_TM // NSA_TILE
    for c0, c1 in groups:
        z = jnp.dot(h, w_ref[:, c0:c1], preferred_element_type=F32)
        for s in range((c1 - c0) // LANES):
            a0 = c0 + s * LANES
            zs = z[:, s * LANES:(s + 1) * LANES]
            if a0 == COL_GATE:
                gT_ref[0] = _sigmoid(zs).T[:GATE_ROWS]
                continue
            if a0 // LANES in NORM_SLABS:
                z2 = zs * zs
                s_lo = jnp.sum(jnp.where(lo, z2, 0.0), axis=-1, keepdims=True)
                s_hi = jnp.sum(jnp.where(lo, 0.0, z2), axis=-1, keepdims=True)
                inv = jnp.where(lo, lax.rsqrt(s_lo * (1.0 / HEAD_DIM) + NORM_EPS),
                                lax.rsqrt(s_hi * (1.0 / HEAD_DIM) + NORM_EPS))
                zs = zs * inv
            zs = zs * cg_ref[:, a0:a0 + LANES]
            if COL_DQ <= a0 < COL_DK:
                dqT_ref[0, a0 - COL_DQ:a0 - COL_DQ + LANES, :] = zs.T.astype(BF16)
            elif COL_DV <= a0 < COL_NQ:
                dvT_ref[0, 0, a0 - COL_DV:a0 - COL_DV + LANES, :] = zs.T.astype(BF16)
            elif COL_NQ <= a0 < COL_KCMP:
                nqT_ref[0, a0 - COL_NQ:a0 - COL_NQ + LANES, :] = zs.T.astype(BF16)
            elif a0 in (COL_VSEL, COL_VWIN):
                zt = zs.T.astype(BF16)
                ref = vsT_ref if a0 == COL_VSEL else vwT_ref
                for u in range(halves):
                    ref[0, u] = zt[:, u * NSA_TILE:(u + 1) * NSA_TILE]
            else:
                k0 = KEY_COLS[a0]
                z_ref[0, :, k0:k0 + LANES] = zs.astype(BF16)


def _attn_in_proj(x, mod, gain, w_pad, colgain):
    b, s, d = x.shape
    tm = ATTN_TM
    halves = tm // NSA_TILE
    tok = lambda i, j: (i, j, 0)
    featT = lambda i, j: (i, 0, j)
    return pl.pallas_call(
        _in_attn_kernel,
        out_shape=(jax.ShapeDtypeStruct((b, s, KZ_WIDTH), BF16),
                   jax.ShapeDtypeStruct((b, 512, s), BF16),
                   jax.ShapeDtypeStruct((b, s // tm, 512, tm), BF16),
                   jax.ShapeDtypeStruct((b, 512, s), BF16),
                   jax.ShapeDtypeStruct((b, s // NSA_TILE, LANES, NSA_TILE), BF16),
                   jax.ShapeDtypeStruct((b, s // NSA_TILE, LANES, NSA_TILE), BF16),
                   jax.ShapeDtypeStruct((b, GATE_ROWS, s), F32)),
        grid=(b, s // tm),
        in_specs=[
            pl.BlockSpec((1, tm, d), tok),
            pl.BlockSpec((1, 6, d), lambda i, j: (i, 0, 0)),
            pl.BlockSpec((1, d), lambda i, j: (0, 0)),
            pl.BlockSpec((d, W_IN_PAD), lambda i, j: (0, 0)),
            pl.BlockSpec((1, W_IN_PAD), lambda i, j: (0, 0)),
        ],
        out_specs=(pl.BlockSpec((1, tm, KZ_WIDTH), tok),
                   pl.BlockSpec((1, 512, tm), featT),
                   pl.BlockSpec((1, 1, 512, tm), lambda i, j: (i, j, 0, 0)),
                   pl.BlockSpec((1, 512, tm), featT),
                   pl.BlockSpec((1, halves, LANES, NSA_TILE), lambda i, j: (i, j, 0, 0)),
                   pl.BlockSpec((1, halves, LANES, NSA_TILE), lambda i, j: (i, j, 0, 0)),
                   pl.BlockSpec((1, GATE_ROWS, tm), featT)),
        compiler_params=_cparams(("parallel", "parallel")),
        name="attn_in_proj",
    )(x, mod, gain, w_pad, colgain)


FIXED_SHIFT_LIMIT = 57.0
FIXED_TILES_PER_STEP = 4
LOG2E = math.log2(math.e)


def _score_bound(gain_q, gain_k):
    return (math.sqrt(HEAD_DIM) * LOG2E) * jnp.max(jnp.abs(gain_q * gain_k))


SUM_ROWS = 8


def _with_ones(vt):
    return jnp.concatenate([vt, jnp.ones((SUM_ROWS, vt.shape[1]), vt.dtype)], axis=0)


def _fixed_update(ss, vts, shift, acc_ref):
    ps = [jnp.exp2(s - shift).astype(BF16) for s in ss]
    acc_ref[...] += sum(jnp.dot(_with_ones(vt), p, preferred_element_type=F32) for vt, p in zip(vts, ps))


def _online_update(ss, vts, m_ref, acc_ref):
    for s, vt in zip(ss, vts):
        m_old = m_ref[...]
        m_new = jnp.maximum(m_old, jnp.max(s, axis=0, keepdims=True))
        alpha = jnp.exp2(m_old - m_new)
        p = jnp.exp2(s - m_new).astype(BF16)
        acc_ref[...] = alpha * acc_ref[...] + jnp.dot(_with_ones(vt), p, preferred_element_type=F32)
        m_ref[...] = m_new


def _tile_loops(n_full, tiles, width, n_masked=1):
    def group(jj, c):
        tiles(tuple(width * jj + u for u in range(width)), (False,) * width)
        return c

    lax.fori_loop(0, n_full // width, group, 0)
    rem = n_full % width
    base = n_full - rem
    for r in range(width):
        @pl.when(rem == r)
        def _():
            tiles(tuple(base + u for u in range(r + n_masked)), (False,) * r + (True,) * n_masked)


def _diff_kernel(mb_ref, lam_ref, qT_ref, k_ref, vT_ref, sub_ref, o_ref, m0, m1, a0, a1, *, tq, tk, lam_init):
    i = pl.program_id(2)
    m_refs, acc_refs = (m0, m1), (a0, a1)
    q = qT_ref[0, 0]
    zero = jnp.zeros((HEAD_DIM, tq), BF16)
    qp = (jnp.concatenate([q[0], zero], axis=0), jnp.concatenate([zero, q[1]], axis=0))
    for mm in range(2):
        m_refs[mm][...] = jnp.full(m_refs[mm].shape, NEG, F32)
        acc_refs[mm][...] = jnp.zeros(acc_refs[mm].shape, F32)
    shift = mb_ref[0]
    dv = 2 * HEAD_DIM

    def tiles(js, masks, fixed):
        kts = [k_ref[0, pl.ds(pl.multiple_of(j * tk, tk), tk), :] for j in js]
        vts = [vT_ref[0, j] for j in js]
        for mm in range(2):
            ss = []
            for j, kt, masked in zip(js, kts, masks):
                s = jnp.dot(kt, qp[mm], preferred_element_type=F32)
                if masked:
                    kpos = j * tk + lax.broadcasted_iota(I32, (tk, tq), 0)
                    t = i * tq + lax.broadcasted_iota(I32, (tk, tq), 1)
                    s = jnp.where(kpos <= t, s, NEG)
                ss.append(s)
            if fixed:
                _fixed_update(ss, vts, shift, acc_refs[mm])
            else:
                _online_update(ss, vts, m_refs[mm], acc_refs[mm])

    n_full = (i * tq) // tk
    n_masked = max(tq // tk, 1)

    @pl.when(shift <= FIXED_SHIFT_LIMIT)
    def _():
        _tile_loops(n_full, functools.partial(tiles, fixed=True), FIXED_TILES_PER_STEP, n_masked)

    @pl.when(shift > FIXED_SHIFT_LIMIT)
    def _():
        _tile_loops(n_full, functools.partial(tiles, fixed=False), 1, n_masked)

    lv = lam_ref[...]
    lam = (jnp.exp(jnp.sum(lv[0:1] * lv[1:2], axis=-1, keepdims=True))
           - jnp.exp(jnp.sum(lv[2:3] * lv[3:4], axis=-1, keepdims=True)) + lam_init)
    o = a0[:dv] / a0[dv:dv + 1] - lam * (a1[:dv] / a1[dv:dv + 1])
    ms = jnp.mean(o * o, axis=0, keepdims=True)
    o = o * lax.rsqrt(ms + NORM_EPS) * sub_ref[...] * (1.0 - lam_init)
    o_ref[0] = o.T.astype(BF16)


def _diff_attention(zb, qT, vT, shift, lam_vecs, subln, lam_init, tq, tk):
    b, s, _ = zb.shape
    kern = functools.partial(_diff_kernel, tq=tq, tk=tk, lam_init=lam_init)
    dv = 2 * HEAD_DIM
    return pl.pallas_call(
        kern,
        out_shape=jax.ShapeDtypeStruct((b, s, DIFF_HEADS * dv), BF16),
        grid=(b, DIFF_HEADS, s // tq),
        in_specs=[
            pl.BlockSpec(memory_space=pltpu.SMEM),
            pl.BlockSpec((4, HEAD_DIM), lambda bi, h, i: (0, 0)),
            pl.BlockSpec((1, 1, 2, HEAD_DIM, tq), lambda bi, h, i: (bi, h, 0, 0, i)),
            pl.BlockSpec((1, s, LANES), lambda bi, h, i: (bi, 0, KZ_DK // LANES + h)),
            pl.BlockSpec((1, s // tk, 2 * HEAD_DIM, tk), lambda bi, h, i: (bi, 0, h, 0)),
            pl.BlockSpec((2 * HEAD_DIM, 1), lambda bi, h, i: (0, 0)),
        ],
        out_specs=pl.BlockSpec((1, tq, LANES), lambda bi, h, i: (bi, i, h)),
        scratch_shapes=[pltpu.VMEM((1, tq), F32)] * 2 + [pltpu.VMEM((dv + SUM_ROWS, tq), F32)] * 2,
        compiler_params=_cparams(("parallel", "parallel", "arbitrary")),
        name="diff_attention",
    )(shift.reshape(1), lam_vecs, qT, zb, vT, subln.reshape(-1, 1))


def _gelu_tanh(v):
    return 0.5 * v * (1.0 + jnp.tanh(math.sqrt(2.0 / math.pi) * (v + 0.044715 * (v * v * v))))


def _cmp_kernel(c_ref, pos_ref, w1_ref, w2_ref, kg_ref, o_ref, *, n_cmp):
    jh = pl.program_id(1)
    half = CMP_STRIDE * HEAD_DIM
    c = c_ref[0, 0]
    w1a = w1_ref[0, :half, :].astype(BF16)
    w1b = w1_ref[0, half:, :].astype(BF16)
    pos = jnp.broadcast_to(pos_ref[0], (8, 2 * half)).astype(BF16)
    bias = (jnp.dot(pos[:, :half], w1a, preferred_element_type=F32)
            + jnp.dot(pos[:, half:], w1b, preferred_element_type=F32))[0:1]
    u = jnp.dot(c, w1a, preferred_element_type=F32)
    v = jnp.dot(c, w1b, preferred_element_type=F32)
    ncp = u.shape[0]
    hid = _gelu_tanh(u + pltpu.roll(v, ncp - 1, 0) + bias)
    y = jnp.dot(hid.astype(BF16), w2_ref[0].astype(BF16), preferred_element_type=F32)
    yn = y * lax.rsqrt(jnp.mean(y * y, axis=-1, keepdims=True) + NORM_EPS) * kg_ref[...]
    y = jnp.where(jh < NSA_KV_HEADS, yn, y)
    row = lax.broadcasted_iota(I32, y.shape, 0)
    o_ref[0, 0] = jnp.where(row < n_cmp, y, 0.0)


def _compress(chunks, pos, w1, w2, kgain):
    b, _, ncp, cd = chunks.shape
    kern = functools.partial(_cmp_kernel, n_cmp=ncp - 1)
    return pl.pallas_call(
        kern,
        out_shape=jax.ShapeDtypeStruct((b, 4, ncp, HEAD_DIM), F32),
        grid=(b, 4),
        in_specs=[
            pl.BlockSpec((1, 1, ncp, cd), lambda bi, j: (bi, j, 0, 0)),
            pl.BlockSpec((1, 1, 2 * cd), lambda bi, j: (j // NSA_KV_HEADS, 0, 0)),
            pl.BlockSpec((1, 2 * cd, CMP_HIDDEN), lambda bi, j: (j // NSA_KV_HEADS, 0, 0)),
            pl.BlockSpec((1, CMP_HIDDEN, HEAD_DIM), lambda bi, j: (j // NSA_KV_HEADS, 0, 0)),
            pl.BlockSpec((1, HEAD_DIM), lambda bi, j: (0, 0)),
        ],
        out_specs=pl.BlockSpec((1, 1, ncp, HEAD_DIM), lambda bi, j: (bi, j, 0, 0)),
        compiler_params=_cparams(("parallel", "parallel")),
        name="nsa_compress",
    )(chunks, pos, w1, w2, kgain)


def _nsa_kernel(mb_ref, qT_ref, ksel_ref, vselT_ref, kwin_ref, vwinT_ref, kc_ref, vcT_ref, ovl_ref, gate_ref, o_ref,
                selb_sc, m_s, m_w, a_s, a_w, *, tq, n_top):
    hk = pl.program_id(1)
    i = pl.program_id(2)
    g4 = NSA_GROUP
    nq = g4 * tq
    q4 = jnp.concatenate([qT_ref[0, 0, g] for g in range(g4)], axis=1)
    half = lax.broadcasted_iota(I32, (2 * HEAD_DIM, nq), 0) // HEAD_DIM
    qp = jnp.where(half == hk, jnp.concatenate([q4, q4], axis=0), jnp.zeros((), BF16))
    t1 = i * tq + lax.broadcasted_iota(I32, (1, tq), 1)

    def tile4(a):
        return jnp.concatenate([a] * g4, axis=1)

    ncp = kc_ref.shape[1]
    sc = jnp.dot(kc_ref[0], qp, preferred_element_type=F32)
    cend = CMP_STRIDE * lax.broadcasted_iota(I32, (ncp, tq), 0) + (CMP_BLOCK - 1)
    sc = sc + tile4(jnp.where(cend <= t1, 0.0, NEG))
    e = jnp.where(sc > 0.5 * NEG, jnp.exp2(sc - jnp.max(sc, axis=0, keepdims=True)), 0.0)
    p = e / jnp.maximum(jnp.sum(e, axis=0, keepdims=True), 1e-30)
    o_c = jnp.dot(vcT_ref[0, 0], p.astype(BF16), preferred_element_type=F32)

    psum = p[:, 0:tq]
    for g in range(1, g4):
        psum = psum + p[:, g * tq:(g + 1) * tq]
    p_hi = psum.astype(BF16)
    p_lo = (psum - p_hi.astype(F32)).astype(BF16)
    imp = (jnp.dot(ovl_ref[...], p_hi, preferred_element_type=F32)
           + jnp.dot(ovl_ref[...], p_lo, preferred_element_type=F32))
    n_sel = imp.shape[0]
    jrow = lax.broadcasted_iota(I32, (n_sel, tq), 0)
    cur = t1 // SEL_BLOCK
    forced = (jrow == 0) | (jrow == cur) | (jrow == cur - 1)
    score = jnp.where(forced, SEL_FORCED_SCORE, jnp.where(jrow <= cur, imp, NEG))
    selb = jnp.full((n_sel, tq), NEG, F32)
    jrow_f = jrow.astype(F32)
    for _ in range(n_top):
        best = jnp.max(score, axis=0, keepdims=True)
        pick = jnp.min(jnp.where(score == best, jrow_f, float(n_sel)), axis=0, keepdims=True)
        hit = jrow_f == pick
        selb = jnp.where(hit, 0.0, selb)
        score = jnp.where(hit, TAKEN, score)
    selb_sc[...] = selb

    for ref in (m_s, m_w):
        ref[...] = jnp.full(ref.shape, NEG, F32)
    for ref in (a_s, a_w):
        ref[...] = jnp.zeros(ref.shape, F32)
    tk = tq
    bpt = tk // SEL_BLOCK
    krow = lax.broadcasted_iota(I32, (tk, tq), 0)
    shift_s, shift_w = mb_ref[0], mb_ref[1]
    fixed = jnp.maximum(shift_s, shift_w) <= FIXED_SHIFT_LIMIT

    def sel_tiles(js, masks, fixed):
        ss = []
        for j, masked in zip(js, masks):
            kt = ksel_ref[0, pl.ds(pl.multiple_of(j * tk, tk), tk), :]
            rows = [jnp.broadcast_to(selb_sc[pl.ds(j * bpt + r, 1), :], (SEL_BLOCK, tq)) for r in range(bpt)]
            bias = jnp.concatenate(rows, axis=0)
            if masked:
                bias = jnp.where(j * tk + krow <= t1, bias, NEG)
            ss.append(jnp.dot(kt, qp, preferred_element_type=F32) + tile4(bias))
        vts = [vselT_ref[0, j] for j in js]
        if fixed:
            _fixed_update(ss, vts, shift_s, a_s)
        else:
            _online_update(ss, vts, m_s, a_s)

    def win_tiles(js, fixed):
        ss = []
        for j in js:
            kt = kwin_ref[0, pl.ds(pl.multiple_of(j * tk, tk), tk), :]
            kpos = j * tk + krow
            bias = jnp.where(kpos <= t1, jnp.where(kpos > t1 - WINDOW, 0.0, NEG), NEG)
            ss.append(jnp.dot(kt, qp, preferred_element_type=F32) + tile4(bias))
        vts = [vwinT_ref[0, j] for j in js]
        if fixed:
            _fixed_update(ss, vts, shift_w, a_w)
        else:
            _online_update(ss, vts, m_w, a_w)

    max_win = (WINDOW + tk - 1) // tk + 1
    n_win = jnp.minimum(i + 1, max_win)
    for use_fixed in (True, False):
        @pl.when(fixed == use_fixed)
        def _():
            _tile_loops(i, functools.partial(sel_tiles, fixed=use_fixed), FIXED_TILES_PER_STEP if use_fixed else 1)
            for c in range(1, max_win + 1):
                @pl.when(n_win == c)
                def _():
                    win_tiles(tuple(i - (c - 1) + u for u in range(c)), use_fixed)

    def gate_row(br):
        gt = gate_ref[0, 0, br]
        return jnp.concatenate([gt[g:g + 1] for g in range(g4)], axis=1)

    dv = HEAD_DIM
    out = (o_c * gate_row(0) + (a_s[:dv] / a_s[dv:dv + 1]) * gate_row(1) + (a_w[:dv] / a_w[dv:dv + 1]) * gate_row(2))
    stacked = jnp.concatenate([out[:, g * tq:(g + 1) * tq] for g in range(g4)], axis=0)
    o_ref[0] = stacked.T.astype(BF16)


def _nsa_attention(zb, qT, vselT, vwinT, kc, vcT, ovl, gatesT, shifts, tq):
    b, s, _ = zb.shape
    nt = s // tq
    n_sel = s // SEL_BLOCK
    ncp = kc.shape[1]
    kern = functools.partial(_nsa_kernel, tq=tq, n_top=min(SEL_TOP, n_sel))
    gd = NSA_GROUP * HEAD_DIM
    nq = NSA_GROUP * tq
    return pl.pallas_call(
        kern,
        out_shape=jax.ShapeDtypeStruct((b, s, NSA_Q_HEADS * HEAD_DIM), BF16),
        grid=(b, NSA_KV_HEADS, nt),
        in_specs=[
            pl.BlockSpec(memory_space=pltpu.SMEM),
            pl.BlockSpec((1, 1, NSA_GROUP, HEAD_DIM, tq), lambda bi, h, i: (bi, h, 0, 0, i)),
            pl.BlockSpec((1, s, LANES), lambda bi, h, i: (bi, 0, KZ_KSEL // LANES)),
            pl.BlockSpec((1, nt, HEAD_DIM, tq), lambda bi, h, i: (bi, 0, h, 0)),
            pl.BlockSpec((1, s, LANES), lambda bi, h, i: (bi, 0, KZ_KWIN // LANES)),
            pl.BlockSpec((1, nt, HEAD_DIM, tq), lambda bi, h, i: (bi, 0, h, 0)),
            pl.BlockSpec((1, ncp, LANES), lambda bi, h, i: (bi, 0, 0)),
            pl.BlockSpec((1, 1, HEAD_DIM, ncp), lambda bi, h, i: (bi, h, 0, 0)),
            pl.BlockSpec((n_sel, ncp), lambda bi, h, i: (0, 0)),
            pl.BlockSpec((1, 1, 3, NSA_GROUP, tq), lambda bi, h, i: (bi, h, 0, 0, i)),
        ],
        out_specs=pl.BlockSpec((1, tq, gd), lambda bi, h, i: (bi, i, h)),
        scratch_shapes=([pltpu.VMEM((n_sel, tq), F32)] + [pltpu.VMEM((1, nq), F32)] * 2
                        + [pltpu.VMEM((HEAD_DIM + SUM_ROWS, nq), F32)] * 2),
        compiler_params=_cparams(("parallel", "parallel", "arbitrary")),
        name="nsa_attention",
    )(shifts, qT, zb, vselT, zb, vwinT, kc, vcT, ovl, gatesT)


def _in_conv_kernel(x_ref, mod_ref, g_ref, w_ref, b_ref, u_ref):
    m = mod_ref[0]
    d = x_ref.shape[-1]
    h = _norm_mod(x_ref[0], g_ref[...], m[1:2], m[0:1]).astype(BF16)
    a = jnp.dot(h, w_ref[:, :d], preferred_element_type=F32) + b_ref[:, :d]
    g = jnp.dot(h, w_ref[:, d:], preferred_element_type=F32) + b_ref[:, d:]
    u_ref[0] = a * _sigmoid(g)


def _conv_in_proj(x, mod, gain, w, bias, tm):
    b, s, d = x.shape
    return pl.pallas_call(
        _in_conv_kernel,
        out_shape=jax.ShapeDtypeStruct((b, s, d), F32),
        grid=(b, s // tm),
        in_specs=[
            pl.BlockSpec((1, tm, d), lambda i, j: (i, j, 0)),
            pl.BlockSpec((1, 6, d), lambda i, j: (i, 0, 0)),
            pl.BlockSpec((1, d), lambda i, j: (0, 0)),
            pl.BlockSpec((d, 2 * d), lambda i, j: (0, 0)),
            pl.BlockSpec((1, 2 * d), lambda i, j: (0, 0)),
        ],
        out_specs=pl.BlockSpec((1, tm, d), lambda i, j: (i, j, 0)),
        compiler_params=_cparams(("parallel", "parallel")),
        name="conv_in_proj",
    )(x, mod, gain, w, bias)


HALO = 32


def _dwconv_kernel(u_ref, halo_ref, w_ref, b_ref, lg_ref, lb_ref, o_ref, buf_sc, acc_sc, *, tm):
    i = pl.program_id(1)
    d = u_ref.shape[-1]
    buf_sc[0:HALO, :] = jnp.where(i > 0, halo_ref[0], 0.0)
    buf_sc[HALO:, :] = u_ref[0]
    off = HALO - (CONV_WIDTH - 1)
    cw = 256
    for c in range(d // cw):
        cs = slice(c * cw, (c + 1) * cw)
        acc = jnp.zeros((tm, cw), F32) + b_ref[:, cs]
        for j in range(CONV_WIDTH):
            acc = acc + w_ref[j:j + 1, cs] * buf_sc[off + j:off + j + tm, cs]
        acc_sc[:, cs] = acc
    y = acc_sc[...]
    mu = jnp.mean(y, axis=-1, keepdims=True)
    yc = y - mu
    var = jnp.mean(yc * yc, axis=-1, keepdims=True)
    yn = yc * lax.rsqrt(var + NORM_EPS) * lg_ref[...] + lb_ref[...]
    o_ref[0] = (yn * _sigmoid(yn)).astype(BF16)


def _dwconv_ln_swish(u, dw_w, dw_b, ln_g, ln_b, tm):
    b, s, d = u.shape
    kern = functools.partial(_dwconv_kernel, tm=tm)
    hb = tm // HALO
    return pl.pallas_call(
        kern,
        out_shape=jax.ShapeDtypeStruct((b, s, d), BF16),
        grid=(b, s // tm),
        in_specs=[
            pl.BlockSpec((1, tm, d), lambda bi, i: (bi, i, 0)),
            pl.BlockSpec((1, HALO, d), lambda bi, i: (bi, jnp.maximum(i * hb - 1, 0), 0)),
            pl.BlockSpec((CONV_WIDTH, d), lambda bi, i: (0, 0)),
            pl.BlockSpec((1, d), lambda bi, i: (0, 0)),
            pl.BlockSpec((1, d), lambda bi, i: (0, 0)),
            pl.BlockSpec((1, d), lambda bi, i: (0, 0)),
        ],
        out_specs=pl.BlockSpec((1, tm, d), lambda bi, i: (bi, i, 0)),
        scratch_shapes=[pltpu.VMEM((tm + HALO, d), F32), pltpu.VMEM((tm, d), F32)],
        compiler_params=_cparams(("parallel", "parallel")),
        name="dwconv_ln_swish",
    )(u, u, dw_w, dw_b, ln_g, ln_b)


def _out_router_kernel(a1_ref, a2_ref, w1_ref, w2_ref, b_ref, x_ref, mod_ref, g_ref, rwh_ref, rwl_ref, rb_ref,
                       xo_ref, h_ref, idx_ref, wt_ref):
    m = mod_ref[0]
    y = (jnp.dot(a1_ref[0], w1_ref[...], preferred_element_type=F32)
         + jnp.dot(a2_ref[0], w2_ref[...], preferred_element_type=F32) + b_ref[...])
    x = x_ref[0] + m[2:3] * y
    xo_ref[0] = x
    h = _norm_mod(x, g_ref[...], m[4:5], m[3:4])
    chunks = h.shape[1] // LANES
    for c in range(chunks):
        h_ref[0, pl.ds(c, h.shape[0], stride=chunks), :] = h[:, c * LANES:(c + 1) * LANES]
    h_hi = h.astype(BF16)
    h_lo = (h - h_hi.astype(F32)).astype(BF16)
    logits = (jnp.dot(h_hi, rwh_ref[...], preferred_element_type=F32)
              + jnp.dot(h_lo, rwh_ref[...], preferred_element_type=F32)
              + jnp.dot(h_hi, rwl_ref[...], preferred_element_type=F32)) + rb_ref[...]
    lane = lax.broadcasted_iota(I32, logits.shape, 1)
    lane_f = lane.astype(F32)
    idx_out = jnp.zeros(logits.shape, F32)
    val_out = jnp.full(logits.shape, NEG, F32)
    for k in range(TOP_K):
        best = jnp.max(logits, axis=-1, keepdims=True)
        pick = jnp.min(jnp.where(logits == best, lane_f, float(LANES)), axis=-1, keepdims=True)
        idx_out = jnp.where(lane == k, pick, idx_out)
        val_out = jnp.where(lane == k, best, val_out)
        logits = jnp.where(lane_f == pick, TAKEN, logits)
    e = jnp.where(lane < TOP_K, jnp.exp(val_out - jnp.max(val_out, axis=-1, keepdims=True)), 0.0)
    idx_ref[0] = idx_out.astype(I32)
    wt_ref[0] = e / jnp.sum(e, axis=-1, keepdims=True)


def _out_proj_router(a1, a2, c1, c2, w1, w2, bias, x, mod, gain, rw_hi, rw_lo, rb, tm):
    b, s, d = x.shape
    kw = w1.shape[0]
    tok = lambda i, j: (i, j, 0)
    const = lambda i, j: (0, 0)
    return pl.pallas_call(
        _out_router_kernel,
        out_shape=(jax.ShapeDtypeStruct((b, s, d), F32), jax.ShapeDtypeStruct((b, s * (d // LANES), LANES), F32),
                   jax.ShapeDtypeStruct((b, s, LANES), I32), jax.ShapeDtypeStruct((b, s, LANES), F32)),
        grid=(b, s // tm),
        in_specs=[
            pl.BlockSpec((1, tm, kw), lambda i, j: (i, j, c1)),
            pl.BlockSpec((1, tm, kw), lambda i, j: (i, j, c2)),
            pl.BlockSpec((kw, d), const),
            pl.BlockSpec((kw, d), const),
            pl.BlockSpec((1, d), const),
            pl.BlockSpec((1, tm, d), tok),
            pl.BlockSpec((1, 6, d), lambda i, j: (i, 0, 0)),
            pl.BlockSpec((1, d), const),
            pl.BlockSpec((d, LANES), const),
            pl.BlockSpec((d, LANES), const),
            pl.BlockSpec((1, LANES), const),
        ],
        out_specs=(pl.BlockSpec((1, tm, d), tok), pl.BlockSpec((1, tm * (d // LANES), LANES), tok),
                   pl.BlockSpec((1, tm, LANES), tok), pl.BlockSpec((1, tm, LANES), tok)),
        compiler_params=_cparams(("parallel", "parallel")),
        name="out_proj_router",
    )(a1, a2, w1, w2, bias, x, mod, gain, rw_hi, rw_lo, rb)


ROW_CHUNKS = D_MODEL // LANES
DEINT = 512
DUMP_ROWS = 512


def _expert_kernel(be_ref, nv_ref, gsrc0_ref, gsrc1_ref, gdstp_ref, gdst0_ref, h_ref, w1_ref, b1g_ref, b1l_ref,
                   w2_ref, b2_ref, pe_ref, po_ref, yt_ref, xbuf, ybuf, w1g_sc, w1l_sc, w2_sc, gsem, ssem,
                   *, n_blocks, dump_row):
    i = pl.program_id(0)
    cur = i % 2
    nxt = 1 - cur
    rows = MOE_BLOCK

    def tile_rows(first):
        return pl.ds(pl.multiple_of(first, ROW_CHUNKS), ROW_CHUNKS)

    def buf_rows(r):
        return pl.ds(r * ROW_CHUNKS, ROW_CHUNKS) if isinstance(r, int) else tile_rows(r * ROW_CHUNKS)

    def gather(idx_ref, slot, r):
        return pltpu.make_async_copy(h_ref.at[tile_rows(idx_ref[0, 0, r])], xbuf.at[slot, buf_rows(r)], gsem.at[slot])

    def scatter(dst_first, slot, r):
        return pltpu.make_async_copy(ybuf.at[slot, buf_rows(r)], yt_ref.at[tile_rows(dst_first)], ssem.at[slot])

    def wait_gather(slot):
        pltpu.make_async_copy(h_ref.at[pl.ds(0, rows * ROW_CHUNKS)], xbuf.at[slot], gsem.at[slot]).wait()

    def wait_scatter(slot):
        pltpu.make_async_copy(ybuf.at[slot], yt_ref.at[pl.ds(0, rows * ROW_CHUNKS)], ssem.at[slot]).wait()

    @pl.when(i == 0)
    def _():
        ybuf[...] = jnp.zeros(ybuf.shape, F32)

        def prime(r, c):
            gather(gsrc0_ref, 0, r).start()
            return c

        lax.fori_loop(0, rows, prime, 0)

    wait_gather(cur)

    @pl.when(i >= 1)
    def _():
        wait_scatter(cur)

    first = i == 0
    changed = jnp.logical_or(first, be_ref[i] != be_ref[jnp.maximum(i - 1, 0)])

    @pl.when(changed)
    def _():
        for c in range(w1_ref.shape[3] // DEINT):
            wc = w1_ref[0, 0, :, c * DEINT:(c + 1) * DEINT].astype(BF16)
            cs = slice(c * (DEINT // 2), (c + 1) * (DEINT // 2))
            w1g_sc[:, cs] = jnp.dot(wc, pe_ref[...], preferred_element_type=F32).astype(BF16)
            w1l_sc[:, cs] = jnp.dot(wc, po_ref[...], preferred_element_type=F32).astype(BF16)
        w2_sc[...] = w2_ref[0, 0].astype(BF16)

    for r in range(rows):
        gather(gsrc1_ref, nxt, r).start(priority=r % 2)
        scatter(jnp.where(first, dump_row + r * ROW_CHUNKS, gdstp_ref[0, 0, r]), nxt, r).start(priority=(r + 1) % 2)

    def chunk(c):
        return pl.ds(c, rows, stride=ROW_CHUNKS)

    x = jnp.concatenate([xbuf[cur, chunk(c), :] for c in range(ROW_CHUNKS)], axis=1)
    row = lax.broadcasted_iota(I32, x.shape, 0)
    xb = jnp.where(row < nv_ref[i], x, 0.0).astype(BF16)
    glu = jnp.dot(xb, w1g_sc[...], preferred_element_type=F32) + b1g_ref[0]
    lin = jnp.dot(xb, w1l_sc[...], preferred_element_type=F32) + b1l_ref[0]
    glu = jnp.minimum(glu, SWIGLU_LIMIT)
    lin = jnp.clip(lin, -SWIGLU_LIMIT, SWIGLU_LIMIT)
    act = glu * _sigmoid(SWIGLU_ALPHA * glu) * (lin + 1.0)
    y = jnp.dot(act.astype(BF16), w2_sc[...], preferred_element_type=F32) + b2_ref[0]
    for c in range(ROW_CHUNKS):
        ybuf[cur, chunk(c), :] = y[:, c * LANES:(c + 1) * LANES]

    @pl.when(i == n_blocks - 1)
    def _():
        def last(r, c):
            scatter(gdst0_ref[0, 0, r], cur, r).start()
            return c

        lax.fori_loop(0, rows, last, 0)
        wait_gather(nxt)
        wait_scatter(nxt)
        wait_scatter(cur)


def _experts(h, gsrc, gdst, block_expert, n_valid, layer, w1, b1g, b1l, w2, b2):
    t = h.shape[0] // ROW_CHUNKS
    n_blocks = gsrc.shape[0]
    d, f2 = w1.shape[2:]
    f = f2 // 2
    sel = np.arange(DEINT)[:, None] - 2 * np.arange(DEINT // 2)[None, :]
    p_even = jnp.asarray((sel == 0).astype(np.float32), dtype=BF16)
    p_odd = jnp.asarray((sel == 1).astype(np.float32), dtype=BF16)
    last = n_blocks - 1

    def wsel(i, be, nv):
        return (be[i], 0, 0)

    def wsel_layer(i, be, nv):
        return (layer, be[i], 0, 0)

    const = lambda i, be, nv: (0, 0)
    idx_blk = (1, 1, MOE_BLOCK)
    kern = functools.partial(_expert_kernel, n_blocks=n_blocks, dump_row=TOP_K * t * ROW_CHUNKS)
    return pl.pallas_call(
        kern,
        out_shape=jax.ShapeDtypeStruct(((TOP_K * t + DUMP_ROWS) * ROW_CHUNKS, LANES), F32),
        grid_spec=pltpu.PrefetchScalarGridSpec(
            num_scalar_prefetch=2,
            grid=(n_blocks,),
            in_specs=[
                pl.BlockSpec(idx_blk, lambda i, be, nv: (i, 0, 0), memory_space=pltpu.SMEM),
                pl.BlockSpec(idx_blk, lambda i, be, nv: (jnp.minimum(i + 1, last), 0, 0), memory_space=pltpu.SMEM),
                pl.BlockSpec(idx_blk, lambda i, be, nv: (jnp.maximum(i - 1, 0), 0, 0), memory_space=pltpu.SMEM),
                pl.BlockSpec(idx_blk, lambda i, be, nv: (i, 0, 0), memory_space=pltpu.SMEM),
                pl.BlockSpec(memory_space=pl.ANY),
                pl.BlockSpec((1, 1, d, f2), wsel_layer),
                pl.BlockSpec((1, 1, f), wsel),
                pl.BlockSpec((1, 1, f), wsel),
                pl.BlockSpec((1, 1, f, d), wsel_layer),
                pl.BlockSpec((1, 1, d), wsel),
                pl.BlockSpec((DEINT, DEINT // 2), const),
                pl.BlockSpec((DEINT, DEINT // 2), const),
            ],
            out_specs=pl.BlockSpec(memory_space=pl.ANY),
            scratch_shapes=[pltpu.VMEM((2, MOE_BLOCK * ROW_CHUNKS, LANES), F32),
                            pltpu.VMEM((2, MOE_BLOCK * ROW_CHUNKS, LANES), F32),
                            pltpu.VMEM((d, f), BF16), pltpu.VMEM((d, f), BF16), pltpu.VMEM((f, d), BF16),
                            pltpu.SemaphoreType.DMA((2,)), pltpu.SemaphoreType.DMA((2,))],
        ),
        compiler_params=_cparams(("arbitrary",)),
        name="moe_experts",
    )(block_expert, n_valid, gsrc, gsrc, gdst, gdst, h, w1, b1g, b1l, w2, b2, p_even, p_odd)


def _combine_kernel(y0_ref, y1_ref, y2_ref, y3_ref, wt_ref, x_ref, mod_ref, o_ref):
    wt = wt_ref[0]
    g2 = mod_ref[0][5:6]
    y_refs = (y0_ref, y1_ref, y2_ref, y3_ref)
    wk = [wt[:, k:k + 1] for k in range(TOP_K)]
    tm = wt.shape[0]
    for c in range(ROW_CHUNKS):
        chunk = pl.ds(c, tm, stride=ROW_CHUNKS)
        y = wk[0] * y_refs[0][chunk, :]
        for k in range(1, TOP_K):
            y = y + wk[k] * y_refs[k][chunk, :]
        cs = slice(c * LANES, (c + 1) * LANES)
        o_ref[0, :, cs] = x_ref[0, :, cs] + g2[:, cs] * y


def _combine(yt, wts, x, mod, tm):
    b, s, d = x.shape
    nt = s // tm
    tiles = b * nt
    tok = lambda i, j: (i, j, 0)

    def yspec(k):
        return pl.BlockSpec((tm * ROW_CHUNKS, LANES), lambda i, j: (k * tiles + i * nt + j, 0))

    return pl.pallas_call(
        _combine_kernel,
        out_shape=jax.ShapeDtypeStruct((b, s, d), F32),
        grid=(b, nt),
        in_specs=[yspec(k) for k in range(TOP_K)] + [
            pl.BlockSpec((1, tm, LANES), tok),
            pl.BlockSpec((1, tm, d), tok),
            pl.BlockSpec((1, 6, d), lambda i, j: (i, 0, 0)),
        ],
        out_specs=pl.BlockSpec((1, tm, d), tok),
        compiler_params=_cparams(("arbitrary", "arbitrary")),
        name="moe_combine",
    )(yt, yt, yt, yt, wts, x, mod)


def _route_slots(top_idx, n_blocks):
    t = top_idx.shape[0]
    n_assign = t * TOP_K
    n_slots = n_blocks * MOE_BLOCK
    sorted_e, order = lax.sort_key_val(top_idx.reshape(-1), jnp.arange(n_assign, dtype=I32))
    experts = jnp.arange(N_EXPERTS + 1, dtype=I32)
    start = jnp.sum((sorted_e[None, :] < experts[:, None]).astype(I32), axis=1)
    counts = start[1:] - start[:-1]
    padded = (counts + MOE_BLOCK - 1) // MOE_BLOCK * MOE_BLOCK
    pad_end = jnp.cumsum(padded)
    pad_start = pad_end - padded
    blk_start = jnp.arange(n_blocks, dtype=I32) * MOE_BLOCK
    block_expert = jnp.minimum(jnp.sum((pad_end[None, :] <= blk_start[:, None]).astype(I32), axis=1), N_EXPERTS - 1)
    n_valid = jnp.clip(pad_start[block_expert] + counts[block_expert] - blk_start, 0, MOE_BLOCK).astype(I32)
    within = jnp.arange(MOE_BLOCK, dtype=I32)[None, :]
    valid = within < n_valid[:, None]
    rank = (blk_start - pad_start[block_expert] + start[block_expert])[:, None] + within
    slot_a = order[jnp.clip(rank, 0, n_assign - 1)]
    slot = blk_start[:, None] + within
    gsrc = jnp.where(valid, slot_a // TOP_K, 0) * ROW_CHUNKS
    gdst = jnp.where(valid, (slot_a % TOP_K) * t + slot_a // TOP_K, n_assign + slot % DUMP_ROWS) * ROW_CHUNKS
    shape = (n_blocks, 1, MOE_BLOCK)
    return gsrc.reshape(shape), gdst.reshape(shape), block_expert, n_valid


def _moe(x_mid, h, idx, wts, mod, layer, w1, b1, w2, b2, tm):
    b, s, d = x_mid.shape
    t = b * s
    n_blocks = -(-t * TOP_K // MOE_BLOCK) + N_EXPERTS
    top_idx = idx.reshape(t, LANES)[:, :TOP_K]
    gsrc, gdst, block_expert, n_valid = _route_slots(top_idx, n_blocks)
    yt = _experts(h.reshape(t * ROW_CHUNKS, LANES), gsrc, gdst, block_expert, n_valid, layer, w1,
                  b1[:, None, 0::2], b1[:, None, 1::2], w2, b2[:, None, :])
    return _combine(yt, wts, x_mid, mod, tm)


def _pad_cols(w, n):
    return jnp.pad(w, ((0, 0), (0, n - w.shape[-1])))


def _attn_colgain(diff_qk_gain, nsa_q_gain, nsa_k_gain):
    scale = HEAD_DIM ** -0.5 * LOG2E
    ones = jnp.ones((LANES,), F32)
    parts = [jnp.tile(diff_qk_gain[0] * scale, 8), jnp.tile(diff_qk_gain[1], 8), jnp.ones((512,), F32),
             jnp.tile(nsa_q_gain * scale, 8), ones, ones, jnp.tile(nsa_k_gain[1], 2), ones,
             jnp.tile(nsa_k_gain[2], 2), ones, ones]
    return jnp.concatenate(parts).reshape(1, W_IN_PAD)


def _overlap_matrix(s):
    n_sel = s // SEL_BLOCK
    ncp = s // CMP_STRIDE
    c_start = np.arange(ncp) * CMP_STRIDE
    s_start = np.arange(n_sel) * SEL_BLOCK
    ovl = (c_start[None, :] < s_start[:, None] + SEL_BLOCK) & (c_start[None, :] + CMP_BLOCK > s_start[:, None])
    ovl[:, ncp - 1] = False
    return jnp.asarray(ovl.astype(np.float32), dtype=BF16)


def _attention_layer(x, mod, gain, w_in, w_out, diff_qk_gain, diff_lambda, diff_subln, nsa_q_gain, nsa_k_gain,
                     cmp_pos, cmp_w1, cmp_w2, lam_init):
    b, s, d = x.shape
    n_gate = NSA_Q_HEADS * 3
    gate_perm = np.arange(n_gate).reshape(NSA_KV_HEADS, NSA_GROUP, 3).transpose(0, 2, 1).reshape(-1)
    w_pad = jnp.concatenate([w_in[:, :COL_GATE], w_in[:, COL_GATE + gate_perm],
                             jnp.zeros((d, W_IN_PAD - COL_GATE - n_gate), w_in.dtype)], axis=1).astype(BF16)
    zb, dqT, dvT, nqT, vselT, vwinT, gT = _attn_in_proj(
        x, mod, gain, w_pad, _attn_colgain(diff_qk_gain, nsa_q_gain, nsa_k_gain))

    o_diff = _diff_attention(zb, dqT.reshape(b, DIFF_HEADS, 2, HEAD_DIM, s), dvT,
                             _score_bound(diff_qk_gain[0], diff_qk_gain[1]), diff_lambda, diff_subln, lam_init,
                             DIFF_TQ, ATTN_TM)

    ncp = s // CMP_STRIDE
    chunks = zb[:, :, KZ_KCMP:KZ_KCMP + 2 * LANES].reshape(b, ncp, CMP_STRIDE, 2, NSA_KV_HEADS, HEAD_DIM)
    chunks = chunks.transpose(0, 3, 4, 1, 2, 5).reshape(b, 4, ncp, CMP_STRIDE * HEAD_DIM)
    cmp_out = _compress(chunks, cmp_pos.reshape(2, 1, CMP_BLOCK * HEAD_DIM), cmp_w1, cmp_w2, nsa_k_gain[0:1])
    kc = cmp_out[:, 0:2].transpose(0, 2, 1, 3).reshape(b, ncp, LANES).astype(BF16)
    vcT = cmp_out[:, 2:4].transpose(0, 1, 3, 2).astype(BF16)
    gatesT = gT[:, :n_gate].reshape(b, NSA_KV_HEADS, 3, NSA_GROUP, s)
    shifts = jnp.stack([_score_bound(nsa_q_gain, nsa_k_gain[1]), _score_bound(nsa_q_gain, nsa_k_gain[2])])
    o_nsa = _nsa_attention(zb, nqT.reshape(b, NSA_KV_HEADS, NSA_GROUP, HEAD_DIM, s), vselT, vwinT, kc, vcT,
                           _overlap_matrix(s), gatesT, shifts, NSA_TILE)
    return o_diff, o_nsa


def kernel(x, c, mod_w, mod_b, norm_mix, norm_ffn, attn_w_in, attn_w_out, diff_qk_gain, diff_lambda, diff_subln,
           nsa_q_gain, nsa_k_gain, nsa_cmp_pos, nsa_cmp_w1, nsa_cmp_w2, conv_pw1_w, conv_pw1_b, conv_dw_w,
           conv_dw_b, conv_ln_g, conv_ln_b, conv_pw2_w, conv_pw2_b, router_w, router_b, moe_w1, moe_b1, moe_w2,
           moe_b2):
    b, s, d = x.shape
    depth = mod_w.shape[0]
    tm = 512 if s % 512 == 0 else 256
    mods = _modulation(c, mod_w, mod_b).reshape(depth, b, 6, d)
    half = d // 2
    for i in range(depth):
        mod = mods[i]
        j = i // 2
        if i % 2 == 0:
            lam_init = 0.8 - 0.6 * math.exp(-0.3 * i)
            o_diff, o_nsa = _attention_layer(
                x, mod, norm_mix[i:i + 1], attn_w_in[j], attn_w_out[j], diff_qk_gain[j], diff_lambda[j],
                diff_subln[j], nsa_q_gain[j], nsa_k_gain[j], nsa_cmp_pos[j], nsa_cmp_w1[j], nsa_cmp_w2[j], lam_init)
            a1, a2, c1, c2 = o_diff, o_nsa, 0, 0
            w_o = attn_w_out[j].astype(BF16)
            bias = jnp.zeros((1, d), F32)
        else:
            u = _conv_in_proj(x, mod, norm_mix[i:i + 1], conv_pw1_w[j].astype(BF16), conv_pw1_b[j:j + 1], tm)
            v = _dwconv_ln_swish(u, conv_dw_w[j].reshape(CONV_WIDTH, d), conv_dw_b[j:j + 1], conv_ln_g[j:j + 1],
                                 conv_ln_b[j:j + 1], 256)
            a1, a2, c1, c2 = v, v, 0, 1
            w_o = conv_pw2_w[j].astype(BF16)
            bias = conv_pw2_b[j:j + 1]
        rw = _pad_cols(router_w[i], LANES)
        rw_hi = rw.astype(BF16)
        rw_lo = (rw - rw_hi.astype(F32)).astype(BF16)
        rb = jnp.concatenate([router_b[i], jnp.full((LANES - N_EXPERTS,), NEG, F32)]).reshape(1, LANES)
        x_mid, h, idx, wts = _out_proj_router(a1, a2, c1, c2, w_o[:half], w_o[half:], bias, x, mod,
                                              norm_ffn[i:i + 1], rw_hi, rw_lo, rb, tm)
        x = _moe(x_mid, h, idx, wts, mod, i, moe_w1, moe_b1[i], moe_w2, moe_b2[i], 256)
    return x
```

```python
import functools
import math

import jax
import jax.numpy as jnp
import numpy as np
from jax import lax
from jax.experimental import pallas as pl
from jax.experimental.pallas import tpu as pltpu

F32 = jnp.float32
BF16 = jnp.bfloat16
I32 = jnp.int32

D_MODEL = 1024
HEAD_DIM = 64
DIFF_HEADS = 4
NSA_Q_HEADS = 8
NSA_KV_HEADS = 2
NSA_GROUP = NSA_Q_HEADS // NSA_KV_HEADS
CMP_BLOCK = 32
CMP_STRIDE = 16
CMP_HIDDEN = 256
SEL_BLOCK = 64
SEL_TOP = 16
SEL_FORCED_SCORE = 1.0e4
WINDOW = 512
CONV_WIDTH = 31
N_EXPERTS = 32
TOP_K = 4
SWIGLU_ALPHA = 1.702
SWIGLU_LIMIT = 7.0
MOE_BLOCK = 256
NORM_EPS = 1e-6

LANES = 128
NEG = -1e30
TAKEN = -3e38
VMEM_LIMIT = 56 * 1024 * 1024

COL_DQ, COL_DK, COL_DV, COL_NQ = 0, 512, 1024, 1536
COL_KCMP, COL_VCMP, COL_KSEL, COL_VSEL, COL_KWIN, COL_VWIN = 2048, 2176, 2304, 2432, 2560, 2688
COL_GATE = 2816
W_IN_PAD = 2944
NORM_SLABS = frozenset(list(range(0, 8)) + list(range(12, 16)) + [COL_KSEL // LANES, COL_KWIN // LANES])


def _cparams(sem):
    return pltpu.CompilerParams(dimension_semantics=sem, vmem_limit_bytes=VMEM_LIMIT)


def _sigmoid(v):
    return 1.0 / (1.0 + jnp.exp(-v))


def _norm_mod(x, g, sc, sh):
    ms = jnp.mean(x * x, axis=-1, keepdims=True)
    return (x * lax.rsqrt(ms + NORM_EPS)) * g * (1.0 + sc) + sh


def _mod_kernel(c_ref, w_ref, b_ref, o_ref):
    c = c_ref[...]
    cond = c * _sigmoid(c)
    o_ref[0] = jnp.dot(cond, w_ref[0], preferred_element_type=F32, precision=lax.Precision.HIGHEST) + b_ref[0]


def _modulation(c, mod_w, mod_b):
    depth, d, n = mod_w.shape
    b = c.shape[0]
    tn = 1536
    return pl.pallas_call(
        _mod_kernel,
        out_shape=jax.ShapeDtypeStruct((depth, b, n), F32),
        grid=(depth, n // tn),
        in_specs=[
            pl.BlockSpec((b, d), lambda i, j: (0, 0)),
            pl.BlockSpec((1, d, tn), lambda i, j: (i, 0, j)),
            pl.BlockSpec((1, 1, tn), lambda i, j: (i, 0, j)),
        ],
        out_specs=pl.BlockSpec((1, b, tn), lambda i, j: (i, 0, j)),
        compiler_params=_cparams(("parallel", "parallel")),
        name="modulation",
    )(c, mod_w, mod_b.reshape(depth, 1, n))


KZ_DK, KZ_KSEL, KZ_KWIN, KZ_WIDTH = 0, 512, 640, 768
KEY_COLS = {COL_DK + 128 * u: KZ_DK + 128 * u for u in range(4)}
KEY_COLS.update({COL_KSEL: KZ_KSEL, COL_KWIN: KZ_KWIN})
ATTN_TM = 512
DIFF_TQ = 1024
NSA_TILE = 512
GATE_ROWS = 32


def _in_attn_kernel(x_ref, mod_ref, g_ref, w_ref, cg_ref, z_ref, dqT_ref, dvT_ref, nqT_ref, vsT_ref, vwT_ref, gT_ref,
                    ch_ref, cmp_sc):
    m = mod_ref[0]
    h = _norm_mod(x_ref[0], g_ref[...], m[1:2], m[0:1]).astype(BF16)
    lo = lax.broadcasted_iota(I32, (1, LANES), 1) < HEAD_DIM
    groups = ((0, 512), (512, 1024), (1024, 1536), (1536, 2048), (2048, 2816), (2816, 2944))
    halves = ATTN_TM // NSA_TILE
    for c0, c1 in groups:
        z = jnp.dot(h, w_ref[:, c0:c1], preferred_element_type=F32)
        for s in range((c1 - c0) // LANES):
            a0 = c0 + s * LANES
            zs = z[:, s * LANES:(s + 1) * LANES]
            if a0 == COL_GATE:
                gT_ref[0] = _sigmoid(zs).T[:GATE_ROWS]
                continue
            if a0 // LANES in NORM_SLABS:
                z2 = zs * zs
                s_lo = jnp.sum(jnp.where(lo, z2, 0.0), axis=-1, keepdims=True)
                s_hi = jnp.sum(jnp.where(lo, 0.0, z2), axis=-1, keepdims=True)
                inv = jnp.where(lo, lax.rsqrt(s_lo * (1.0 / HEAD_DIM) + NORM_EPS),
                                lax.rsqrt(s_hi * (1.0 / HEAD_DIM) + NORM_EPS))
                zs = zs * inv
            zs = zs * cg_ref[:, a0:a0 + LANES]
            if COL_DQ <= a0 < COL_DK:
                dqT_ref[0, a0 - COL_DQ:a0 - COL_DQ + LANES, :] = zs.T.astype(BF16)
            elif COL_DV <= a0 < COL_NQ:
                dvT_ref[0, 0, a0 - COL_DV:a0 - COL_DV + LANES, :] = zs.T.astype(BF16)
            elif COL_NQ <= a0 < COL_KCMP:
                nqT_ref[0, a0 - COL_NQ:a0 - COL_NQ + LANES, :] = zs.T.astype(BF16)
            elif a0 in (COL_VSEL, COL_VWIN):
                zt = zs.T.astype(BF16)
                ref = vsT_ref if a0 == COL_VSEL else vwT_ref
                for u in range(halves):
                    ref[0, u] = zt[:, u * NSA_TILE:(u + 1) * NSA_TILE]
            elif a0 in (COL_KCMP, COL_VCMP):
                cmp_sc[...] = zs
                first = 0 if a0 == COL_KCMP else NSA_KV_HEADS
                n_rows = zs.shape[0] // CMP_STRIDE
                for pair in range(CMP_STRIDE // 2):
                    even = cmp_sc[pl.ds(2 * pair, n_rows, stride=CMP_STRIDE), :]
                    odd = cmp_sc[pl.ds(2 * pair + 1, n_rows, stride=CMP_STRIDE), :]
                    cs = slice(pair * LANES, (pair + 1) * LANES)
                    ch_ref[0, first, :, cs] = jnp.where(lo, even, pltpu.roll(odd, HEAD_DIM, 1)).astype(BF16)
                    ch_ref[0, first + 1, :, cs] = jnp.where(lo, pltpu.roll(even, HEAD_DIM, 1), odd).astype(BF16)
            else:
                k0 = KEY_COLS[a0]
                z_ref[0, :, k0:k0 + LANES] = zs.astype(BF16)


def _attn_in_proj(x, mod, gain, w_pad, colgain):
    b, s, d = x.shape
    tm = ATTN_TM
    halves = tm // NSA_TILE
    tok = lambda i, j: (i, j, 0)
    featT = lambda i, j: (i, 0, j)
    return pl.pallas_call(
        _in_attn_kernel,
        out_shape=(jax.ShapeDtypeStruct((b, s, KZ_WIDTH), BF16),
                   jax.ShapeDtypeStruct((b, 512, s), BF16),
                   jax.ShapeDtypeStruct((b, s // tm, 512, tm), BF16),
                   jax.ShapeDtypeStruct((b, 512, s), BF16),
                   jax.ShapeDtypeStruct((b, s // NSA_TILE, LANES, NSA_TILE), BF16),
                   jax.ShapeDtypeStruct((b, s // NSA_TILE, LANES, NSA_TILE), BF16),
                   jax.ShapeDtypeStruct((b, GATE_ROWS, s), F32),
                   jax.ShapeDtypeStruct((b, 2 * NSA_KV_HEADS, s // CMP_STRIDE, CMP_STRIDE * HEAD_DIM), BF16)),
        grid=(b, s // tm),
        in_specs=[
            pl.BlockSpec((1, tm, d), tok),
            pl.BlockSpec((1, 6, d), lambda i, j: (i, 0, 0)),
            pl.BlockSpec((1, d), lambda i, j: (0, 0)),
            pl.BlockSpec((d, W_IN_PAD), lambda i, j: (0, 0)),
            pl.BlockSpec((1, W_IN_PAD), lambda i, j: (0, 0)),
        ],
        out_specs=(pl.BlockSpec((1, tm, KZ_WIDTH), tok),
                   pl.BlockSpec((1, 512, tm), featT),
                   pl.BlockSpec((1, 1, 512, tm), lambda i, j: (i, j, 0, 0)),
                   pl.BlockSpec((1, 512, tm), featT),
                   pl.BlockSpec((1, halves, LANES, NSA_TILE), lambda i, j: (i, j, 0, 0)),
                   pl.BlockSpec((1, halves, LANES, NSA_TILE), lambda i, j: (i, j, 0, 0)),
                   pl.BlockSpec((1, GATE_ROWS, tm), featT),
                   pl.BlockSpec((1, 2 * NSA_KV_HEADS, tm // CMP_STRIDE, CMP_STRIDE * HEAD_DIM),
                                lambda i, j: (i, 0, j, 0))),
        scratch_shapes=[pltpu.VMEM((tm, LANES), F32)],
        compiler_params=_cparams(("parallel", "parallel")),
        name="attn_in_proj",
    )(x, mod, gain, w_pad, colgain)


FIXED_SHIFT_LIMIT = 57.0
FIXED_TILES_PER_STEP = 4
LOG2E = math.log2(math.e)


def _score_bound(gain_q, gain_k):
    return (math.sqrt(HEAD_DIM) * LOG2E) * jnp.max(jnp.abs(gain_q * gain_k))


SUM_ROWS = 8


def _with_ones(vt):
    return jnp.concatenate([vt, jnp.ones((SUM_ROWS, vt.shape[1]), vt.dtype)], axis=0)


def _fixed_update(ss, vts, shift, acc_ref):
    ps = [jnp.exp2(s - shift).astype(BF16) for s in ss]
    acc_ref[...] += sum(jnp.dot(_with_ones(vt), p, preferred_element_type=F32) for vt, p in zip(vts, ps))


def _online_update(ss, vts, m_ref, acc_ref):
    for s, vt in zip(ss, vts):
        m_old = m_ref[...]
        m_new = jnp.maximum(m_old, jnp.max(s, axis=0, keepdims=True))
        alpha = jnp.exp2(m_old - m_new)
        p = jnp.exp2(s - m_new).astype(BF16)
        acc_ref[...] = alpha * acc_ref[...] + jnp.dot(_with_ones(vt), p, preferred_element_type=F32)
        m_ref[...] = m_new


def _tile_loops(n_full, tiles, width, n_masked=1):
    def group(jj, c):
        tiles(tuple(width * jj + u for u in range(width)), (False,) * width)
        return c

    lax.fori_loop(0, n_full // width, group, 0)
    rem = n_full % width
    base = n_full - rem
    for r in range(width):
        @pl.when(rem == r)
        def _():
            tiles(tuple(base + u for u in range(r + n_masked)), (False,) * r + (True,) * n_masked)


def _diff_kernel(mb_ref, lam_ref, qT_ref, k_ref, vT_ref, sub_ref, o_ref, m0, m1, a0, a1, *, tq, tk, lam_init):
    i = pl.program_id(2)
    m_refs, acc_refs = (m0, m1), (a0, a1)
    q = qT_ref[0, 0]
    zero = jnp.zeros((HEAD_DIM, tq), BF16)
    qp = (jnp.concatenate([q[0], zero], axis=0), jnp.concatenate([zero, q[1]], axis=0))
    for mm in range(2):
        m_refs[mm][...] = jnp.full(m_refs[mm].shape, NEG, F32)
        acc_refs[mm][...] = jnp.zeros(acc_refs[mm].shape, F32)
    shift = mb_ref[0]
    dv = 2 * HEAD_DIM

    def tiles(js, masks, fixed):
        kts = [k_ref[0, pl.ds(pl.multiple_of(j * tk, tk), tk), :] for j in js]
        vts = [vT_ref[0, j] for j in js]
        for mm in range(2):
            ss = []
            for j, kt, masked in zip(js, kts, masks):
                s = jnp.dot(kt, qp[mm], preferred_element_type=F32)
                if masked:
                    kpos = j * tk + lax.broadcasted_iota(I32, (tk, tq), 0)
                    t = i * tq + lax.broadcasted_iota(I32, (tk, tq), 1)
                    s = jnp.where(kpos <= t, s, NEG)
                ss.append(s)
            if fixed:
                _fixed_update(ss, vts, shift, acc_refs[mm])
            else:
                _online_update(ss, vts, m_refs[mm], acc_refs[mm])

    n_full = (i * tq) // tk
    n_masked = max(tq // tk, 1)

    @pl.when(shift <= FIXED_SHIFT_LIMIT)
    def _():
        _tile_loops(n_full, functools.partial(tiles, fixed=True), FIXED_TILES_PER_STEP, n_masked)

    @pl.when(shift > FIXED_SHIFT_LIMIT)
    def _():
        _tile_loops(n_full, functools.partial(tiles, fixed=False), 1, n_masked)

    lv = lam_ref[...]
    lam = (jnp.exp(jnp.sum(lv[0:1] * lv[1:2], axis=-1, keepdims=True))
           - jnp.exp(jnp.sum(lv[2:3] * lv[3:4], axis=-1, keepdims=True)) + lam_init)
    o = a0[:dv] / a0[dv:dv + 1] - lam * (a1[:dv] / a1[dv:dv + 1])
    ms = jnp.mean(o * o, axis=0, keepdims=True)
    o = o * lax.rsqrt(ms + NORM_EPS) * sub_ref[...] * (1.0 - lam_init)
    o_ref[0] = o.T.astype(BF16)


def _diff_attention(zb, qT, vT, shift, lam_vecs, subln, lam_init, tq, tk):
    b, s, _ = zb.shape
    kern = functools.partial(_diff_kernel, tq=tq, tk=tk, lam_init=lam_init)
    dv = 2 * HEAD_DIM
    return pl.pallas_call(
        kern,
        out_shape=jax.ShapeDtypeStruct((b, s, DIFF_HEADS * dv), BF16),
        grid=(b, DIFF_HEADS, s // tq),
        in_specs=[
            pl.BlockSpec(memory_space=pltpu.SMEM),
            pl.BlockSpec((4, HEAD_DIM), lambda bi, h, i: (0, 0)),
            pl.BlockSpec((1, 1, 2, HEAD_DIM, tq), lambda bi, h, i: (bi, h, 0, 0, i)),
            pl.BlockSpec((1, s, LANES), lambda bi, h, i: (bi, 0, KZ_DK // LANES + h)),
            pl.BlockSpec((1, s // tk, 2 * HEAD_DIM, tk), lambda bi, h, i: (bi, 0, h, 0)),
            pl.BlockSpec((2 * HEAD_DIM, 1), lambda bi, h, i: (0, 0)),
        ],
        out_specs=pl.BlockSpec((1, tq, LANES), lambda bi, h, i: (bi, i, h)),
        scratch_shapes=[pltpu.VMEM((1, tq), F32)] * 2 + [pltpu.VMEM((dv + SUM_ROWS, tq), F32)] * 2,
        compiler_params=_cparams(("parallel", "parallel", "arbitrary")),
        name="diff_attention",
    )(shift.reshape(1), lam_vecs, qT, zb, vT, subln.reshape(-1, 1))


def _gelu_tanh(v):
    return 0.5 * v * (1.0 + jnp.tanh(math.sqrt(2.0 / math.pi) * (v + 0.044715 * (v * v * v))))


def _cmp_kernel(c_ref, pos_ref, w1_ref, w2_ref, kg_ref, o_ref, *, n_cmp):
    jh = pl.program_id(1)
    half = CMP_STRIDE * HEAD_DIM
    c = c_ref[0, 0]
    w1a = w1_ref[0, :half, :].astype(BF16)
    w1b = w1_ref[0, half:, :].astype(BF16)
    pos = jnp.broadcast_to(pos_ref[0], (8, 2 * half)).astype(BF16)
    bias = (jnp.dot(pos[:, :half], w1a, preferred_element_type=F32)
            + jnp.dot(pos[:, half:], w1b, preferred_element_type=F32))[0:1]
    u = jnp.dot(c, w1a, preferred_element_type=F32)
    v = jnp.dot(c, w1b, preferred_element_type=F32)
    ncp = u.shape[0]
    hid = _gelu_tanh(u + pltpu.roll(v, ncp - 1, 0) + bias)
    y = jnp.dot(hid.astype(BF16), w2_ref[0].astype(BF16), preferred_element_type=F32)
    yn = y * lax.rsqrt(jnp.mean(y * y, axis=-1, keepdims=True) + NORM_EPS) * kg_ref[...]
    y = jnp.where(jh < NSA_KV_HEADS, yn, y)
    row = lax.broadcasted_iota(I32, y.shape, 0)
    o_ref[0, 0] = jnp.where(row < n_cmp, y, 0.0)


def _compress(chunks, pos, w1, w2, kgain):
    b, _, ncp, cd = chunks.shape
    kern = functools.partial(_cmp_kernel, n_cmp=ncp - 1)
    return pl.pallas_call(
        kern,
        out_shape=jax.ShapeDtypeStruct((b, 4, ncp, HEAD_DIM), F32),
        grid=(b, 4),
        in_specs=[
            pl.BlockSpec((1, 1, ncp, cd), lambda bi, j: (bi, j, 0, 0)),
            pl.BlockSpec((1, 1, 2 * cd), lambda bi, j: (j // NSA_KV_HEADS, 0, 0)),
            pl.BlockSpec((1, 2 * cd, CMP_HIDDEN), lambda bi, j: (j // NSA_KV_HEADS, 0, 0)),
            pl.BlockSpec((1, CMP_HIDDEN, HEAD_DIM), lambda bi, j: (j // NSA_KV_HEADS, 0, 0)),
            pl.BlockSpec((1, HEAD_DIM), lambda bi, j: (0, 0)),
        ],
        out_specs=pl.BlockSpec((1, 1, ncp, HEAD_DIM), lambda bi, j: (bi, j, 0, 0)),
        compiler_params=_cparams(("parallel", "parallel")),
        name="nsa_compress",
    )(chunks, pos, w1, w2, kgain)


def _nsa_kernel(mb_ref, qT_ref, ksel_ref, vselT_ref, kwin_ref, vwinT_ref, kc_ref, vcT_ref, ovl_ref, gate_ref, o_ref,
                selb_sc, m_s, m_w, a_s, a_w, *, tq, n_top):
    hk = pl.program_id(1)
    i = pl.program_id(2)
    g4 = NSA_GROUP
    nq = g4 * tq
    q4 = jnp.concatenate([qT_ref[0, 0, g] for g in range(g4)], axis=1)
    half = lax.broadcasted_iota(I32, (2 * HEAD_DIM, nq), 0) // HEAD_DIM
    qp = jnp.where(half == hk, jnp.concatenate([q4, q4], axis=0), jnp.zeros((), BF16))
    t1 = i * tq + lax.broadcasted_iota(I32, (1, tq), 1)

    def tile4(a):
        return jnp.concatenate([a] * g4, axis=1)

    ncp = kc_ref.shape[1]
    sc = jnp.dot(kc_ref[0], qp, preferred_element_type=F32)
    cend = CMP_STRIDE * lax.broadcasted_iota(I32, (ncp, tq), 0) + (CMP_BLOCK - 1)
    sc = sc + tile4(jnp.where(cend <= t1, 0.0, NEG))
    e = jnp.where(sc > 0.5 * NEG, jnp.exp2(sc - jnp.max(sc, axis=0, keepdims=True)), 0.0)
    p = e / jnp.maximum(jnp.sum(e, axis=0, keepdims=True), 1e-30)
    o_c = jnp.dot(vcT_ref[0, 0], p.astype(BF16), preferred_element_type=F32)

    psum = p[:, 0:tq]
    for g in range(1, g4):
        psum = psum + p[:, g * tq:(g + 1) * tq]
    p_hi = psum.astype(BF16)
    p_lo = (psum - p_hi.astype(F32)).astype(BF16)
    imp = (jnp.dot(ovl_ref[...], p_hi, preferred_element_type=F32)
           + jnp.dot(ovl_ref[...], p_lo, preferred_element_type=F32))
    n_sel = imp.shape[0]
    jrow = lax.broadcasted_iota(I32, (n_sel, tq), 0)
    cur = t1 // SEL_BLOCK
    forced = (jrow == 0) | (jrow == cur) | (jrow == cur - 1)
    score = jnp.where(forced, SEL_FORCED_SCORE, jnp.where(jrow <= cur, imp, NEG))
    selb = jnp.full((n_sel, tq), NEG, F32)
    jrow_f = jrow.astype(F32)
    for _ in range(n_top):
        best = jnp.max(score, axis=0, keepdims=True)
        pick = jnp.min(jnp.where(score == best, jrow_f, float(n_sel)), axis=0, keepdims=True)
        hit = jrow_f == pick
        selb = jnp.where(hit, 0.0, selb)
        score = jnp.where(hit, TAKEN, score)
    selb_sc[...] = selb

    for ref in (m_s, m_w):
        ref[...] = jnp.full(ref.shape, NEG, F32)
    for ref in (a_s, a_w):
        ref[...] = jnp.zeros(ref.shape, F32)
    tk = tq
    bpt = tk // SEL_BLOCK
    krow = lax.broadcasted_iota(I32, (tk, tq), 0)
    shift_s, shift_w = mb_ref[0], mb_ref[1]
    fixed = jnp.maximum(shift_s, shift_w) <= FIXED_SHIFT_LIMIT

    def sel_tiles(js, masks, fixed):
        ss = []
        for j, masked in zip(js, masks):
            kt = ksel_ref[0, pl.ds(pl.multiple_of(j * tk, tk), tk), :]
            rows = [jnp.broadcast_to(selb_sc[pl.ds(j * bpt + r, 1), :], (SEL_BLOCK, tq)) for r in range(bpt)]
            bias = jnp.concatenate(rows, axis=0)
            if masked:
                bias = jnp.where(j * tk + krow <= t1, bias, NEG)
            ss.append(jnp.dot(kt, qp, preferred_element_type=F32) + tile4(bias))
        vts = [vselT_ref[0, j] for j in js]
        if fixed:
            _fixed_update(ss, vts, shift_s, a_s)
        else:
            _online_update(ss, vts, m_s, a_s)

    def win_tiles(js, fixed):
        ss = []
        for j in js:
            kt = kwin_ref[0, pl.ds(pl.multiple_of(j * tk, tk), tk), :]
            kpos = j * tk + krow
            bias = jnp.where(kpos <= t1, jnp.where(kpos > t1 - WINDOW, 0.0, NEG), NEG)
            ss.append(jnp.dot(kt, qp, preferred_element_type=F32) + tile4(bias))
        vts = [vwinT_ref[0, j] for j in js]
        if fixed:
            _fixed_update(ss, vts, shift_w, a_w)
        else:
            _online_update(ss, vts, m_w, a_w)

    max_win = (WINDOW + tk - 1) // tk + 1
    n_win = jnp.minimum(i + 1, max_win)
    for use_fixed in (True, False):
        @pl.when(fixed == use_fixed)
        def _():
            _tile_loops(i, functools.partial(sel_tiles, fixed=use_fixed), FIXED_TILES_PER_STEP if use_fixed else 1)
            for c in range(1, max_win + 1):
                @pl.when(n_win == c)
                def _():
                    win_tiles(tuple(i - (c - 1) + u for u in range(c)), use_fixed)

    def gate_row(br):
        gt = gate_ref[0, 0, br]
        return jnp.concatenate([gt[g:g + 1] for g in range(g4)], axis=1)

    dv = HEAD_DIM
    out = (o_c * gate_row(0) + (a_s[:dv] / a_s[dv:dv + 1]) * gate_row(1) + (a_w[:dv] / a_w[dv:dv + 1]) * gate_row(2))
    stacked = jnp.concatenate([out[:, g * tq:(g + 1) * tq] for g in range(g4)], axis=0)
    o_ref[0] = stacked.T.astype(BF16)


def _nsa_attention(zb, qT, vselT, vwinT, kc, vcT, ovl, gatesT, shifts, tq):
    b, s, _ = zb.shape
    nt = s // tq
    n_sel = s // SEL_BLOCK
    ncp = kc.shape[1]
    kern = functools.partial(_nsa_kernel, tq=tq, n_top=min(SEL_TOP, n_sel))
    gd = NSA_GROUP * HEAD_DIM
    nq = NSA_GROUP * tq
    return pl.pallas_call(
        kern,
        out_shape=jax.ShapeDtypeStruct((b, s, NSA_Q_HEADS * HEAD_DIM), BF16),
        grid=(b, NSA_KV_HEADS, nt),
        in_specs=[
            pl.BlockSpec(memory_space=pltpu.SMEM),
            pl.BlockSpec((1, 1, NSA_GROUP, HEAD_DIM, tq), lambda bi, h, i: (bi, h, 0, 0, i)),
            pl.BlockSpec((1, s, LANES), lambda bi, h, i: (bi, 0, KZ_KSEL // LANES)),
            pl.BlockSpec((1, nt, HEAD_DIM, tq), lambda bi, h, i: (bi, 0, h, 0)),
            pl.BlockSpec((1, s, LANES), lambda bi, h, i: (bi, 0, KZ_KWIN // LANES)),
            pl.BlockSpec((1, nt, HEAD_DIM, tq), lambda bi, h, i: (bi, 0, h, 0)),
            pl.BlockSpec((1, ncp, LANES), lambda bi, h, i: (bi, 0, 0)),
            pl.BlockSpec((1, 1, HEAD_DIM, ncp), lambda bi, h, i: (bi, h, 0, 0)),
            pl.BlockSpec((n_sel, ncp), lambda bi, h, i: (0, 0)),
            pl.BlockSpec((1, 1, 3, NSA_GROUP, tq), lambda bi, h, i: (bi, h, 0, 0, i)),
        ],
        out_specs=pl.BlockSpec((1, tq, gd), lambda bi, h, i: (bi, i, h)),
        scratch_shapes=([pltpu.VMEM((n_sel, tq), F32)] + [pltpu.VMEM((1, nq), F32)] * 2
                        + [pltpu.VMEM((HEAD_DIM + SUM_ROWS, nq), F32)] * 2),
        compiler_params=_cparams(("parallel", "parallel", "arbitrary")),
        name="nsa_attention",
    )(shifts, qT, zb, vselT, zb, vwinT, kc, vcT, ovl, gatesT)


def _in_conv_kernel(x_ref, mod_ref, g_ref, w_ref, b_ref, u_ref):
    m = mod_ref[0]
    d = x_ref.shape[-1]
    h = _norm_mod(x_ref[0], g_ref[...], m[1:2], m[0:1]).astype(BF16)
    a = jnp.dot(h, w_ref[:, :d], preferred_element_type=F32) + b_ref[:, :d]
    g = jnp.dot(h, w_ref[:, d:], preferred_element_type=F32) + b_ref[:, d:]
    u_ref[0] = a * _sigmoid(g)


def _conv_in_proj(x, mod, gain, w, bias, tm):
    b, s, d = x.shape
    return pl.pallas_call(
        _in_conv_kernel,
        out_shape=jax.ShapeDtypeStruct((b, s, d), F32),
        grid=(b, s // tm),
        in_specs=[
            pl.BlockSpec((1, tm, d), lambda i, j: (i, j, 0)),
            pl.BlockSpec((1, 6, d), lambda i, j: (i, 0, 0)),
            pl.BlockSpec((1, d), lambda i, j: (0, 0)),
            pl.BlockSpec((d, 2 * d), lambda i, j: (0, 0)),
            pl.BlockSpec((1, 2 * d), lambda i, j: (0, 0)),
        ],
        out_specs=pl.BlockSpec((1, tm, d), lambda i, j: (i, j, 0)),
        compiler_params=_cparams(("parallel", "parallel")),
        name="conv_in_proj",
    )(x, mod, gain, w, bias)


HALO = 32


def _dwconv_kernel(u_ref, halo_ref, w_ref, b_ref, lg_ref, lb_ref, o_ref, buf_sc, acc_sc, *, tm):
    i = pl.program_id(1)
    d = u_ref.shape[-1]
    buf_sc[0:HALO, :] = jnp.where(i > 0, halo_ref[0], 0.0)
    buf_sc[HALO:, :] = u_ref[0]
    off = HALO - (CONV_WIDTH - 1)
    cw = 256
    for c in range(d // cw):
        cs = slice(c * cw, (c + 1) * cw)
        acc = jnp.zeros((tm, cw), F32) + b_ref[:, cs]
        for j in range(CONV_WIDTH):
            acc = acc + w_ref[j:j + 1, cs] * buf_sc[off + j:off + j + tm, cs]
        acc_sc[:, cs] = acc
    y = acc_sc[...]
    mu = jnp.mean(y, axis=-1, keepdims=True)
    yc = y - mu
    var = jnp.mean(yc * yc, axis=-1, keepdims=True)
    yn = yc * lax.rsqrt(var + NORM_EPS) * lg_ref[...] + lb_ref[...]
    o_ref[0] = (yn * _sigmoid(yn)).astype(BF16)


def _dwconv_ln_swish(u, dw_w, dw_b, ln_g, ln_b, tm):
    b, s, d = u.shape
    kern = functools.partial(_dwconv_kernel, tm=tm)
    hb = tm // HALO
    return pl.pallas_call(
        kern,
        out_shape=jax.ShapeDtypeStruct((b, s, d), BF16),
        grid=(b, s // tm),
        in_specs=[
            pl.BlockSpec((1, tm, d), lambda bi, i: (bi, i, 0)),
            pl.BlockSpec((1, HALO, d), lambda bi, i: (bi, jnp.maximum(i * hb - 1, 0), 0)),
            pl.BlockSpec((CONV_WIDTH, d), lambda bi, i: (0, 0)),
            pl.BlockSpec((1, d), lambda bi, i: (0, 0)),
            pl.BlockSpec((1, d), lambda bi, i: (0, 0)),
            pl.BlockSpec((1, d), lambda bi, i: (0, 0)),
        ],
        out_specs=pl.BlockSpec((1, tm, d), lambda bi, i: (bi, i, 0)),
        scratch_shapes=[pltpu.VMEM((tm + HALO, d), F32), pltpu.VMEM((tm, d), F32)],
        compiler_params=_cparams(("parallel", "parallel")),
        name="dwconv_ln_swish",
    )(u, u, dw_w, dw_b, ln_g, ln_b)


def _out_router_kernel(a1_ref, a2_ref, w1_ref, w2_ref, b_ref, x_ref, mod_ref, g_ref, rw_ref, rb_ref,
                       xo_ref, h_ref, idx_ref, wt_ref):
    m = mod_ref[0]
    y = (jnp.dot(a1_ref[0], w1_ref[...], preferred_element_type=F32)
         + jnp.dot(a2_ref[0], w2_ref[...], preferred_element_type=F32) + b_ref[...])
    x = x_ref[0] + m[2:3] * y
    xo_ref[0] = x
    h = _norm_mod(x, g_ref[...], m[4:5], m[3:4])
    chunks = h.shape[1] // LANES
    for c in range(chunks):
        h_ref[0, pl.ds(c, h.shape[0], stride=chunks), :] = h[:, c * LANES:(c + 1) * LANES]
    h_hi = h.astype(BF16)
    h_lo = (h - h_hi.astype(F32)).astype(BF16)
    part = (jnp.dot(h_hi, rw_ref[...], preferred_element_type=F32)
            + jnp.dot(h_lo, rw_ref[...], preferred_element_type=F32))
    logits = part[:, :LANES] + part[:, LANES:] + rb_ref[...]
    lane = lax.broadcasted_iota(I32, logits.shape, 1)
    lane_f = lane.astype(F32)
    idx_out = jnp.zeros(logits.shape, F32)
    val_out = jnp.full(logits.shape, NEG, F32)
    for k in range(TOP_K):
        best = jnp.max(logits, axis=-1, keepdims=True)
        pick = jnp.min(jnp.where(logits == best, lane_f, float(LANES)), axis=-1, keepdims=True)
        idx_out = jnp.where(lane == k, pick, idx_out)
        val_out = jnp.where(lane == k, best, val_out)
        logits = jnp.where(lane_f == pick, TAKEN, logits)
    e = jnp.where(lane < TOP_K, jnp.exp(val_out - jnp.max(val_out, axis=-1, keepdims=True)), 0.0)
    idx_ref[0] = idx_out.astype(I32)
    wt_ref[0] = e / jnp.sum(e, axis=-1, keepdims=True)


def _out_proj_router(a1, a2, c1, c2, w1, w2, bias, x, mod, gain, rw_parts, rb, tm):
    b, s, d = x.shape
    kw = w1.shape[0]
    tok = lambda i, j: (i, j, 0)
    const = lambda i, j: (0, 0)
    return pl.pallas_call(
        _out_router_kernel,
        out_shape=(jax.ShapeDtypeStruct((b, s, d), F32), jax.ShapeDtypeStruct((b, s * (d // LANES), LANES), F32),
                   jax.ShapeDtypeStruct((b, s, LANES), I32), jax.ShapeDtypeStruct((b, s, LANES), F32)),
        grid=(b, s // tm),
        in_specs=[
            pl.BlockSpec((1, tm, kw), lambda i, j: (i, j, c1)),
            pl.BlockSpec((1, tm, kw), lambda i, j: (i, j, c2)),
            pl.BlockSpec((kw, d), const),
            pl.BlockSpec((kw, d), const),
            pl.BlockSpec((1, d), const),
            pl.BlockSpec((1, tm, d), tok),
            pl.BlockSpec((1, 6, d), lambda i, j: (i, 0, 0)),
            pl.BlockSpec((1, d), const),
            pl.BlockSpec((d, 2 * LANES), const),
            pl.BlockSpec((1, LANES), const),
        ],
        out_specs=(pl.BlockSpec((1, tm, d), tok), pl.BlockSpec((1, tm * (d // LANES), LANES), tok),
                   pl.BlockSpec((1, tm, LANES), tok), pl.BlockSpec((1, tm, LANES), tok)),
        compiler_params=_cparams(("parallel", "parallel")),
        name="out_proj_router",
    )(a1, a2, w1, w2, bias, x, mod, gain, rw_parts, rb)


ROW_CHUNKS = D_MODEL // LANES
DEINT = 512
DUMP_ROWS = 512


def _expert_kernel(be_ref, nv_ref, gsrc0_ref, gsrc1_ref, gdstp_ref, gdst0_ref, h_ref, w1_ref, b1g_ref, b1l_ref,
                   w2_ref, b2_ref, pe_ref, po_ref, yt_ref, xbuf, ybuf, w1g_sc, w1l_sc, w2_sc, gsem, ssem,
                   *, n_blocks, dump_row):
    i = pl.program_id(0)
    cur = i % 2
    nxt = 1 - cur
    rows = MOE_BLOCK

    def tile_rows(first):
        return pl.ds(pl.multiple_of(first, ROW_CHUNKS), ROW_CHUNKS)

    def buf_rows(r):
        return pl.ds(r * ROW_CHUNKS, ROW_CHUNKS) if isinstance(r, int) else tile_rows(r * ROW_CHUNKS)

    def gather(idx_ref, slot, r):
        return pltpu.make_async_copy(h_ref.at[tile_rows(idx_ref[0, 0, r])], xbuf.at[slot, buf_rows(r)], gsem.at[slot])

    def scatter(dst_first, slot, r):
        return pltpu.make_async_copy(ybuf.at[slot, buf_rows(r)], yt_ref.at[tile_rows(dst_first)], ssem.at[slot])

    def wait_gather(slot):
        pltpu.make_async_copy(h_ref.at[pl.ds(0, rows * ROW_CHUNKS)], xbuf.at[slot], gsem.at[slot]).wait()

    def wait_scatter(slot):
        pltpu.make_async_copy(ybuf.at[slot], yt_ref.at[pl.ds(0, rows * ROW_CHUNKS)], ssem.at[slot]).wait()

    @pl.when(i == 0)
    def _():
        ybuf[...] = jnp.zeros(ybuf.shape, F32)

        def prime(r, c):
            gather(gsrc0_ref, 0, r).start()
            return c

        lax.fori_loop(0, rows, prime, 0)

    wait_gather(cur)

    @pl.when(i >= 1)
    def _():
        wait_scatter(cur)

    first = i == 0
    changed = jnp.logical_or(first, be_ref[i] != be_ref[jnp.maximum(i - 1, 0)])

    @pl.when(changed)
    def _():
        for c in range(w1_ref.shape[3] // DEINT):
            wc = w1_ref[0, 0, :, c * DEINT:(c + 1) * DEINT].astype(BF16)
            cs = slice(c * (DEINT // 2), (c + 1) * (DEINT // 2))
            w1g_sc[:, cs] = jnp.dot(wc, pe_ref[...], preferred_element_type=F32).astype(BF16)
            w1l_sc[:, cs] = jnp.dot(wc, po_ref[...], preferred_element_type=F32).astype(BF16)
        w2_sc[...] = w2_ref[0, 0].astype(BF16)

    for r in range(rows):
        gather(gsrc1_ref, nxt, r).start(priority=r % 2)
        scatter(jnp.where(first, dump_row + r * ROW_CHUNKS, gdstp_ref[0, 0, r]), nxt, r).start(priority=(r + 1) % 2)

    def chunk(c):
        return pl.ds(c, rows, stride=ROW_CHUNKS)

    x = jnp.concatenate([xbuf[cur, chunk(c), :] for c in range(ROW_CHUNKS)], axis=1)
    row = lax.broadcasted_iota(I32, x.shape, 0)
    xb = jnp.where(row < nv_ref[i], x, 0.0).astype(BF16)
    glu = jnp.dot(xb, w1g_sc[...], preferred_element_type=F32) + b1g_ref[0]
    lin = jnp.dot(xb, w1l_sc[...], preferred_element_type=F32) + b1l_ref[0]
    glu = jnp.minimum(glu, SWIGLU_LIMIT)
    lin = jnp.clip(lin, -SWIGLU_LIMIT, SWIGLU_LIMIT)
    act = glu * _sigmoid(SWIGLU_ALPHA * glu) * (lin + 1.0)
    y = jnp.dot(act.astype(BF16), w2_sc[...], preferred_element_type=F32) + b2_ref[0]
    for c in range(ROW_CHUNKS):
        ybuf[cur, chunk(c), :] = y[:, c * LANES:(c + 1) * LANES]

    @pl.when(i == n_blocks - 1)
    def _():
        def last(r, c):
            scatter(gdst0_ref[0, 0, r], cur, r).start()
            return c

        lax.fori_loop(0, rows, last, 0)
        wait_gather(nxt)
        wait_scatter(nxt)
        wait_scatter(cur)


def _experts(h, gsrc, gdst, block_expert, n_valid, layer, w1, b1g, b1l, w2, b2):
    t = h.shape[0] // ROW_CHUNKS
    n_blocks = gsrc.shape[0]
    d, f2 = w1.shape[2:]
    f = f2 // 2
    sel = np.arange(DEINT)[:, None] - 2 * np.arange(DEINT // 2)[None, :]
    p_even = jnp.asarray((sel == 0).astype(np.float32), dtype=BF16)
    p_odd = jnp.asarray((sel == 1).astype(np.float32), dtype=BF16)
    last = n_blocks - 1

    def wsel(i, be, nv):
        return (be[i], 0, 0)

    def wsel_layer(i, be, nv):
        return (layer, be[i], 0, 0)

    const = lambda i, be, nv: (0, 0)
    idx_blk = (1, 1, MOE_BLOCK)
    kern = functools.partial(_expert_kernel, n_blocks=n_blocks, dump_row=TOP_K * t * ROW_CHUNKS)
    return pl.pallas_call(
        kern,
        out_shape=jax.ShapeDtypeStruct(((TOP_K * t + DUMP_ROWS) * ROW_CHUNKS, LANES), F32),
        grid_spec=pltpu.PrefetchScalarGridSpec(
            num_scalar_prefetch=2,
            grid=(n_blocks,),
            in_specs=[
                pl.BlockSpec(idx_blk, lambda i, be, nv: (i, 0, 0), memory_space=pltpu.SMEM),
                pl.BlockSpec(idx_blk, lambda i, be, nv: (jnp.minimum(i + 1, last), 0, 0), memory_space=pltpu.SMEM),
                pl.BlockSpec(idx_blk, lambda i, be, nv: (jnp.maximum(i - 1, 0), 0, 0), memory_space=pltpu.SMEM),
                pl.BlockSpec(idx_blk, lambda i, be, nv: (i, 0, 0), memory_space=pltpu.SMEM),
                pl.BlockSpec(memory_space=pl.ANY),
                pl.BlockSpec((1, 1, d, f2), wsel_layer),
                pl.BlockSpec((1, 1, f), wsel),
                pl.BlockSpec((1, 1, f), wsel),
                pl.BlockSpec((1, 1, f, d), wsel_layer),
                pl.BlockSpec((1, 1, d), wsel),
                pl.BlockSpec((DEINT, DEINT // 2), const),
                pl.BlockSpec((DEINT, DEINT // 2), const),
            ],
            out_specs=pl.BlockSpec(memory_space=pl.ANY),
            scratch_shapes=[pltpu.VMEM((2, MOE_BLOCK * ROW_CHUNKS, LANES), F32),
                            pltpu.VMEM((2, MOE_BLOCK * ROW_CHUNKS, LANES), F32),
                            pltpu.VMEM((d, f), BF16), pltpu.VMEM((d, f), BF16), pltpu.VMEM((f, d), BF16),
                            pltpu.SemaphoreType.DMA((2,)), pltpu.SemaphoreType.DMA((2,))],
        ),
        compiler_params=_cparams(("arbitrary",)),
        name="moe_experts",
    )(block_expert, n_valid, gsrc, gsrc, gdst, gdst, h, w1, b1g, b1l, w2, b2, p_even, p_odd)


def _combine_kernel(y0_ref, y1_ref, y2_ref, y3_ref, wt_ref, x_ref, mod_ref, o_ref):
    wt = wt_ref[0]
    g2 = mod_ref[0][5:6]
    y_refs = (y0_ref, y1_ref, y2_ref, y3_ref)
    wk = [wt[:, k:k + 1] for k in range(TOP_K)]
    tm = wt.shape[0]
    for c in range(ROW_CHUNKS):
        chunk = pl.ds(c, tm, stride=ROW_CHUNKS)
        y = wk[0] * y_refs[0][chunk, :]
        for k in range(1, TOP_K):
            y = y + wk[k] * y_refs[k][chunk, :]
        cs = slice(c * LANES, (c + 1) * LANES)
        o_ref[0, :, cs] = x_ref[0, :, cs] + g2[:, cs] * y


def _combine(yt, wts, x, mod, tm):
    b, s, d = x.shape
    nt = s // tm
    tiles = b * nt
    tok = lambda i, j: (i, j, 0)

    def yspec(k):
        return pl.BlockSpec((tm * ROW_CHUNKS, LANES), lambda i, j: (k * tiles + i * nt + j, 0))

    return pl.pallas_call(
        _combine_kernel,
        out_shape=jax.ShapeDtypeStruct((b, s, d), F32),
        grid=(b, nt),
        in_specs=[yspec(k) for k in range(TOP_K)] + [
            pl.BlockSpec((1, tm, LANES), tok),
            pl.BlockSpec((1, tm, d), tok),
            pl.BlockSpec((1, 6, d), lambda i, j: (i, 0, 0)),
        ],
        out_specs=pl.BlockSpec((1, tm, d), tok),
        compiler_params=_cparams(("arbitrary", "arbitrary")),
        name="moe_combine",
    )(yt, yt, yt, yt, wts, x, mod)


def _route_slots(top_idx, n_blocks):
    t = top_idx.shape[0]
    n_assign = t * TOP_K
    n_slots = n_blocks * MOE_BLOCK
    sorted_e, order = lax.sort_key_val(top_idx.reshape(-1), jnp.arange(n_assign, dtype=I32))
    experts = jnp.arange(N_EXPERTS + 1, dtype=I32)
    start = jnp.sum((sorted_e[None, :] < experts[:, None]).astype(I32), axis=1)
    counts = start[1:] - start[:-1]
    padded = (counts + MOE_BLOCK - 1) // MOE_BLOCK * MOE_BLOCK
    pad_end = jnp.cumsum(padded)
    pad_start = pad_end - padded
    blk_start = jnp.arange(n_blocks, dtype=I32) * MOE_BLOCK
    block_expert = jnp.minimum(jnp.sum((pad_end[None, :] <= blk_start[:, None]).astype(I32), axis=1), N_EXPERTS - 1)
    n_valid = jnp.clip(pad_start[block_expert] + counts[block_expert] - blk_start, 0, MOE_BLOCK).astype(I32)
    within = jnp.arange(MOE_BLOCK, dtype=I32)[None, :]
    valid = within < n_valid[:, None]
    rank = (blk_start - pad_start[block_expert] + start[block_expert])[:, None] + within
    slot_a = order[jnp.clip(rank, 0, n_assign - 1)]
    slot = blk_start[:, None] + within
    gsrc = jnp.where(valid, slot_a // TOP_K, 0) * ROW_CHUNKS
    gdst = jnp.where(valid, (slot_a % TOP_K) * t + slot_a // TOP_K, n_assign + slot % DUMP_ROWS) * ROW_CHUNKS
    shape = (n_blocks, 1, MOE_BLOCK)
    return gsrc.reshape(shape), gdst.reshape(shape), block_expert, n_valid


def _moe(x_mid, h, idx, wts, mod, layer, w1, b1, w2, b2, tm):
    b, s, d = x_mid.shape
    t = b * s
    n_blocks = -(-t * TOP_K // MOE_BLOCK) + N_EXPERTS
    top_idx = idx.reshape(t, LANES)[:, :TOP_K]
    gsrc, gdst, block_expert, n_valid = _route_slots(top_idx, n_blocks)
    yt = _experts(h.reshape(t * ROW_CHUNKS, LANES), gsrc, gdst, block_expert, n_valid, layer, w1,
                  b1[:, None, 0::2], b1[:, None, 1::2], w2, b2[:, None, :])
    return _combine(yt, wts, x_mid, mod, tm)


def _pad_cols(w, n):
    return jnp.pad(w, ((0, 0), (0, n - w.shape[-1])))


def _attn_colgain(diff_qk_gain, nsa_q_gain, nsa_k_gain):
    scale = HEAD_DIM ** -0.5 * LOG2E
    ones = jnp.ones((LANES,), F32)
    parts = [jnp.tile(diff_qk_gain[0] * scale, 8), jnp.tile(diff_qk_gain[1], 8), jnp.ones((512,), F32),
             jnp.tile(nsa_q_gain * scale, 8), ones, ones, jnp.tile(nsa_k_gain[1], 2), ones,
             jnp.tile(nsa_k_gain[2], 2), ones, ones]
    return jnp.concatenate(parts).reshape(1, W_IN_PAD)


def _overlap_matrix(s):
    n_sel = s // SEL_BLOCK
    ncp = s // CMP_STRIDE
    c_start = np.arange(ncp) * CMP_STRIDE
    s_start = np.arange(n_sel) * SEL_BLOCK
    ovl = (c_start[None, :] < s_start[:, None] + SEL_BLOCK) & (c_start[None, :] + CMP_BLOCK > s_start[:, None])
    ovl[:, ncp - 1] = False
    return jnp.asarray(ovl.astype(np.float32), dtype=BF16)


def _attention_layer(x, mod, gain, w_in, w_out, diff_qk_gain, diff_lambda, diff_subln, nsa_q_gain, nsa_k_gain,
                     cmp_pos, cmp_w1, cmp_w2, lam_init):
    b, s, d = x.shape
    n_gate = NSA_Q_HEADS * 3
    gate_perm = np.arange(n_gate).reshape(NSA_KV_HEADS, NSA_GROUP, 3).transpose(0, 2, 1).reshape(-1)
    w_pad = jnp.concatenate([w_in[:, :COL_GATE], w_in[:, COL_GATE + gate_perm],
                             jnp.zeros((d, W_IN_PAD - COL_GATE - n_gate), w_in.dtype)], axis=1).astype(BF16)
    zb, dqT, dvT, nqT, vselT, vwinT, gT, chunks = _attn_in_proj(
        x, mod, gain, w_pad, _attn_colgain(diff_qk_gain, nsa_q_gain, nsa_k_gain))

    o_diff = _diff_attention(zb, dqT.reshape(b, DIFF_HEADS, 2, HEAD_DIM, s), dvT,
                             _score_bound(diff_qk_gain[0], diff_qk_gain[1]), diff_lambda, diff_subln, lam_init,
                             DIFF_TQ, ATTN_TM)

    ncp = s // CMP_STRIDE
    cmp_out = _compress(chunks, cmp_pos.reshape(2, 1, CMP_BLOCK * HEAD_DIM), cmp_w1, cmp_w2, nsa_k_gain[0:1])
    kc = cmp_out[:, 0:2].transpose(0, 2, 1, 3).reshape(b, ncp, LANES).astype(BF16)
    vcT = cmp_out[:, 2:4].transpose(0, 1, 3, 2).astype(BF16)
    gatesT = gT[:, :n_gate].reshape(b, NSA_KV_HEADS, 3, NSA_GROUP, s)
    shifts = jnp.stack([_score_bound(nsa_q_gain, nsa_k_gain[1]), _score_bound(nsa_q_gain, nsa_k_gain[2])])
    o_nsa = _nsa_attention(zb, nqT.reshape(b, NSA_KV_HEADS, NSA_GROUP, HEAD_DIM, s), vselT, vwinT, kc, vcT,
                           _overlap_matrix(s), gatesT, shifts, NSA_TILE)
    return o_diff, o_nsa


def kernel(x, c, mod_w, mod_b, norm_mix, norm_ffn, attn_w_in, attn_w_out, diff_qk_gain, diff_lambda, diff_subln,
           nsa_q_gain, nsa_k_gain, nsa_cmp_pos, nsa_cmp_w1, nsa_cmp_w2, conv_pw1_w, conv_pw1_b, conv_dw_w,
           conv_dw_b, conv_ln_g, conv_ln_b, conv_pw2_w, conv_pw2_b, router_w, router_b, moe_w1, moe_b1, moe_w2,
           moe_b2):
    b, s, d = x.shape
    depth = mod_w.shape[0]
    tm = 512 if s % 512 == 0 else 256
    mods = _modulation(c, mod_w, mod_b).reshape(depth, b, 6, d)
    half = d // 2
    for i in range(depth):
        mod = mods[i]
        j = i // 2
        if i % 2 == 0:
            lam_init = 0.8 - 0.6 * math.exp(-0.3 * i)
            o_diff, o_nsa = _attention_layer(
                x, mod, norm_mix[i:i + 1], attn_w_in[j], attn_w_out[j], diff_qk_gain[j], diff_lambda[j],
                diff_subln[j], nsa_q_gain[j], nsa_k_gain[j], nsa_cmp_pos[j], nsa_cmp_w1[j], nsa_cmp_w2[j], lam_init)
            a1, a2, c1, c2 = o_diff, o_nsa, 0, 0
            w_o = attn_w_out[j].astype(BF16)
            bias = jnp.zeros((1, d), F32)
        else:
            u = _conv_in_proj(x, mod, norm_mix[i:i + 1], conv_pw1_w[j].astype(BF16), conv_pw1_b[j:j + 1], tm)
            v = _dwconv_ln_swish(u, conv_dw_w[j].reshape(CONV_WIDTH, d), conv_dw_b[j:j + 1], conv_ln_g[j:j + 1],
                                 conv_ln_b[j:j + 1], 256)
            a1, a2, c1, c2 = v, v, 0, 1
            w_o = conv_pw2_w[j].astype(BF16)
            bias = conv_pw2_b[j:j + 1]
        rw = _pad_cols(router_w[i], LANES)
        rw_hi = rw.astype(BF16)
        rw_lo = (rw - rw_hi.astype(F32)).astype(BF16)
        rb = jnp.concatenate([router_b[i], jnp.full((LANES - N_EXPERTS,), NEG, F32)]).reshape(1, LANES)
        x_mid, h, idx, wts = _out_proj_router(a1, a2, c1, c2, w_o[:half], w_o[half:], bias, x, mod,
                                              norm_ffn[i:i + 1], jnp.concatenate([rw_hi, rw_lo], axis=1), rb, tm)
        x = _moe(x_mid, h, idx, wts, mod, i, moe_w1, moe_b1[i], moe_w2, moe_b2[i], 256)
    return x
```

```python
import functools
import math

import jax
import jax.numpy as jnp
import numpy as np
from jax import lax
from jax.experimental import pallas as pl
from jax.experimental.pallas import tpu as pltpu

F32 = jnp.float32
BF16 = jnp.bfloat16
I32 = jnp.int32

D_MODEL = 1024
HEAD_DIM = 64
DIFF_HEADS = 4
NSA_Q_HEADS = 8
NSA_KV_HEADS = 2
NSA_GROUP = NSA_Q_HEADS // NSA_KV_HEADS
CMP_BLOCK = 32
CMP_STRIDE = 16
CMP_HIDDEN = 256
SEL_BLOCK = 64
SEL_TOP = 16
SEL_FORCED_SCORE = 1.0e4
WINDOW = 512
CONV_WIDTH = 31
N_EXPERTS = 32
TOP_K = 4
SWIGLU_ALPHA = 1.702
SWIGLU_LIMIT = 7.0
MOE_BLOCK = 256
NORM_EPS = 1e-6

LANES = 128
NEG = -1e30
TAKEN = -3e38
VMEM_LIMIT = 56 * 1024 * 1024

COL_DQ, COL_DK, COL_DV, COL_NQ = 0, 512, 1024, 1536
COL_KCMP, COL_VCMP, COL_KSEL, COL_VSEL, COL_KWIN, COL_VWIN = 2048, 2176, 2304, 2432, 2560, 2688
COL_GATE = 2816
W_IN_PAD = 2944
NORM_SLABS = frozenset(list(range(0, 8)) + list(range(12, 16)) + [COL_KSEL // LANES, COL_KWIN // LANES])


def _cparams(sem):
    return pltpu.CompilerParams(dimension_semantics=sem, vmem_limit_bytes=VMEM_LIMIT)


def _sigmoid(v):
    return 1.0 / (1.0 + jnp.exp(-v))


def _norm_mod(x, g, sc, sh):
    ms = jnp.mean(x * x, axis=-1, keepdims=True)
    return (x * lax.rsqrt(ms + NORM_EPS)) * g * (1.0 + sc) + sh


def _mod_kernel(c_ref, w_ref, b_ref, o_ref):
    c = c_ref[...]
    cond = c * _sigmoid(c)
    o_ref[0] = jnp.dot(cond, w_ref[0], preferred_element_type=F32, precision=lax.Precision.HIGHEST) + b_ref[0]


def _modulation(c, mod_w, mod_b):
    depth, d, n = mod_w.shape
    b = c.shape[0]
    tn = 1536
    return pl.pallas_call(
        _mod_kernel,
        out_shape=jax.ShapeDtypeStruct((depth, b, n), F32),
        grid=(depth, n // tn),
        in_specs=[
            pl.BlockSpec((b, d), lambda i, j: (0, 0)),
            pl.BlockSpec((1, d, tn), lambda i, j: (i, 0, j)),
            pl.BlockSpec((1, 1, tn), lambda i, j: (i, 0, j)),
        ],
        out_specs=pl.BlockSpec((1, b, tn), lambda i, j: (i, 0, j)),
        compiler_params=_cparams(("parallel", "parallel")),
        name="modulation",
    )(c, mod_w, mod_b.reshape(depth, 1, n))


KZ_DK, KZ_KSEL, KZ_KWIN, KZ_WIDTH = 0, 512, 640, 768
KEY_COLS = {COL_DK + 128 * u: KZ_DK + 128 * u for u in range(4)}
KEY_COLS.update({COL_KSEL: KZ_KSEL, COL_KWIN: KZ_KWIN})
ATTN_TM = 512
DIFF_TQ = 1024
NSA_TILE = 512
GATE_ROWS = 32


def _in_attn_kernel(x_ref, mod_ref, g_ref, w_ref, cg_ref, z_ref, dqT_ref, dvT_ref, nqT_ref, vsT_ref, vwT_ref, gT_ref,
                    ch_ref, cmp_sc):
    m = mod_ref[0]
    h = _norm_mod(x_ref[0], g_ref[...], m[1:2], m[0:1]).astype(BF16)
    lo = lax.broadcasted_iota(I32, (1, LANES), 1) < HEAD_DIM
    groups = ((0, 512), (512, 1024), (1024, 1536), (1536, 2048), (2048, 2816), (2816, 2944))
    halves = ATTN_TM // NSA_TILE
    for c0, c1 in groups:
        z = jnp.dot(h, w_ref[:, c0:c1], preferred_element_type=F32)
        for s in range((c1 - c0) // LANES):
            a0 = c0 + s * LANES
            zs = z[:, s * LANES:(s + 1) * LANES]
            if a0 == COL_GATE:
                gT_ref[0] = _sigmoid(zs).T[:GATE_ROWS]
                continue
            if a0 // LANES in NORM_SLABS:
                z2 = zs * zs
                s_lo = jnp.sum(jnp.where(lo, z2, 0.0), axis=-1, keepdims=True)
                s_hi = jnp.sum(jnp.where(lo, 0.0, z2), axis=-1, keepdims=True)
                inv = jnp.where(lo, lax.rsqrt(s_lo * (1.0 / HEAD_DIM) + NORM_EPS),
                                lax.rsqrt(s_hi * (1.0 / HEAD_DIM) + NORM_EPS))
                zs = zs * inv
            zs = zs * cg_ref[:, a0:a0 + LANES]
            if COL_DQ <= a0 < COL_DK:
                dqT_ref[0, a0 - COL_DQ:a0 - COL_DQ + LANES, :] = zs.T.astype(BF16)
            elif COL_DV <= a0 < COL_NQ:
                dvT_ref[0, 0, a0 - COL_DV:a0 - COL_DV + LANES, :] = zs.T.astype(BF16)
            elif COL_NQ <= a0 < COL_KCMP:
                nqT_ref[0, a0 - COL_NQ:a0 - COL_NQ + LANES, :] = zs.T.astype(BF16)
            elif a0 in (COL_VSEL, COL_VWIN):
                zt = zs.T.astype(BF16)
                ref = vsT_ref if a0 == COL_VSEL else vwT_ref
                for u in range(halves):
                    ref[0, u] = zt[:, u * NSA_TILE:(u + 1) * NSA_TILE]
            elif a0 in (COL_KCMP, COL_VCMP):
                cmp_sc[...] = zs
                first = 0 if a0 == COL_KCMP else NSA_KV_HEADS
                n_rows = zs.shape[0] // CMP_STRIDE
                for pair in range(CMP_STRIDE // 2):
                    even = cmp_sc[pl.ds(2 * pair, n_rows, stride=CMP_STRIDE), :]
                    odd = cmp_sc[pl.ds(2 * pair + 1, n_rows, stride=CMP_STRIDE), :]
                    cs = slice(pair * LANES, (pair + 1) * LANES)
                    ch_ref[0, first, :, cs] = jnp.where(lo, even, pltpu.roll(odd, HEAD_DIM, 1)).astype(BF16)
                    ch_ref[0, first + 1, :, cs] = jnp.where(lo, pltpu.roll(even, HEAD_DIM, 1), odd).astype(BF16)
            else:
                k0 = KEY_COLS[a0]
                z_ref[0, :, k0:k0 + LANES] = zs.astype(BF16)


def _attn_in_proj(x, mod, gain, w_pad, colgain):
    b, s, d = x.shape
    tm = ATTN_TM
    halves = tm // NSA_TILE
    tok = lambda i, j: (i, j, 0)
    featT = lambda i, j: (i, 0, j)
    return pl.pallas_call(
        _in_attn_kernel,
        out_shape=(jax.ShapeDtypeStruct((b, s, KZ_WIDTH), BF16),
                   jax.ShapeDtypeStruct((b, 512, s), BF16),
                   jax.ShapeDtypeStruct((b, s // tm, 512, tm), BF16),
                   jax.ShapeDtypeStruct((b, 512, s), BF16),
                   jax.ShapeDtypeStruct((b, s // NSA_TILE, LANES, NSA_TILE), BF16),
                   jax.ShapeDtypeStruct((b, s // NSA_TILE, LANES, NSA_TILE), BF16),
                   jax.ShapeDtypeStruct((b, GATE_ROWS, s), F32),
                   jax.ShapeDtypeStruct((b, 2 * NSA_KV_HEADS, s // CMP_STRIDE, CMP_STRIDE * HEAD_DIM), BF16)),
        grid=(b, s // tm),
        in_specs=[
            pl.BlockSpec((1, tm, d), tok),
            pl.BlockSpec((1, 6, d), lambda i, j: (i, 0, 0)),
            pl.BlockSpec((1, d), lambda i, j: (0, 0)),
            pl.BlockSpec((d, W_IN_PAD), lambda i, j: (0, 0)),
            pl.BlockSpec((1, W_IN_PAD), lambda i, j: (0, 0)),
        ],
        out_specs=(pl.BlockSpec((1, tm, KZ_WIDTH), tok),
                   pl.BlockSpec((1, 512, tm), featT),
                   pl.BlockSpec((1, 1, 512, tm), lambda i, j: (i, j, 0, 0)),
                   pl.BlockSpec((1, 512, tm), featT),
                   pl.BlockSpec((1, halves, LANES, NSA_TILE), lambda i, j: (i, j, 0, 0)),
                   pl.BlockSpec((1, halves, LANES, NSA_TILE), lambda i, j: (i, j, 0, 0)),
                   pl.BlockSpec((1, GATE_ROWS, tm), featT),
                   pl.BlockSpec((1, 2 * NSA_KV_HEADS, tm // CMP_STRIDE, CMP_STRIDE * HEAD_DIM),
                                lambda i, j: (i, 0, j, 0))),
        scratch_shapes=[pltpu.VMEM((tm, LANES), F32)],
        compiler_params=_cparams(("parallel", "parallel")),
        name="attn_in_proj",
    )(x, mod, gain, w_pad, colgain)


FIXED_SHIFT_LIMIT = 57.0
FIXED_TILES_PER_STEP = 4
LOG2E = math.log2(math.e)


def _score_bound(gain_q, gain_k):
    return (math.sqrt(HEAD_DIM) * LOG2E) * jnp.max(jnp.abs(gain_q * gain_k))


SUM_ROWS = 8


def _with_ones(vt):
    return jnp.concatenate([vt, jnp.ones((SUM_ROWS, vt.shape[1]), vt.dtype)], axis=0)


def _fixed_update(ss, vts, shift, acc_ref):
    ps = [jnp.exp2(s - shift).astype(BF16) for s in ss]
    acc_ref[...] += sum(jnp.dot(_with_ones(vt), p, preferred_element_type=F32) for vt, p in zip(vts, ps))


def _online_update(ss, vts, m_ref, acc_ref):
    for s, vt in zip(ss, vts):
        m_old = m_ref[...]
        m_new = jnp.maximum(m_old, jnp.max(s, axis=0, keepdims=True))
        alpha = jnp.exp2(m_old - m_new)
        p = jnp.exp2(s - m_new).astype(BF16)
        acc_ref[...] = alpha * acc_ref[...] + jnp.dot(_with_ones(vt), p, preferred_element_type=F32)
        m_ref[...] = m_new


def _tile_loops(n_full, tiles, width, n_masked=1):
    def group(jj, c):
        tiles(tuple(width * jj + u for u in range(width)), (False,) * width)
        return c

    lax.fori_loop(0, n_full // width, group, 0)
    rem = n_full % width
    base = n_full - rem
    for r in range(width):
        @pl.when(rem == r)
        def _():
            tiles(tuple(base + u for u in range(r + n_masked)), (False,) * r + (True,) * n_masked)


def _diff_kernel(mb_ref, lam_ref, qT_ref, k_ref, vT_ref, sub_ref, o_ref, m0, m1, a0, a1, *, tq, tk, lam_init):
    i = pl.program_id(2)
    m_refs, acc_refs = (m0, m1), (a0, a1)
    q = qT_ref[0, 0]
    zero = jnp.zeros((HEAD_DIM, tq), BF16)
    qp = (jnp.concatenate([q[0], zero], axis=0), jnp.concatenate([zero, q[1]], axis=0))
    for mm in range(2):
        m_refs[mm][...] = jnp.full(m_refs[mm].shape, NEG, F32)
        acc_refs[mm][...] = jnp.zeros(acc_refs[mm].shape, F32)
    shift = mb_ref[0]
    dv = 2 * HEAD_DIM

    def tiles(js, masks, fixed):
        kts = [k_ref[0, pl.ds(pl.multiple_of(j * tk, tk), tk), :] for j in js]
        vts = [vT_ref[0, j] for j in js]
        for mm in range(2):
            ss = []
            for j, kt, masked in zip(js, kts, masks):
                s = jnp.dot(kt, qp[mm], preferred_element_type=F32)
                if masked:
                    kpos = j * tk + lax.broadcasted_iota(I32, (tk, tq), 0)
                    t = i * tq + lax.broadcasted_iota(I32, (tk, tq), 1)
                    s = jnp.where(kpos <= t, s, NEG)
                ss.append(s)
            if fixed:
                _fixed_update(ss, vts, shift, acc_refs[mm])
            else:
                _online_update(ss, vts, m_refs[mm], acc_refs[mm])

    n_full = (i * tq) // tk
    n_masked = max(tq // tk, 1)

    @pl.when(shift <= FIXED_SHIFT_LIMIT)
    def _():
        _tile_loops(n_full, functools.partial(tiles, fixed=True), FIXED_TILES_PER_STEP, n_masked)

    @pl.when(shift > FIXED_SHIFT_LIMIT)
    def _():
        _tile_loops(n_full, functools.partial(tiles, fixed=False), 1, n_masked)

    lv = lam_ref[...]
    lam = (jnp.exp(jnp.sum(lv[0:1] * lv[1:2], axis=-1, keepdims=True))
           - jnp.exp(jnp.sum(lv[2:3] * lv[3:4], axis=-1, keepdims=True)) + lam_init)
    o = a0[:dv] / a0[dv:dv + 1] - lam * (a1[:dv] / a1[dv:dv + 1])
    ms = jnp.mean(o * o, axis=0, keepdims=True)
    o = o * lax.rsqrt(ms + NORM_EPS) * sub_ref[...] * (1.0 - lam_init)
    o_ref[0] = o.T.astype(BF16)


def _diff_attention(zb, qT, vT, shift, lam_vecs, subln, lam_init, tq, tk):
    b, s, _ = zb.shape
    kern = functools.partial(_diff_kernel, tq=tq, tk=tk, lam_init=lam_init)
    dv = 2 * HEAD_DIM
    return pl.pallas_call(
        kern,
        out_shape=jax.ShapeDtypeStruct((b, s, DIFF_HEADS * dv), BF16),
        grid=(b, DIFF_HEADS, s // tq),
        in_specs=[
            pl.BlockSpec(memory_space=pltpu.SMEM),
            pl.BlockSpec((4, HEAD_DIM), lambda bi, h, i: (0, 0)),
            pl.BlockSpec((1, 1, 2, HEAD_DIM, tq), lambda bi, h, i: (bi, h, 0, 0, i)),
            pl.BlockSpec((1, s, LANES), lambda bi, h, i: (bi, 0, KZ_DK // LANES + h)),
            pl.BlockSpec((1, s // tk, 2 * HEAD_DIM, tk), lambda bi, h, i: (bi, 0, h, 0)),
            pl.BlockSpec((2 * HEAD_DIM, 1), lambda bi, h, i: (0, 0)),
        ],
        out_specs=pl.BlockSpec((1, tq, LANES), lambda bi, h, i: (bi, i, h)),
        scratch_shapes=[pltpu.VMEM((1, tq), F32)] * 2 + [pltpu.VMEM((dv + SUM_ROWS, tq), F32)] * 2,
        compiler_params=_cparams(("parallel", "parallel", "arbitrary")),
        name="diff_attention",
    )(shift.reshape(1), lam_vecs, qT, zb, vT, subln.reshape(-1, 1))


def _gelu_tanh(v):
    return 0.5 * v * (1.0 + jnp.tanh(math.sqrt(2.0 / math.pi) * (v + 0.044715 * (v * v * v))))


def _cmp_kernel(c_ref, pos_ref, w1_ref, w2_ref, kg_ref, o_ref, *, n_cmp):
    jh = pl.program_id(1)
    half = CMP_STRIDE * HEAD_DIM
    c = c_ref[0, 0]
    w1a = w1_ref[0, :half, :].astype(BF16)
    w1b = w1_ref[0, half:, :].astype(BF16)
    pos = jnp.broadcast_to(pos_ref[0], (8, 2 * half)).astype(BF16)
    bias = (jnp.dot(pos[:, :half], w1a, preferred_element_type=F32)
            + jnp.dot(pos[:, half:], w1b, preferred_element_type=F32))[0:1]
    u = jnp.dot(c, w1a, preferred_element_type=F32)
    v = jnp.dot(c, w1b, preferred_element_type=F32)
    ncp = u.shape[0]
    hid = _gelu_tanh(u + pltpu.roll(v, ncp - 1, 0) + bias)
    y = jnp.dot(hid.astype(BF16), w2_ref[0].astype(BF16), preferred_element_type=F32)
    yn = y * lax.rsqrt(jnp.mean(y * y, axis=-1, keepdims=True) + NORM_EPS) * kg_ref[...]
    y = jnp.where(jh < NSA_KV_HEADS, yn, y)
    row = lax.broadcasted_iota(I32, y.shape, 0)
    o_ref[0, 0] = jnp.where(row < n_cmp, y, 0.0)


def _compress(chunks, pos, w1, w2, kgain):
    b, _, ncp, cd = chunks.shape
    kern = functools.partial(_cmp_kernel, n_cmp=ncp - 1)
    return pl.pallas_call(
        kern,
        out_shape=jax.ShapeDtypeStruct((b, 4, ncp, HEAD_DIM), F32),
        grid=(b, 4),
        in_specs=[
            pl.BlockSpec((1, 1, ncp, cd), lambda bi, j: (bi, j, 0, 0)),
            pl.BlockSpec((1, 1, 2 * cd), lambda bi, j: (j // NSA_KV_HEADS, 0, 0)),
            pl.BlockSpec((1, 2 * cd, CMP_HIDDEN), lambda bi, j: (j // NSA_KV_HEADS, 0, 0)),
            pl.BlockSpec((1, CMP_HIDDEN, HEAD_DIM), lambda bi, j: (j // NSA_KV_HEADS, 0, 0)),
            pl.BlockSpec((1, HEAD_DIM), lambda bi, j: (0, 0)),
        ],
        out_specs=pl.BlockSpec((1, 1, ncp, HEAD_DIM), lambda bi, j: (bi, j, 0, 0)),
        compiler_params=_cparams(("parallel", "parallel")),
        name="nsa_compress",
    )(chunks, pos, w1, w2, kgain)


def _nsa_kernel(mb_ref, qT_ref, ksel_ref, vselT_ref, kwin_ref, vwinT_ref, kc_ref, vcT_ref, ovl_ref, gate_ref, o_ref,
                selb_sc, m_s, m_w, a_s, a_w, *, tq, n_top):
    hk = pl.program_id(1)
    i = pl.program_id(2)
    g4 = NSA_GROUP
    nq = g4 * tq
    q4 = jnp.concatenate([qT_ref[0, 0, g] for g in range(g4)], axis=1)
    half = lax.broadcasted_iota(I32, (2 * HEAD_DIM, nq), 0) // HEAD_DIM
    qp = jnp.where(half == hk, jnp.concatenate([q4, q4], axis=0), jnp.zeros((), BF16))
    t1 = i * tq + lax.broadcasted_iota(I32, (1, tq), 1)

    def tile4(a):
        return jnp.concatenate([a] * g4, axis=1)

    ncp = kc_ref.shape[1]
    sc = jnp.dot(kc_ref[0], qp, preferred_element_type=F32)
    cend = CMP_STRIDE * lax.broadcasted_iota(I32, (ncp, tq), 0) + (CMP_BLOCK - 1)
    sc = sc + tile4(jnp.where(cend <= t1, 0.0, NEG))
    e = jnp.where(sc > 0.5 * NEG, jnp.exp2(sc - jnp.max(sc, axis=0, keepdims=True)), 0.0)
    p = e / jnp.maximum(jnp.sum(e, axis=0, keepdims=True), 1e-30)
    o_c = jnp.dot(vcT_ref[0, 0], p.astype(BF16), preferred_element_type=F32)

    psum = p[:, 0:tq]
    for g in range(1, g4):
        psum = psum + p[:, g * tq:(g + 1) * tq]
    p_hi = psum.astype(BF16)
    p_lo = (psum - p_hi.astype(F32)).astype(BF16)
    imp = (jnp.dot(ovl_ref[...], p_hi, preferred_element_type=F32)
           + jnp.dot(ovl_ref[...], p_lo, preferred_element_type=F32))
    n_sel = imp.shape[0]
    jrow = lax.broadcasted_iota(I32, (n_sel, tq), 0)
    cur = t1 // SEL_BLOCK
    forced = (jrow == 0) | (jrow == cur) | (jrow == cur - 1)
    score = jnp.where(forced, SEL_FORCED_SCORE, jnp.where(jrow <= cur, imp, NEG))
    selb = jnp.full((n_sel, tq), NEG, F32)
    jrow_f = jrow.astype(F32)
    for _ in range(n_top):
        best = jnp.max(score, axis=0, keepdims=True)
        pick = jnp.min(jnp.where(score == best, jrow_f, float(n_sel)), axis=0, keepdims=True)
        hit = jrow_f == pick
        selb = jnp.where(hit, 0.0, selb)
        score = jnp.where(hit, TAKEN, score)
    selb_sc[...] = selb

    for ref in (m_s, m_w):
        ref[...] = jnp.full(ref.shape, NEG, F32)
    for ref in (a_s, a_w):
        ref[...] = jnp.zeros(ref.shape, F32)
    tk = tq
    bpt = tk // SEL_BLOCK
    krow = lax.broadcasted_iota(I32, (tk, tq), 0)
    shift_s, shift_w = mb_ref[0], mb_ref[1]
    fixed = jnp.maximum(shift_s, shift_w) <= FIXED_SHIFT_LIMIT

    def sel_tiles(js, masks, fixed):
        ss = []
        for j, masked in zip(js, masks):
            kt = ksel_ref[0, pl.ds(pl.multiple_of(j * tk, tk), tk), :]
            rows = [jnp.broadcast_to(selb_sc[pl.ds(j * bpt + r, 1), :], (SEL_BLOCK, tq)) for r in range(bpt)]
            bias = jnp.concatenate(rows, axis=0)
            if masked:
                bias = jnp.where(j * tk + krow <= t1, bias, NEG)
            ss.append(jnp.dot(kt, qp, preferred_element_type=F32) + tile4(bias))
        vts = [vselT_ref[0, j] for j in js]
        if fixed:
            _fixed_update(ss, vts, shift_s, a_s)
        else:
            _online_update(ss, vts, m_s, a_s)

    def win_tiles(js, fixed):
        ss = []
        for j in js:
            kt = kwin_ref[0, pl.ds(pl.multiple_of(j * tk, tk), tk), :]
            kpos = j * tk + krow
            bias = jnp.where(kpos <= t1, jnp.where(kpos > t1 - WINDOW, 0.0, NEG), NEG)
            ss.append(jnp.dot(kt, qp, preferred_element_type=F32) + tile4(bias))
        vts = [vwinT_ref[0, j] for j in js]
        if fixed:
            _fixed_update(ss, vts, shift_w, a_w)
        else:
            _online_update(ss, vts, m_w, a_w)

    max_win = (WINDOW + tk - 1) // tk + 1
    n_win = jnp.minimum(i + 1, max_win)
    for use_fixed in (True, False):
        @pl.when(fixed == use_fixed)
        def _():
            _tile_loops(i, functools.partial(sel_tiles, fixed=use_fixed), FIXED_TILES_PER_STEP if use_fixed else 1)
            for c in range(1, max_win + 1):
                @pl.when(n_win == c)
                def _():
                    win_tiles(tuple(i - (c - 1) + u for u in range(c)), use_fixed)

    def gate_row(br):
        gt = gate_ref[0, 0, br]
        return jnp.concatenate([gt[g:g + 1] for g in range(g4)], axis=1)

    dv = HEAD_DIM
    out = (o_c * gate_row(0) + (a_s[:dv] / a_s[dv:dv + 1]) * gate_row(1) + (a_w[:dv] / a_w[dv:dv + 1]) * gate_row(2))
    stacked = jnp.concatenate([out[:, g * tq:(g + 1) * tq] for g in range(g4)], axis=0)
    o_ref[0] = stacked.T.astype(BF16)


def _nsa_attention(zb, qT, vselT, vwinT, kc, vcT, ovl, gatesT, shifts, tq):
    b, s, _ = zb.shape
    nt = s // tq
    n_sel = s // SEL_BLOCK
    ncp = kc.shape[1]
    kern = functools.partial(_nsa_kernel, tq=tq, n_top=min(SEL_TOP, n_sel))
    gd = NSA_GROUP * HEAD_DIM
    nq = NSA_GROUP * tq
    return pl.pallas_call(
        kern,
        out_shape=jax.ShapeDtypeStruct((b, s, NSA_Q_HEADS * HEAD_DIM), BF16),
        grid=(b, NSA_KV_HEADS, nt),
        in_specs=[
            pl.BlockSpec(memory_space=pltpu.SMEM),
            pl.BlockSpec((1, 1, NSA_GROUP, HEAD_DIM, tq), lambda bi, h, i: (bi, h, 0, 0, i)),
            pl.BlockSpec((1, s, LANES), lambda bi, h, i: (bi, 0, KZ_KSEL // LANES)),
            pl.BlockSpec((1, nt, HEAD_DIM, tq), lambda bi, h, i: (bi, 0, h, 0)),
            pl.BlockSpec((1, s, LANES), lambda bi, h, i: (bi, 0, KZ_KWIN // LANES)),
            pl.BlockSpec((1, nt, HEAD_DIM, tq), lambda bi, h, i: (bi, 0, h, 0)),
            pl.BlockSpec((1, ncp, LANES), lambda bi, h, i: (bi, 0, 0)),
            pl.BlockSpec((1, 1, HEAD_DIM, ncp), lambda bi, h, i: (bi, h, 0, 0)),
            pl.BlockSpec((n_sel, ncp), lambda bi, h, i: (0, 0)),
            pl.BlockSpec((1, 1, 3, NSA_GROUP, tq), lambda bi, h, i: (bi, h, 0, 0, i)),
        ],
        out_specs=pl.BlockSpec((1, tq, gd), lambda bi, h, i: (bi, i, h)),
        scratch_shapes=([pltpu.VMEM((n_sel, tq), F32)] + [pltpu.VMEM((1, nq), F32)] * 2
                        + [pltpu.VMEM((HEAD_DIM + SUM_ROWS, nq), F32)] * 2),
        compiler_params=_cparams(("parallel", "parallel", "arbitrary")),
        name="nsa_attention",
    )(shifts, qT, zb, vselT, zb, vwinT, kc, vcT, ovl, gatesT)


def _in_conv_kernel(x_ref, mod_ref, g_ref, w_ref, b_ref, u_ref):
    m = mod_ref[0]
    d = x_ref.shape[-1]
    h = _norm_mod(x_ref[0], g_ref[...], m[1:2], m[0:1]).astype(BF16)
    a = jnp.dot(h, w_ref[:, :d], preferred_element_type=F32) + b_ref[:, :d]
    g = jnp.dot(h, w_ref[:, d:], preferred_element_type=F32) + b_ref[:, d:]
    u_ref[0] = a * _sigmoid(g)


def _conv_in_proj(x, mod, gain, w, bias, tm):
    b, s, d = x.shape
    return pl.pallas_call(
        _in_conv_kernel,
        out_shape=jax.ShapeDtypeStruct((b, s, d), F32),
        grid=(b, s // tm),
        in_specs=[
            pl.BlockSpec((1, tm, d), lambda i, j: (i, j, 0)),
            pl.BlockSpec((1, 6, d), lambda i, j: (i, 0, 0)),
            pl.BlockSpec((1, d), lambda i, j: (0, 0)),
            pl.BlockSpec((d, 2 * d), lambda i, j: (0, 0)),
            pl.BlockSpec((1, 2 * d), lambda i, j: (0, 0)),
        ],
        out_specs=pl.BlockSpec((1, tm, d), lambda i, j: (i, j, 0)),
        compiler_params=_cparams(("parallel", "parallel")),
        name="conv_in_proj",
    )(x, mod, gain, w, bias)


HALO = 32


def _dwconv_kernel(u_ref, halo_ref, w_ref, b_ref, lg_ref, lb_ref, o_ref, buf_sc, acc_sc, *, tm):
    i = pl.program_id(1)
    d = u_ref.shape[-1]
    buf_sc[0:HALO, :] = jnp.where(i > 0, halo_ref[0], 0.0)
    buf_sc[HALO:, :] = u_ref[0]
    off = HALO - (CONV_WIDTH - 1)
    cw, rw = LANES, 64
    for c in range(d // cw):
        cs = slice(c * cw, (c + 1) * cw)
        for r0 in range(0, tm, rw):
            acc = jnp.zeros((rw, cw), F32) + b_ref[:, cs]
            for j in range(CONV_WIDTH):
                acc = acc + w_ref[j:j + 1, cs] * buf_sc[off + j + r0:off + j + r0 + rw, cs]
            acc_sc[r0:r0 + rw, cs] = acc
    y = acc_sc[...]
    mu = jnp.mean(y, axis=-1, keepdims=True)
    yc = y - mu
    var = jnp.mean(yc * yc, axis=-1, keepdims=True)
    yn = yc * lax.rsqrt(var + NORM_EPS) * lg_ref[...] + lb_ref[...]
    o_ref[0] = (yn * _sigmoid(yn)).astype(BF16)


def _dwconv_ln_swish(u, dw_w, dw_b, ln_g, ln_b, tm):
    b, s, d = u.shape
    kern = functools.partial(_dwconv_kernel, tm=tm)
    hb = tm // HALO
    return pl.pallas_call(
        kern,
        out_shape=jax.ShapeDtypeStruct((b, s, d), BF16),
        grid=(b, s // tm),
        in_specs=[
            pl.BlockSpec((1, tm, d), lambda bi, i: (bi, i, 0)),
            pl.BlockSpec((1, HALO, d), lambda bi, i: (bi, jnp.maximum(i * hb - 1, 0), 0)),
            pl.BlockSpec((CONV_WIDTH, d), lambda bi, i: (0, 0)),
            pl.BlockSpec((1, d), lambda bi, i: (0, 0)),
            pl.BlockSpec((1, d), lambda bi, i: (0, 0)),
            pl.BlockSpec((1, d), lambda bi, i: (0, 0)),
        ],
        out_specs=pl.BlockSpec((1, tm, d), lambda bi, i: (bi, i, 0)),
        scratch_shapes=[pltpu.VMEM((tm + HALO, d), F32), pltpu.VMEM((tm, d), F32)],
        compiler_params=_cparams(("parallel", "parallel")),
        name="dwconv_ln_swish",
    )(u, u, dw_w, dw_b, ln_g, ln_b)


def _out_router_kernel(a1_ref, a2_ref, w1_ref, w2_ref, b_ref, x_ref, mod_ref, g_ref, rw_ref, rb_ref,
                       xo_ref, h_ref, idx_ref, wt_ref):
    m = mod_ref[0]
    y = (jnp.dot(a1_ref[0], w1_ref[...], preferred_element_type=F32)
         + jnp.dot(a2_ref[0], w2_ref[...], preferred_element_type=F32) + b_ref[...])
    x = x_ref[0] + m[2:3] * y
    xo_ref[0] = x
    h = _norm_mod(x, g_ref[...], m[4:5], m[3:4])
    chunks = h.shape[1] // LANES
    for c in range(chunks):
        h_ref[0, pl.ds(c, h.shape[0], stride=chunks), :] = h[:, c * LANES:(c + 1) * LANES]
    h_hi = h.astype(BF16)
    h_lo = (h - h_hi.astype(F32)).astype(BF16)
    rw_hi = rw_ref[:, :LANES]
    logits = (jnp.dot(h_hi, rw_hi, preferred_element_type=F32)
              + jnp.dot(h_lo, rw_hi, preferred_element_type=F32)
              + jnp.dot(h_hi, rw_ref[:, LANES:], preferred_element_type=F32)) + rb_ref[...]
    lane = lax.broadcasted_iota(I32, logits.shape, 1)
    lane_f = lane.astype(F32)
    idx_out = jnp.zeros(logits.shape, F32)
    val_out = jnp.full(logits.shape, NEG, F32)
    for k in range(TOP_K):
        best = jnp.max(logits, axis=-1, keepdims=True)
        pick = jnp.min(jnp.where(logits == best, lane_f, float(LANES)), axis=-1, keepdims=True)
        idx_out = jnp.where(lane == k, pick, idx_out)
        val_out = jnp.where(lane == k, best, val_out)
        logits = jnp.where(lane_f == pick, TAKEN, logits)
    e = jnp.where(lane < TOP_K, jnp.exp(val_out - jnp.max(val_out, axis=-1, keepdims=True)), 0.0)
    idx_ref[0] = idx_out.astype(I32)
    wt_ref[0] = e / jnp.sum(e, axis=-1, keepdims=True)


def _out_proj_router(a1, a2, c1, c2, w1, w2, bias, x, mod, gain, rw_parts, rb, tm):
    b, s, d = x.shape
    kw = w1.shape[0]
    tok = lambda i, j: (i, j, 0)
    const = lambda i, j: (0, 0)
    return pl.pallas_call(
        _out_router_kernel,
        out_shape=(jax.ShapeDtypeStruct((b, s, d), F32), jax.ShapeDtypeStruct((b, s * (d // LANES), LANES), F32),
                   jax.ShapeDtypeStruct((b, s, LANES), I32), jax.ShapeDtypeStruct((b, s, LANES), F32)),
        grid=(b, s // tm),
        in_specs=[
            pl.BlockSpec((1, tm, kw), lambda i, j: (i, j, c1)),
            pl.BlockSpec((1, tm, kw), lambda i, j: (i, j, c2)),
            pl.BlockSpec((kw, d), const),
            pl.BlockSpec((kw, d), const),
            pl.BlockSpec((1, d), const),
            pl.BlockSpec((1, tm, d), tok),
            pl.BlockSpec((1, 6, d), lambda i, j: (i, 0, 0)),
            pl.BlockSpec((1, d), const),
            pl.BlockSpec((d, 2 * LANES), const),
            pl.BlockSpec((1, LANES), const),
        ],
        out_specs=(pl.BlockSpec((1, tm, d), tok), pl.BlockSpec((1, tm * (d // LANES), LANES), tok),
                   pl.BlockSpec((1, tm, LANES), tok), pl.BlockSpec((1, tm, LANES), tok)),
        compiler_params=_cparams(("parallel", "parallel")),
        name="out_proj_router",
    )(a1, a2, w1, w2, bias, x, mod, gain, rw_parts, rb)


ROW_CHUNKS = D_MODEL // LANES
DEINT = 512
DUMP_ROWS = 512


def _expert_kernel(be_ref, nv_ref, gsrc0_ref, gsrc1_ref, gdstp_ref, gdst0_ref, h_ref, w1_ref, b1g_ref, b1l_ref,
                   w2_ref, b2_ref, pe_ref, po_ref, yt_ref, xbuf, ybuf, w1g_sc, w1l_sc, w2_sc, gsem, ssem,
                   *, n_blocks, dump_row):
    i = pl.program_id(0)
    cur = i % 2
    nxt = 1 - cur
    rows = MOE_BLOCK

    def tile_rows(first):
        return pl.ds(pl.multiple_of(first, ROW_CHUNKS), ROW_CHUNKS)

    def buf_rows(r):
        return pl.ds(r * ROW_CHUNKS, ROW_CHUNKS) if isinstance(r, int) else tile_rows(r * ROW_CHUNKS)

    def gather(idx_ref, slot, r):
        return pltpu.make_async_copy(h_ref.at[tile_rows(idx_ref[0, 0, r])], xbuf.at[slot, buf_rows(r)], gsem.at[slot])

    def scatter(dst_first, slot, r):
        return pltpu.make_async_copy(ybuf.at[slot, buf_rows(r)], yt_ref.at[tile_rows(dst_first)], ssem.at[slot])

    def wait_gather(slot):
        pltpu.make_async_copy(h_ref.at[pl.ds(0, rows * ROW_CHUNKS)], xbuf.at[slot], gsem.at[slot]).wait()

    def wait_scatter(slot):
        pltpu.make_async_copy(ybuf.at[slot], yt_ref.at[pl.ds(0, rows * ROW_CHUNKS)], ssem.at[slot]).wait()

    @pl.when(i == 0)
    def _():
        ybuf[...] = jnp.zeros(ybuf.shape, F32)

        def prime(r, c):
            gather(gsrc0_ref, 0, r).start()
            return c

        lax.fori_loop(0, rows, prime, 0)

    wait_gather(cur)

    @pl.when(i >= 1)
    def _():
        wait_scatter(cur)

    first = i == 0
    changed = jnp.logical_or(first, be_ref[i] != be_ref[jnp.maximum(i - 1, 0)])

    @pl.when(changed)
    def _():
        for c in range(w1_ref.shape[3] // DEINT):
            wc = w1_ref[0, 0, :, c * DEINT:(c + 1) * DEINT].astype(BF16)
            cs = slice(c * (DEINT // 2), (c + 1) * (DEINT // 2))
            w1g_sc[:, cs] = jnp.dot(wc, pe_ref[...], preferred_element_type=F32).astype(BF16)
            w1l_sc[:, cs] = jnp.dot(wc, po_ref[...], preferred_element_type=F32).astype(BF16)
        w2_sc[...] = w2_ref[0, 0].astype(BF16)

    for r in range(rows):
        gather(gsrc1_ref, nxt, r).start(priority=r % 2)
        scatter(jnp.where(first, dump_row + r * ROW_CHUNKS, gdstp_ref[0, 0, r]), nxt, r).start(priority=(r + 1) % 2)

    def chunk(c):
        return pl.ds(c, rows, stride=ROW_CHUNKS)

    x = jnp.concatenate([xbuf[cur, chunk(c), :] for c in range(ROW_CHUNKS)], axis=1)
    row = lax.broadcasted_iota(I32, x.shape, 0)
    xb = jnp.where(row < nv_ref[i], x, 0.0).astype(BF16)
    glu = jnp.dot(xb, w1g_sc[...], preferred_element_type=F32) + b1g_ref[0]
    lin = jnp.dot(xb, w1l_sc[...], preferred_element_type=F32) + b1l_ref[0]
    glu = jnp.minimum(glu, SWIGLU_LIMIT)
    lin = jnp.clip(lin, -SWIGLU_LIMIT, SWIGLU_LIMIT)
    act = glu * _sigmoid(SWIGLU_ALPHA * glu) * (lin + 1.0)
    y = jnp.dot(act.astype(BF16), w2_sc[...], preferred_element_type=F32) + b2_ref[0]
    for c in range(ROW_CHUNKS):
        ybuf[cur, chunk(c), :] = y[:, c * LANES:(c + 1) * LANES]

    @pl.when(i == n_blocks - 1)
    def _():
        def last(r, c):
            scatter(gdst0_ref[0, 0, r], cur, r).start()
            return c

        lax.fori_loop(0, rows, last, 0)
        wait_gather(nxt)
        wait_scatter(nxt)
        wait_scatter(cur)


def _experts(h, gsrc, gdst, block_expert, n_valid, layer, w1, b1g, b1l, w2, b2):
    t = h.shape[0] // ROW_CHUNKS
    n_blocks = gsrc.shape[0]
    d, f2 = w1.shape[2:]
    f = f2 // 2
    sel = np.arange(DEINT)[:, None] - 2 * np.arange(DEINT // 2)[None, :]
    p_even = jnp.asarray((sel == 0).astype(np.float32), dtype=BF16)
    p_odd = jnp.asarray((sel == 1).astype(np.float32), dtype=BF16)
    last = n_blocks - 1

    def wsel(i, be, nv):
        return (be[i], 0, 0)

    def wsel_layer(i, be, nv):
        return (layer, be[i], 0, 0)

    const = lambda i, be, nv: (0, 0)
    idx_blk = (1, 1, MOE_BLOCK)
    kern = functools.partial(_expert_kernel, n_blocks=n_blocks, dump_row=TOP_K * t * ROW_CHUNKS)
    return pl.pallas_call(
        kern,
        out_shape=jax.ShapeDtypeStruct(((TOP_K * t + DUMP_ROWS) * ROW_CHUNKS, LANES), F32),
        grid_spec=pltpu.PrefetchScalarGridSpec(
            num_scalar_prefetch=2,
            grid=(n_blocks,),
            in_specs=[
                pl.BlockSpec(idx_blk, lambda i, be, nv: (i, 0, 0), memory_space=pltpu.SMEM),
                pl.BlockSpec(idx_blk, lambda i, be, nv: (jnp.minimum(i + 1, last), 0, 0), memory_space=pltpu.SMEM),
                pl.BlockSpec(idx_blk, lambda i, be, nv: (jnp.maximum(i - 1, 0), 0, 0), memory_space=pltpu.SMEM),
                pl.BlockSpec(idx_blk, lambda i, be, nv: (i, 0, 0), memory_space=pltpu.SMEM),
                pl.BlockSpec(memory_space=pl.ANY),
                pl.BlockSpec((1, 1, d, f2), wsel_layer),
                pl.BlockSpec((1, 1, f), wsel),
                pl.BlockSpec((1, 1, f), wsel),
                pl.BlockSpec((1, 1, f, d), wsel_layer),
                pl.BlockSpec((1, 1, d), wsel),
                pl.BlockSpec((DEINT, DEINT // 2), const),
                pl.BlockSpec((DEINT, DEINT // 2), const),
            ],
            out_specs=pl.BlockSpec(memory_space=pl.ANY),
            scratch_shapes=[pltpu.VMEM((2, MOE_BLOCK * ROW_CHUNKS, LANES), F32),
                            pltpu.VMEM((2, MOE_BLOCK * ROW_CHUNKS, LANES), F32),
                            pltpu.VMEM((d, f), BF16), pltpu.VMEM((d, f), BF16), pltpu.VMEM((f, d), BF16),
                            pltpu.SemaphoreType.DMA((2,)), pltpu.SemaphoreType.DMA((2,))],
        ),
        compiler_params=_cparams(("arbitrary",)),
        name="moe_experts",
    )(block_expert, n_valid, gsrc, gsrc, gdst, gdst, h, w1, b1g, b1l, w2, b2, p_even, p_odd)


def _combine_kernel(y0_ref, y1_ref, y2_ref, y3_ref, wt_ref, x_ref, mod_ref, o_ref):
    wt = wt_ref[0]
    g2 = mod_ref[0][5:6]
    y_refs = (y0_ref, y1_ref, y2_ref, y3_ref)
    wk = [wt[:, k:k + 1] for k in range(TOP_K)]
    tm = wt.shape[0]
    for c in range(ROW_CHUNKS):
        chunk = pl.ds(c, tm, stride=ROW_CHUNKS)
        y = wk[0] * y_refs[0][chunk, :]
        for k in range(1, TOP_K):
            y = y + wk[k] * y_refs[k][chunk, :]
        cs = slice(c * LANES, (c + 1) * LANES)
        o_ref[0, :, cs] = x_ref[0, :, cs] + g2[:, cs] * y


def _combine(yt, wts, x, mod, tm):
    b, s, d = x.shape
    nt = s // tm
    tiles = b * nt
    tok = lambda i, j: (i, j, 0)

    def yspec(k):
        return pl.BlockSpec((tm * ROW_CHUNKS, LANES), lambda i, j: (k * tiles + i * nt + j, 0))

    return pl.pallas_call(
        _combine_kernel,
        out_shape=jax.ShapeDtypeStruct((b, s, d), F32),
        grid=(b, nt),
        in_specs=[yspec(k) for k in range(TOP_K)] + [
            pl.BlockSpec((1, tm, LANES), tok),
            pl.BlockSpec((1, tm, d), tok),
            pl.BlockSpec((1, 6, d), lambda i, j: (i, 0, 0)),
        ],
        out_specs=pl.BlockSpec((1, tm, d), tok),
        compiler_params=_cparams(("arbitrary", "arbitrary")),
        name="moe_combine",
    )(yt, yt, yt, yt, wts, x, mod)


def _route_slots(top_idx, n_blocks):
    t = top_idx.shape[0]
    n_assign = t * TOP_K
    n_slots = n_blocks * MOE_BLOCK
    sorted_e, order = lax.sort_key_val(top_idx.reshape(-1), jnp.arange(n_assign, dtype=I32))
    experts = jnp.arange(N_EXPERTS + 1, dtype=I32)
    start = jnp.sum((sorted_e[None, :] < experts[:, None]).astype(I32), axis=1)
    counts = start[1:] - start[:-1]
    padded = (counts + MOE_BLOCK - 1) // MOE_BLOCK * MOE_BLOCK
    pad_end = jnp.cumsum(padded)
    pad_start = pad_end - padded
    blk_start = jnp.arange(n_blocks, dtype=I32) * MOE_BLOCK
    block_expert = jnp.minimum(jnp.sum((pad_end[None, :] <= blk_start[:, None]).astype(I32), axis=1), N_EXPERTS - 1)
    n_valid = jnp.clip(pad_start[block_expert] + counts[block_expert] - blk_start, 0, MOE_BLOCK).astype(I32)
    within = jnp.arange(MOE_BLOCK, dtype=I32)[None, :]
    valid = within < n_valid[:, None]
    rank = (blk_start - pad_start[block_expert] + start[block_expert])[:, None] + within
    slot_a = order[jnp.clip(rank, 0, n_assign - 1)]
    slot = blk_start[:, None] + within
    gsrc = jnp.where(valid, slot_a // TOP_K, 0) * ROW_CHUNKS
    gdst = jnp.where(valid, (slot_a % TOP_K) * t + slot_a // TOP_K, n_assign + slot % DUMP_ROWS) * ROW_CHUNKS
    shape = (n_blocks, 1, MOE_BLOCK)
    return gsrc.reshape(shape), gdst.reshape(shape), block_expert, n_valid


def _moe(x_mid, h, idx, wts, mod, layer, w1, b1, w2, b2, tm):
    b, s, d = x_mid.shape
    t = b * s
    n_blocks = -(-t * TOP_K // MOE_BLOCK) + N_EXPERTS
    top_idx = idx.reshape(t, LANES)[:, :TOP_K]
    gsrc, gdst, block_expert, n_valid = _route_slots(top_idx, n_blocks)
    yt = _experts(h.reshape(t * ROW_CHUNKS, LANES), gsrc, gdst, block_expert, n_valid, layer, w1,
                  b1[:, None, 0::2], b1[:, None, 1::2], w2, b2[:, None, :])
    return _combine(yt, wts, x_mid, mod, tm)


def _pad_cols(w, n):
    return jnp.pad(w, ((0, 0), (0, n - w.shape[-1])))


def _attn_colgain(diff_qk_gain, nsa_q_gain, nsa_k_gain):
    scale = HEAD_DIM ** -0.5 * LOG2E
    ones = jnp.ones((LANES,), F32)
    parts = [jnp.tile(diff_qk_gain[0] * scale, 8), jnp.tile(diff_qk_gain[1], 8), jnp.ones((512,), F32),
             jnp.tile(nsa_q_gain * scale, 8), ones, ones, jnp.tile(nsa_k_gain[1], 2), ones,
             jnp.tile(nsa_k_gain[2], 2), ones, ones]
    return jnp.concatenate(parts).reshape(1, W_IN_PAD)


def _overlap_matrix(s):
    n_sel = s // SEL_BLOCK
    ncp = s // CMP_STRIDE
    c_start = np.arange(ncp) * CMP_STRIDE
    s_start = np.arange(n_sel) * SEL_BLOCK
    ovl = (c_start[None, :] < s_start[:, None] + SEL_BLOCK) & (c_start[None, :] + CMP_BLOCK > s_start[:, None])
    ovl[:, ncp - 1] = False
    return jnp.asarray(ovl.astype(np.float32), dtype=BF16)


def _attention_layer(x, mod, gain, w_in, w_out, diff_qk_gain, diff_lambda, diff_subln, nsa_q_gain, nsa_k_gain,
                     cmp_pos, cmp_w1, cmp_w2, lam_init):
    b, s, d = x.shape
    n_gate = NSA_Q_HEADS * 3
    gate_perm = np.arange(n_gate).reshape(NSA_KV_HEADS, NSA_GROUP, 3).transpose(0, 2, 1).reshape(-1)
    w_pad = jnp.concatenate([w_in[:, :COL_GATE], w_in[:, COL_GATE + gate_perm],
                             jnp.zeros((d, W_IN_PAD - COL_GATE - n_gate), w_in.dtype)], axis=1).astype(BF16)
    zb, dqT, dvT, nqT, vselT, vwinT, gT, chunks = _attn_in_proj(
        x, mod, gain, w_pad, _attn_colgain(diff_qk_gain, nsa_q_gain, nsa_k_gain))

    o_diff = _diff_attention(zb, dqT.reshape(b, DIFF_HEADS, 2, HEAD_DIM, s), dvT,
                             _score_bound(diff_qk_gain[0], diff_qk_gain[1]), diff_lambda, diff_subln, lam_init,
                             DIFF_TQ, ATTN_TM)

    ncp = s // CMP_STRIDE
    cmp_out = _compress(chunks, cmp_pos.reshape(2, 1, CMP_BLOCK * HEAD_DIM), cmp_w1, cmp_w2, nsa_k_gain[0:1])
    kc = cmp_out[:, 0:2].transpose(0, 2, 1, 3).reshape(b, ncp, LANES).astype(BF16)
    vcT = cmp_out[:, 2:4].transpose(0, 1, 3, 2).astype(BF16)
    gatesT = gT[:, :n_gate].reshape(b, NSA_KV_HEADS, 3, NSA_GROUP, s)
    shifts = jnp.stack([_score_bound(nsa_q_gain, nsa_k_gain[1]), _score_bound(nsa_q_gain, nsa_k_gain[2])])
    o_nsa = _nsa_attention(zb, nqT.reshape(b, NSA_KV_HEADS, NSA_GROUP, HEAD_DIM, s), vselT, vwinT, kc, vcT,
                           _overlap_matrix(s), gatesT, shifts, NSA_TILE)
    return o_diff, o_nsa


def kernel(x, c, mod_w, mod_b, norm_mix, norm_ffn, attn_w_in, attn_w_out, diff_qk_gain, diff_lambda, diff_subln,
           nsa_q_gain, nsa_k_gain, nsa_cmp_pos, nsa_cmp_w1, nsa_cmp_w2, conv_pw1_w, conv_pw1_b, conv_dw_w,
           conv_dw_b, conv_ln_g, conv_ln_b, conv_pw2_w, conv_pw2_b, router_w, router_b, moe_w1, moe_b1, moe_w2,
           moe_b2):
    b, s, d = x.shape
    depth = mod_w.shape[0]
    tm = 512 if s % 512 == 0 else 256
    mods = _modulation(c, mod_w, mod_b).reshape(depth, b, 6, d)
    half = d // 2
    for i in range(depth):
        mod = mods[i]
        j = i // 2
        if i % 2 == 0:
            lam_init = 0.8 - 0.6 * math.exp(-0.3 * i)
            o_diff, o_nsa = _attention_layer(
                x, mod, norm_mix[i:i + 1], attn_w_in[j], attn_w_out[j], diff_qk_gain[j], diff_lambda[j],
                diff_subln[j], nsa_q_gain[j], nsa_k_gain[j], nsa_cmp_pos[j], nsa_cmp_w1[j], nsa_cmp_w2[j], lam_init)
            a1, a2, c1, c2 = o_diff, o_nsa, 0, 0
            w_o = attn_w_out[j].astype(BF16)
            bias = jnp.zeros((1, d), F32)
        else:
            u = _conv_in_proj(x, mod, norm_mix[i:i + 1], conv_pw1_w[j].astype(BF16), conv_pw1_b[j:j + 1], tm)
            v = _dwconv_ln_swish(u, conv_dw_w[j].reshape(CONV_WIDTH, d), conv_dw_b[j:j + 1], conv_ln_g[j:j + 1],
                                 conv_ln_b[j:j + 1], 256)
            a1, a2, c1, c2 = v, v, 0, 1
            w_o = conv_pw2_w[j].astype(BF16)
            bias = conv_pw2_b[j:j + 1]
        rw = _pad_cols(router_w[i], LANES)
        rw_hi = rw.astype(BF16)
        rw_lo = (rw - rw_hi.astype(F32)).astype(BF16)
        rb = jnp.concatenate([router_b[i], jnp.full((LANES - N_EXPERTS,), NEG, F32)]).reshape(1, LANES)
        x_mid, h, idx, wts = _out_proj_router(a1, a2, c1, c2, w_o[:half], w_o[half:], bias, x, mod,
                                              norm_ffn[i:i + 1], jnp.concatenate([rw_hi, rw_lo], axis=1), rb, tm)
        x = _moe(x_mid, h, idx, wts, mod, i, moe_w1, moe_b1[i], moe_w2, moe_b2[i], 256)
    return x
```

```python
import functools
import math

import jax
import jax.numpy as jnp
import numpy as np
from jax import lax
from jax.experimental import pallas as pl
from jax.experimental.pallas import tpu as pltpu

F32 = jnp.float32
BF16 = jnp.bfloat16
I32 = jnp.int32

D_MODEL = 1024
HEAD_DIM = 64
DIFF_HEADS = 4
NSA_Q_HEADS = 8
NSA_KV_HEADS = 2
NSA_GROUP = NSA_Q_HEADS // NSA_KV_HEADS
CMP_BLOCK = 32
CMP_STRIDE = 16
CMP_HIDDEN = 256
SEL_BLOCK = 64
SEL_TOP = 16
SEL_FORCED_SCORE = 1.0e4
WINDOW = 512
CONV_WIDTH = 31
N_EXPERTS = 32
TOP_K = 4
SWIGLU_ALPHA = 1.702
SWIGLU_LIMIT = 7.0
MOE_BLOCK = 256
NORM_EPS = 1e-6

LANES = 128
NEG = -1e30
TAKEN = -3e38
VMEM_LIMIT = 56 * 1024 * 1024

COL_DQ, COL_DK, COL_DV, COL_NQ = 0, 512, 1024, 1536
COL_KCMP, COL_VCMP, COL_KSEL, COL_VSEL, COL_KWIN, COL_VWIN = 2048, 2176, 2304, 2432, 2560, 2688
COL_GATE = 2816
W_IN_PAD = 2944
NORM_SLABS = frozenset(list(range(0, 8)) + list(range(12, 16)) + [COL_KSEL // LANES, COL_KWIN // LANES])


def _cparams(sem):
    return pltpu.CompilerParams(dimension_semantics=sem, vmem_limit_bytes=VMEM_LIMIT)


def _sigmoid(v):
    return 1.0 / (1.0 + jnp.exp(-v))


def _norm_mod(x, g, sc, sh):
    ms = jnp.mean(x * x, axis=-1, keepdims=True)
    return (x * lax.rsqrt(ms + NORM_EPS)) * g * (1.0 + sc) + sh


def _mod_kernel(c_ref, w_ref, b_ref, o_ref):
    c = c_ref[...]
    cond = c * _sigmoid(c)
    o_ref[0] = jnp.dot(cond, w_ref[0], preferred_element_type=F32, precision=lax.Precision.HIGHEST) + b_ref[0]


def _modulation(c, mod_w, mod_b):
    depth, d, n = mod_w.shape
    b = c.shape[0]
    tn = 1536
    return pl.pallas_call(
        _mod_kernel,
        out_shape=jax.ShapeDtypeStruct((depth, b, n), F32),
        grid=(depth, n // tn),
        in_specs=[
            pl.BlockSpec((b, d), lambda i, j: (0, 0)),
            pl.BlockSpec((1, d, tn), lambda i, j: (i, 0, j)),
            pl.BlockSpec((1, 1, tn), lambda i, j: (i, 0, j)),
        ],
        out_specs=pl.BlockSpec((1, b, tn), lambda i, j: (i, 0, j)),
        compiler_params=_cparams(("parallel", "parallel")),
        name="modulation",
    )(c, mod_w, mod_b.reshape(depth, 1, n))


KZ_DK, KZ_KSEL, KZ_KWIN, KZ_WIDTH = 0, 512, 640, 768
KEY_COLS = {COL_DK + 128 * u: KZ_DK + 128 * u for u in range(4)}
KEY_COLS.update({COL_KSEL: KZ_KSEL, COL_KWIN: KZ_KWIN})
ATTN_TM = 512
DIFF_TQ = 1024
NSA_TILE = 512
GATE_ROWS = 32


def _in_attn_kernel(x_ref, mod_ref, g_ref, w_ref, cg_ref, z_ref, dqT_ref, dvT_ref, nqT_ref, vsT_ref, vwT_ref, gT_ref,
                    ch_ref, cmp_sc):
    m = mod_ref[0]
    h = _norm_mod(x_ref[0], g_ref[...], m[1:2], m[0:1]).astype(BF16)
    lo = lax.broadcasted_iota(I32, (1, LANES), 1) < HEAD_DIM
    groups = ((0, 512), (512, 1024), (1024, 1536), (1536, 2048), (2048, 2816), (2816, 2944))
    halves = ATTN_TM // NSA_TILE
    for c0, c1 in groups:
        z = jnp.dot(h, w_ref[:, c0:c1], preferred_element_type=F32)
        for s in range((c1 - c0) // LANES):
            a0 = c0 + s * LANES
            zs = z[:, s * LANES:(s + 1) * LANES]
            if a0 == COL_GATE:
                gT_ref[0] = _sigmoid(zs).T[:GATE_ROWS]
                continue
            if a0 // LANES in NORM_SLABS:
                z2 = zs * zs
                s_lo = jnp.sum(jnp.where(lo, z2, 0.0), axis=-1, keepdims=True)
                s_hi = jnp.sum(jnp.where(lo, 0.0, z2), axis=-1, keepdims=True)
                inv = jnp.where(lo, lax.rsqrt(s_lo * (1.0 / HEAD_DIM) + NORM_EPS),
                                lax.rsqrt(s_hi * (1.0 / HEAD_DIM) + NORM_EPS))
                zs = zs * inv
            zs = zs * cg_ref[:, a0:a0 + LANES]
            if COL_DQ <= a0 < COL_DK:
                dqT_ref[0, a0 - COL_DQ:a0 - COL_DQ + LANES, :] = zs.T.astype(BF16)
            elif COL_DV <= a0 < COL_NQ:
                dvT_ref[0, 0, a0 - COL_DV:a0 - COL_DV + LANES, :] = zs.T.astype(BF16)
            elif COL_NQ <= a0 < COL_KCMP:
                nqT_ref[0, a0 - COL_NQ:a0 - COL_NQ + LANES, :] = zs.T.astype(BF16)
            elif a0 in (COL_VSEL, COL_VWIN):
                zt = zs.T.astype(BF16)
                ref = vsT_ref if a0 == COL_VSEL else vwT_ref
                for u in range(halves):
                    ref[0, u] = zt[:, u * NSA_TILE:(u + 1) * NSA_TILE]
            elif a0 in (COL_KCMP, COL_VCMP):
                cmp_sc[...] = zs
                first = 0 if a0 == COL_KCMP else NSA_KV_HEADS
                n_rows = zs.shape[0] // CMP_STRIDE
                for pair in range(CMP_STRIDE // 2):
                    even = cmp_sc[pl.ds(2 * pair, n_rows, stride=CMP_STRIDE), :]
                    odd = cmp_sc[pl.ds(2 * pair + 1, n_rows, stride=CMP_STRIDE), :]
                    cs = slice(pair * LANES, (pair + 1) * LANES)
                    ch_ref[0, first, :, cs] = jnp.where(lo, even, pltpu.roll(odd, HEAD_DIM, 1)).astype(BF16)
                    ch_ref[0, first + 1, :, cs] = jnp.where(lo, pltpu.roll(even, HEAD_DIM, 1), odd).astype(BF16)
            else:
                k0 = KEY_COLS[a0]
                z_ref[0, :, k0:k0 + LANES] = zs.astype(BF16)


def _attn_in_proj(x, mod, gain, w_pad, colgain):
    b, s, d = x.shape
    tm = ATTN_TM
    halves = tm // NSA_TILE
    tok = lambda i, j: (i, j, 0)
    featT = lambda i, j: (i, 0, j)
    return pl.pallas_call(
        _in_attn_kernel,
        out_shape=(jax.ShapeDtypeStruct((b, s, KZ_WIDTH), BF16),
                   jax.ShapeDtypeStruct((b, 512, s), BF16),
                   jax.ShapeDtypeStruct((b, s // tm, 512, tm), BF16),
                   jax.ShapeDtypeStruct((b, 512, s), BF16),
                   jax.ShapeDtypeStruct((b, s // NSA_TILE, LANES, NSA_TILE), BF16),
                   jax.ShapeDtypeStruct((b, s // NSA_TILE, LANES, NSA_TILE), BF16),
                   jax.ShapeDtypeStruct((b, GATE_ROWS, s), F32),
                   jax.ShapeDtypeStruct((b, 2 * NSA_KV_HEADS, s // CMP_STRIDE, CMP_STRIDE * HEAD_DIM), BF16)),
        grid=(b, s // tm),
        in_specs=[
            pl.BlockSpec((1, tm, d), tok),
            pl.BlockSpec((1, 6, d), lambda i, j: (i, 0, 0)),
            pl.BlockSpec((1, d), lambda i, j: (0, 0)),
            pl.BlockSpec((d, W_IN_PAD), lambda i, j: (0, 0)),
            pl.BlockSpec((1, W_IN_PAD), lambda i, j: (0, 0)),
        ],
        out_specs=(pl.BlockSpec((1, tm, KZ_WIDTH), tok),
                   pl.BlockSpec((1, 512, tm), featT),
                   pl.BlockSpec((1, 1, 512, tm), lambda i, j: (i, j, 0, 0)),
                   pl.BlockSpec((1, 512, tm), featT),
                   pl.BlockSpec((1, halves, LANES, NSA_TILE), lambda i, j: (i, j, 0, 0)),
                   pl.BlockSpec((1, halves, LANES, NSA_TILE), lambda i, j: (i, j, 0, 0)),
                   pl.BlockSpec((1, GATE_ROWS, tm), featT),
                   pl.BlockSpec((1, 2 * NSA_KV_HEADS, tm // CMP_STRIDE, CMP_STRIDE * HEAD_DIM),
                                lambda i, j: (i, 0, j, 0))),
        scratch_shapes=[pltpu.VMEM((tm, LANES), F32)],
        compiler_params=_cparams(("parallel", "parallel")),
        name="attn_in_proj",
    )(x, mod, gain, w_pad, colgain)


FIXED_SHIFT_LIMIT = 57.0
FIXED_TILES_PER_STEP = 4
LOG2E = math.log2(math.e)


def _score_bound(gain_q, gain_k):
    return (math.sqrt(HEAD_DIM) * LOG2E) * jnp.max(jnp.abs(gain_q * gain_k))


SUM_ROWS = 8


def _with_ones(vt):
    return jnp.concatenate([vt, jnp.ones((SUM_ROWS, vt.shape[1]), vt.dtype)], axis=0)


def _fixed_update(ss, vts, shift, acc_ref):
    ps = [jnp.exp2(s - shift).astype(BF16) for s in ss]
    acc_ref[...] += sum(jnp.dot(_with_ones(vt), p, preferred_element_type=F32) for vt, p in zip(vts, ps))


def _online_update(ss, vts, m_ref, acc_ref):
    for s, vt in zip(ss, vts):
        m_old = m_ref[...]
        m_new = jnp.maximum(m_old, jnp.max(s, axis=0, keepdims=True))
        alpha = jnp.exp2(m_old - m_new)
        p = jnp.exp2(s - m_new).astype(BF16)
        acc_ref[...] = alpha * acc_ref[...] + jnp.dot(_with_ones(vt), p, preferred_element_type=F32)
        m_ref[...] = m_new


def _tile_loops(n_full, tiles, width, n_masked=1):
    def group(jj, c):
        tiles(tuple(width * jj + u for u in range(width)), (False,) * width)
        return c

    lax.fori_loop(0, n_full // width, group, 0)
    rem = n_full % width
    base = n_full - rem
    for r in range(width):
        @pl.when(rem == r)
        def _():
            tiles(tuple(base + u for u in range(r + n_masked)), (False,) * r + (True,) * n_masked)


def _diff_kernel(mb_ref, lam_ref, qT_ref, k_ref, vT_ref, sub_ref, o_ref, m0, m1, a0, a1, *, tq, tk, lam_init):
    i = pl.program_id(2)
    m_refs, acc_refs = (m0, m1), (a0, a1)
    q = qT_ref[0, 0]
    zero = jnp.zeros((HEAD_DIM, tq), BF16)
    qp = (jnp.concatenate([q[0], zero], axis=0), jnp.concatenate([zero, q[1]], axis=0))
    for mm in range(2):
        m_refs[mm][...] = jnp.full(m_refs[mm].shape, NEG, F32)
        acc_refs[mm][...] = jnp.zeros(acc_refs[mm].shape, F32)
    shift = mb_ref[0]
    dv = 2 * HEAD_DIM

    def tiles(js, masks, fixed):
        kts = [k_ref[0, pl.ds(pl.multiple_of(j * tk, tk), tk), :] for j in js]
        vts = [vT_ref[0, j] for j in js]
        for mm in range(2):
            ss = []
            for j, kt, masked in zip(js, kts, masks):
                s = jnp.dot(kt, qp[mm], preferred_element_type=F32)
                if masked:
                    kpos = j * tk + lax.broadcasted_iota(I32, (tk, tq), 0)
                    t = i * tq + lax.broadcasted_iota(I32, (tk, tq), 1)
                    s = jnp.where(kpos <= t, s, NEG)
                ss.append(s)
            if fixed:
                _fixed_update(ss, vts, shift, acc_refs[mm])
            else:
                _online_update(ss, vts, m_refs[mm], acc_refs[mm])

    n_full = (i * tq) // tk
    n_masked = max(tq // tk, 1)

    @pl.when(shift <= FIXED_SHIFT_LIMIT)
    def _():
        _tile_loops(n_full, functools.partial(tiles, fixed=True), FIXED_TILES_PER_STEP, n_masked)

    @pl.when(shift > FIXED_SHIFT_LIMIT)
    def _():
        _tile_loops(n_full, functools.partial(tiles, fixed=False), 1, n_masked)

    lv = lam_ref[...]
    lam = (jnp.exp(jnp.sum(lv[0:1] * lv[1:2], axis=-1, keepdims=True))
           - jnp.exp(jnp.sum(lv[2:3] * lv[3:4], axis=-1, keepdims=True)) + lam_init)
    o = a0[:dv] / a0[dv:dv + 1] - lam * (a1[:dv] / a1[dv:dv + 1])
    ms = jnp.mean(o * o, axis=0, keepdims=True)
    o = o * lax.rsqrt(ms + NORM_EPS) * sub_ref[...] * (1.0 - lam_init)
    o_ref[0] = o.T.astype(BF16)


def _diff_attention(zb, qT, vT, shift, lam_vecs, subln, lam_init, tq, tk):
    b, s, _ = zb.shape
    kern = functools.partial(_diff_kernel, tq=tq, tk=tk, lam_init=lam_init)
    dv = 2 * HEAD_DIM
    return pl.pallas_call(
        kern,
        out_shape=jax.ShapeDtypeStruct((b, s, DIFF_HEADS * dv), BF16),
        grid=(b, DIFF_HEADS, s // tq),
        in_specs=[
            pl.BlockSpec(memory_space=pltpu.SMEM),
            pl.BlockSpec((4, HEAD_DIM), lambda bi, h, i: (0, 0)),
            pl.BlockSpec((1, 1, 2, HEAD_DIM, tq), lambda bi, h, i: (bi, h, 0, 0, i)),
            pl.BlockSpec((1, s, LANES), lambda bi, h, i: (bi, 0, KZ_DK // LANES + h)),
            pl.BlockSpec((1, s // tk, 2 * HEAD_DIM, tk), lambda bi, h, i: (bi, 0, h, 0)),
            pl.BlockSpec((2 * HEAD_DIM, 1), lambda bi, h, i: (0, 0)),
        ],
        out_specs=pl.BlockSpec((1, tq, LANES), lambda bi, h, i: (bi, i, h)),
        scratch_shapes=[pltpu.VMEM((1, tq), F32)] * 2 + [pltpu.VMEM((dv + SUM_ROWS, tq), F32)] * 2,
        compiler_params=_cparams(("parallel", "parallel", "arbitrary")),
        name="diff_attention",
    )(shift.reshape(1), lam_vecs, qT, zb, vT, subln.reshape(-1, 1))


def _gelu_tanh(v):
    return 0.5 * v * (1.0 + jnp.tanh(math.sqrt(2.0 / math.pi) * (v + 0.044715 * (v * v * v))))


def _cmp_kernel(c_ref, pos_ref, w1_ref, w2_ref, kg_ref, o_ref, *, n_cmp):
    jh = pl.program_id(1)
    half = CMP_STRIDE * HEAD_DIM
    c = c_ref[0, 0]
    w1a = w1_ref[0, :half, :].astype(BF16)
    w1b = w1_ref[0, half:, :].astype(BF16)
    pos = jnp.broadcast_to(pos_ref[0], (8, 2 * half)).astype(BF16)
    bias = (jnp.dot(pos[:, :half], w1a, preferred_element_type=F32)
            + jnp.dot(pos[:, half:], w1b, preferred_element_type=F32))[0:1]
    u = jnp.dot(c, w1a, preferred_element_type=F32)
    v = jnp.dot(c, w1b, preferred_element_type=F32)
    ncp = u.shape[0]
    hid = _gelu_tanh(u + pltpu.roll(v, ncp - 1, 0) + bias)
    y = jnp.dot(hid.astype(BF16), w2_ref[0].astype(BF16), preferred_element_type=F32)
    yn = y * lax.rsqrt(jnp.mean(y * y, axis=-1, keepdims=True) + NORM_EPS) * kg_ref[...]
    y = jnp.where(jh < NSA_KV_HEADS, yn, y)
    row = lax.broadcasted_iota(I32, y.shape, 0)
    o_ref[0, 0] = jnp.where(row < n_cmp, y, 0.0)


def _compress(chunks, pos, w1, w2, kgain):
    b, _, ncp, cd = chunks.shape
    kern = functools.partial(_cmp_kernel, n_cmp=ncp - 1)
    return pl.pallas_call(
        kern,
        out_shape=jax.ShapeDtypeStruct((b, 4, ncp, HEAD_DIM), F32),
        grid=(b, 4),
        in_specs=[
            pl.BlockSpec((1, 1, ncp, cd), lambda bi, j: (bi, j, 0, 0)),
            pl.BlockSpec((1, 1, 2 * cd), lambda bi, j: (j // NSA_KV_HEADS, 0, 0)),
            pl.BlockSpec((1, 2 * cd, CMP_HIDDEN), lambda bi, j: (j // NSA_KV_HEADS, 0, 0)),
            pl.BlockSpec((1, CMP_HIDDEN, HEAD_DIM), lambda bi, j: (j // NSA_KV_HEADS, 0, 0)),
            pl.BlockSpec((1, HEAD_DIM), lambda bi, j: (0, 0)),
        ],
        out_specs=pl.BlockSpec((1, 1, ncp, HEAD_DIM), lambda bi, j: (bi, j, 0, 0)),
        compiler_params=_cparams(("parallel", "parallel")),
        name="nsa_compress",
    )(chunks, pos, w1, w2, kgain)


def _nsa_kernel(mb_ref, qT_ref, ksel_ref, vselT_ref, kwin_ref, vwinT_ref, kc_ref, vcT_ref, ovl_ref, gate_ref, o_ref,
                selb_sc, m_s, m_w, a_s, a_w, *, tq, n_top):
    hk = pl.program_id(1)
    i = pl.program_id(2)
    g4 = NSA_GROUP
    nq = g4 * tq
    q4 = jnp.concatenate([qT_ref[0, 0, g] for g in range(g4)], axis=1)
    half = lax.broadcasted_iota(I32, (2 * HEAD_DIM, nq), 0) // HEAD_DIM
    qp = jnp.where(half == hk, jnp.concatenate([q4, q4], axis=0), jnp.zeros((), BF16))
    t1 = i * tq + lax.broadcasted_iota(I32, (1, tq), 1)

    def tile4(a):
        return jnp.concatenate([a] * g4, axis=1)

    ncp = kc_ref.shape[1]
    sc = jnp.dot(kc_ref[0], qp, preferred_element_type=F32)
    cend = CMP_STRIDE * lax.broadcasted_iota(I32, (ncp, tq), 0) + (CMP_BLOCK - 1)
    sc = sc + tile4(jnp.where(cend <= t1, 0.0, NEG))
    e = jnp.where(sc > 0.5 * NEG, jnp.exp2(sc - jnp.max(sc, axis=0, keepdims=True)), 0.0)
    p = e / jnp.maximum(jnp.sum(e, axis=0, keepdims=True), 1e-30)
    o_c = jnp.dot(vcT_ref[0, 0], p.astype(BF16), preferred_element_type=F32)

    psum = p[:, 0:tq]
    for g in range(1, g4):
        psum = psum + p[:, g * tq:(g + 1) * tq]
    p_hi = psum.astype(BF16)
    p_lo = (psum - p_hi.astype(F32)).astype(BF16)
    imp = (jnp.dot(ovl_ref[...], p_hi, preferred_element_type=F32)
           + jnp.dot(ovl_ref[...], p_lo, preferred_element_type=F32))
    n_sel = imp.shape[0]
    jrow = lax.broadcasted_iota(I32, (n_sel, tq), 0)
    cur = t1 // SEL_BLOCK
    forced = (jrow == 0) | (jrow == cur) | (jrow == cur - 1)
    score = jnp.where(forced, SEL_FORCED_SCORE, jnp.where(jrow <= cur, imp, NEG))
    selb = jnp.full((n_sel, tq), NEG, F32)
    jrow_f = jrow.astype(F32)
    for _ in range(n_top):
        best = jnp.max(score, axis=0, keepdims=True)
        pick = jnp.min(jnp.where(score == best, jrow_f, float(n_sel)), axis=0, keepdims=True)
        hit = jrow_f == pick
        selb = jnp.where(hit, 0.0, selb)
        score = jnp.where(hit, TAKEN, score)
    selb_sc[...] = selb

    for ref in (m_s, m_w):
        ref[...] = jnp.full(ref.shape, NEG, F32)
    for ref in (a_s, a_w):
        ref[...] = jnp.zeros(ref.shape, F32)
    tk = tq
    bpt = tk // SEL_BLOCK
    krow = lax.broadcasted_iota(I32, (tk, tq), 0)
    shift_s, shift_w = mb_ref[0], mb_ref[1]
    fixed = jnp.maximum(shift_s, shift_w) <= FIXED_SHIFT_LIMIT

    def sel_tiles(js, masks, fixed):
        ss = []
        for j, masked in zip(js, masks):
            kt = ksel_ref[0, pl.ds(pl.multiple_of(j * tk, tk), tk), :]
            rows = [jnp.broadcast_to(selb_sc[pl.ds(j * bpt + r, 1), :], (SEL_BLOCK, tq)) for r in range(bpt)]
            bias = jnp.concatenate(rows, axis=0)
            if masked:
                bias = jnp.where(j * tk + krow <= t1, bias, NEG)
            ss.append(jnp.dot(kt, qp, preferred_element_type=F32) + tile4(bias))
        vts = [vselT_ref[0, j] for j in js]
        if fixed:
            _fixed_update(ss, vts, shift_s, a_s)
        else:
            _online_update(ss, vts, m_s, a_s)

    def win_tiles(js, fixed):
        ss = []
        for j in js:
            kt = kwin_ref[0, pl.ds(pl.multiple_of(j * tk, tk), tk), :]
            kpos = j * tk + krow
            bias = jnp.where(kpos <= t1, jnp.where(kpos > t1 - WINDOW, 0.0, NEG), NEG)
            ss.append(jnp.dot(kt, qp, preferred_element_type=F32) + tile4(bias))
        vts = [vwinT_ref[0, j] for j in js]
        if fixed:
            _fixed_update(ss, vts, shift_w, a_w)
        else:
            _online_update(ss, vts, m_w, a_w)

    max_win = (WINDOW + tk - 1) // tk + 1
    n_win = jnp.minimum(i + 1, max_win)
    for use_fixed in (True, False):
        @pl.when(fixed == use_fixed)
        def _():
            _tile_loops(i, functools.partial(sel_tiles, fixed=use_fixed), FIXED_TILES_PER_STEP if use_fixed else 1)
            for c in range(1, max_win + 1):
                @pl.when(n_win == c)
                def _():
                    win_tiles(tuple(i - (c - 1) + u for u in range(c)), use_fixed)

    def gate_row(br):
        gt = gate_ref[0, 0, br]
        return jnp.concatenate([gt[g:g + 1] for g in range(g4)], axis=1)

    dv = HEAD_DIM
    out = (o_c * gate_row(0) + (a_s[:dv] / a_s[dv:dv + 1]) * gate_row(1) + (a_w[:dv] / a_w[dv:dv + 1]) * gate_row(2))
    stacked = jnp.concatenate([out[:, g * tq:(g + 1) * tq] for g in range(g4)], axis=0)
    o_ref[0] = stacked.T.astype(BF16)


def _nsa_attention(zb, qT, vselT, vwinT, kc, vcT, ovl, gatesT, shifts, tq):
    b, s, _ = zb.shape
    nt = s // tq
    n_sel = s // SEL_BLOCK
    ncp = kc.shape[1]
    kern = functools.partial(_nsa_kernel, tq=tq, n_top=min(SEL_TOP, n_sel))
    gd = NSA_GROUP * HEAD_DIM
    nq = NSA_GROUP * tq
    return pl.pallas_call(
        kern,
        out_shape=jax.ShapeDtypeStruct((b, s, NSA_Q_HEADS * HEAD_DIM), BF16),
        grid=(b, NSA_KV_HEADS, nt),
        in_specs=[
            pl.BlockSpec(memory_space=pltpu.SMEM),
            pl.BlockSpec((1, 1, NSA_GROUP, HEAD_DIM, tq), lambda bi, h, i: (bi, h, 0, 0, i)),
            pl.BlockSpec((1, s, LANES), lambda bi, h, i: (bi, 0, KZ_KSEL // LANES)),
            pl.BlockSpec((1, nt, HEAD_DIM, tq), lambda bi, h, i: (bi, 0, h, 0)),
            pl.BlockSpec((1, s, LANES), lambda bi, h, i: (bi, 0, KZ_KWIN // LANES)),
            pl.BlockSpec((1, nt, HEAD_DIM, tq), lambda bi, h, i: (bi, 0, h, 0)),
            pl.BlockSpec((1, ncp, LANES), lambda bi, h, i: (bi, 0, 0)),
            pl.BlockSpec((1, 1, HEAD_DIM, ncp), lambda bi, h, i: (bi, h, 0, 0)),
            pl.BlockSpec((n_sel, ncp), lambda bi, h, i: (0, 0)),
            pl.BlockSpec((1, 1, 3, NSA_GROUP, tq), lambda bi, h, i: (bi, h, 0, 0, i)),
        ],
        out_specs=pl.BlockSpec((1, tq, gd), lambda bi, h, i: (bi, i, h)),
        scratch_shapes=([pltpu.VMEM((n_sel, tq), F32)] + [pltpu.VMEM((1, nq), F32)] * 2
                        + [pltpu.VMEM((HEAD_DIM + SUM_ROWS, nq), F32)] * 2),
        compiler_params=_cparams(("parallel", "parallel", "arbitrary")),
        name="nsa_attention",
    )(shifts, qT, zb, vselT, zb, vwinT, kc, vcT, ovl, gatesT)


def _in_conv_kernel(x_ref, mod_ref, g_ref, w_ref, b_ref, u_ref):
    m = mod_ref[0]
    d = x_ref.shape[-1]
    h = _norm_mod(x_ref[0], g_ref[...], m[1:2], m[0:1]).astype(BF16)
    a = jnp.dot(h, w_ref[:, :d], preferred_element_type=F32) + b_ref[:, :d]
    g = jnp.dot(h, w_ref[:, d:], preferred_element_type=F32) + b_ref[:, d:]
    u_ref[0] = a * _sigmoid(g)


def _conv_in_proj(x, mod, gain, w, bias, tm):
    b, s, d = x.shape
    return pl.pallas_call(
        _in_conv_kernel,
        out_shape=jax.ShapeDtypeStruct((b, s, d), F32),
        grid=(b, s // tm),
        in_specs=[
            pl.BlockSpec((1, tm, d), lambda i, j: (i, j, 0)),
            pl.BlockSpec((1, 6, d), lambda i, j: (i, 0, 0)),
            pl.BlockSpec((1, d), lambda i, j: (0, 0)),
            pl.BlockSpec((d, 2 * d), lambda i, j: (0, 0)),
            pl.BlockSpec((1, 2 * d), lambda i, j: (0, 0)),
        ],
        out_specs=pl.BlockSpec((1, tm, d), lambda i, j: (i, j, 0)),
        compiler_params=_cparams(("parallel", "parallel")),
        name="conv_in_proj",
    )(x, mod, gain, w, bias)


HALO = 32


def _dwconv_kernel(u_ref, halo_ref, w_ref, b_ref, lg_ref, lb_ref, o_ref, buf_sc, acc_sc, *, tm):
    i = pl.program_id(1)
    d = u_ref.shape[-1]
    buf_sc[0:HALO, :] = jnp.where(i > 0, halo_ref[0], 0.0)
    buf_sc[HALO:, :] = u_ref[0]
    off = HALO - (CONV_WIDTH - 1)
    cw = 256
    for c in range(d // cw):
        cs = slice(c * cw, (c + 1) * cw)
        acc = jnp.zeros((tm, cw), F32) + b_ref[:, cs]
        for j in range(CONV_WIDTH):
            acc = acc + w_ref[j:j + 1, cs] * buf_sc[off + j:off + j + tm, cs]
        acc_sc[:, cs] = acc
    y = acc_sc[...]
    mu = jnp.mean(y, axis=-1, keepdims=True)
    yc = y - mu
    var = jnp.mean(yc * yc, axis=-1, keepdims=True)
    yn = yc * lax.rsqrt(var + NORM_EPS) * lg_ref[...] + lb_ref[...]
    o_ref[0] = (yn * _sigmoid(yn)).astype(BF16)


def _dwconv_ln_swish(u, dw_w, dw_b, ln_g, ln_b, tm):
    b, s, d = u.shape
    kern = functools.partial(_dwconv_kernel, tm=tm)
    hb = tm // HALO
    return pl.pallas_call(
        kern,
        out_shape=jax.ShapeDtypeStruct((b, s, d), BF16),
        grid=(b, s // tm),
        in_specs=[
            pl.BlockSpec((1, tm, d), lambda bi, i: (bi, i, 0)),
            pl.BlockSpec((1, HALO, d), lambda bi, i: (bi, jnp.maximum(i * hb - 1, 0), 0)),
            pl.BlockSpec((CONV_WIDTH, d), lambda bi, i: (0, 0)),
            pl.BlockSpec((1, d), lambda bi, i: (0, 0)),
            pl.BlockSpec((1, d), lambda bi, i: (0, 0)),
            pl.BlockSpec((1, d), lambda bi, i: (0, 0)),
        ],
        out_specs=pl.BlockSpec((1, tm, d), lambda bi, i: (bi, i, 0)),
        scratch_shapes=[pltpu.VMEM((tm + HALO, d), F32), pltpu.VMEM((tm, d), F32)],
        compiler_params=_cparams(("parallel", "parallel")),
        name="dwconv_ln_swish",
    )(u, u, dw_w, dw_b, ln_g, ln_b)


def _out_router_kernel(a1_ref, a2_ref, w1_ref, w2_ref, b_ref, x_ref, mod_ref, g_ref, rw_ref, rb_ref,
                       xo_ref, h_ref, idx_ref, wt_ref):
    m = mod_ref[0]
    y = (jnp.dot(a1_ref[0], w1_ref[...], preferred_element_type=F32)
         + jnp.dot(a2_ref[0], w2_ref[...], preferred_element_type=F32) + b_ref[...])
    x = x_ref[0] + m[2:3] * y
    xo_ref[0] = x
    h = _norm_mod(x, g_ref[...], m[4:5], m[3:4])
    chunks = h.shape[1] // LANES
    for c in range(chunks):
        h_ref[0, pl.ds(c, h.shape[0], stride=chunks), :] = h[:, c * LANES:(c + 1) * LANES]
    h_hi = h.astype(BF16)
    h_lo = (h - h_hi.astype(F32)).astype(BF16)
    rw_hi = rw_ref[:, :LANES]
    logits = (jnp.dot(h_hi, rw_hi, preferred_element_type=F32)
              + jnp.dot(h_lo, rw_hi, preferred_element_type=F32)
              + jnp.dot(h_hi, rw_ref[:, LANES:], preferred_element_type=F32)) + rb_ref[...]
    lane = lax.broadcasted_iota(I32, logits.shape, 1)
    lane_f = lane.astype(F32)
    idx_out = jnp.zeros(logits.shape, F32)
    val_out = jnp.full(logits.shape, NEG, F32)
    for k in range(TOP_K):
        best = jnp.max(logits, axis=-1, keepdims=True)
        pick = jnp.min(jnp.where(logits == best, lane_f, float(LANES)), axis=-1, keepdims=True)
        idx_out = jnp.where(lane == k, pick, idx_out)
        val_out = jnp.where(lane == k, best, val_out)
        logits = jnp.where(lane_f == pick, TAKEN, logits)
    e = jnp.where(lane < TOP_K, jnp.exp(val_out - jnp.max(val_out, axis=-1, keepdims=True)), 0.0)
    idx_ref[0] = idx_out.astype(I32)
    wt_ref[0] = e / jnp.sum(e, axis=-1, keepdims=True)


def _out_proj_router(a1, a2, c1, c2, w1, w2, bias, x, mod, gain, rw_parts, rb, tm):
    b, s, d = x.shape
    kw = w1.shape[0]
    tok = lambda i, j: (i, j, 0)
    const = lambda i, j: (0, 0)
    return pl.pallas_call(
        _out_router_kernel,
        out_shape=(jax.ShapeDtypeStruct((b, s, d), F32), jax.ShapeDtypeStruct((b, s * (d // LANES), LANES), F32),
                   jax.ShapeDtypeStruct((b, s, LANES), I32), jax.ShapeDtypeStruct((b, s, LANES), F32)),
        grid=(b, s // tm),
        in_specs=[
            pl.BlockSpec((1, tm, kw), lambda i, j: (i, j, c1)),
            pl.BlockSpec((1, tm, kw), lambda i, j: (i, j, c2)),
            pl.BlockSpec((kw, d), const),
            pl.BlockSpec((kw, d), const),
            pl.BlockSpec((1, d), const),
            pl.BlockSpec((1, tm, d), tok),
            pl.BlockSpec((1, 6, d), lambda i, j: (i, 0, 0)),
            pl.BlockSpec((1, d), const),
            pl.BlockSpec((d, 2 * LANES), const),
            pl.BlockSpec((1, LANES), const),
        ],
        out_specs=(pl.BlockSpec((1, tm, d), tok), pl.BlockSpec((1, tm * (d // LANES), LANES), tok),
                   pl.BlockSpec((1, tm, LANES), tok), pl.BlockSpec((1, tm, LANES), tok)),
        compiler_params=_cparams(("parallel", "parallel")),
        name="out_proj_router",
    )(a1, a2, w1, w2, bias, x, mod, gain, rw_parts, rb)


ROW_CHUNKS = D_MODEL // LANES
DEINT = 512
DUMP_ROWS = 512


def _expert_kernel(be_ref, nv_ref, gsrc0_ref, gsrc1_ref, gdstp_ref, gdst0_ref, h_ref, w1_ref, b1g_ref, b1l_ref,
                   w2_ref, b2_ref, pe_ref, po_ref, yt_ref, xbuf, ybuf, w1g_sc, w1l_sc, w2_sc, gsem, ssem,
                   *, n_blocks, dump_row):
    i = pl.program_id(0)
    cur = i % 2
    nxt = 1 - cur
    rows = MOE_BLOCK

    def tile_rows(first):
        return pl.ds(pl.multiple_of(first, ROW_CHUNKS), ROW_CHUNKS)

    def buf_rows(r):
        return pl.ds(r * ROW_CHUNKS, ROW_CHUNKS) if isinstance(r, int) else tile_rows(r * ROW_CHUNKS)

    def gather(idx_ref, slot, r):
        return pltpu.make_async_copy(h_ref.at[tile_rows(idx_ref[0, 0, r])], xbuf.at[slot, buf_rows(r)], gsem.at[slot])

    def scatter(dst_first, slot, r):
        return pltpu.make_async_copy(ybuf.at[slot, buf_rows(r)], yt_ref.at[tile_rows(dst_first)], ssem.at[slot])

    def wait_gather(slot):
        pltpu.make_async_copy(h_ref.at[pl.ds(0, rows * ROW_CHUNKS)], xbuf.at[slot], gsem.at[slot]).wait()

    def wait_scatter(slot):
        pltpu.make_async_copy(ybuf.at[slot], yt_ref.at[pl.ds(0, rows * ROW_CHUNKS)], ssem.at[slot]).wait()

    @pl.when(i == 0)
    def _():
        ybuf[...] = jnp.zeros(ybuf.shape, F32)

        def prime(r, c):
            gather(gsrc0_ref, 0, r).start()
            return c

        lax.fori_loop(0, rows, prime, 0)

    wait_gather(cur)

    @pl.when(i >= 1)
    def _():
        wait_scatter(cur)

    first = i == 0
    changed = jnp.logical_or(first, be_ref[i] != be_ref[jnp.maximum(i - 1, 0)])

    @pl.when(changed)
    def _():
        for c in range(w1_ref.shape[3] // DEINT):
            wc = w1_ref[0, 0, :, c * DEINT:(c + 1) * DEINT].astype(BF16)
            cs = slice(c * (DEINT // 2), (c + 1) * (DEINT // 2))
            w1g_sc[:, cs] = jnp.dot(wc, pe_ref[...], preferred_element_type=F32).astype(BF16)
            w1l_sc[:, cs] = jnp.dot(wc, po_ref[...], preferred_element_type=F32).astype(BF16)
        w2_sc[...] = w2_ref[0, 0].astype(BF16)

    for r in range(rows):
        gather(gsrc1_ref, nxt, r).start(priority=r % 2)
        scatter(jnp.where(first, dump_row + r * ROW_CHUNKS, gdstp_ref[0, 0, r]), nxt, r).start(priority=(r + 1) % 2)

    def chunk(c):
        return pl.ds(c, rows, stride=ROW_CHUNKS)

    x = jnp.concatenate([xbuf[cur, chunk(c), :] for c in range(ROW_CHUNKS)], axis=1)
    row = lax.broadcasted_iota(I32, x.shape, 0)
    xb = jnp.where(row < nv_ref[i], x, 0.0).astype(BF16)
    glu = jnp.dot(xb, w1g_sc[...], preferred_element_type=F32) + b1g_ref[0]
    lin = jnp.dot(xb, w1l_sc[...], preferred_element_type=F32) + b1l_ref[0]
    glu = jnp.minimum(glu, SWIGLU_LIMIT)
    lin = jnp.clip(lin, -SWIGLU_LIMIT, SWIGLU_LIMIT)
    act = glu * _sigmoid(SWIGLU_ALPHA * glu) * (lin + 1.0)
    y = jnp.dot(act.astype(BF16), w2_sc[...], preferred_element_type=F32) + b2_ref[0]
    for c in range(ROW_CHUNKS):
        ybuf[cur, chunk(c), :] = y[:, c * LANES:(c + 1) * LANES]

    @pl.when(i == n_blocks - 1)
    def _():
        def last(r, c):
            scatter(gdst0_ref[0, 0, r], cur, r).start()
            return c

        lax.fori_loop(0, rows, last, 0)
        wait_gather(nxt)
        wait_scatter(nxt)
        wait_scatter(cur)


def _experts(h, gsrc, gdst, block_expert, n_valid, layer, w1, b1g, b1l, w2, b2):
    t = h.shape[0] // ROW_CHUNKS
    n_blocks = gsrc.shape[0]
    d, f2 = w1.shape[2:]
    f = f2 // 2
    sel = np.arange(DEINT)[:, None] - 2 * np.arange(DEINT // 2)[None, :]
    p_even = jnp.asarray((sel == 0).astype(np.float32), dtype=BF16)
    p_odd = jnp.asarray((sel == 1).astype(np.float32), dtype=BF16)
    last = n_blocks - 1

    def wsel(i, be, nv):
        return (be[i], 0, 0)

    def wsel_layer(i, be, nv):
        return (layer, be[i], 0, 0)

    const = lambda i, be, nv: (0, 0)
    idx_blk = (1, 1, MOE_BLOCK)
    kern = functools.partial(_expert_kernel, n_blocks=n_blocks, dump_row=TOP_K * t * ROW_CHUNKS)
    return pl.pallas_call(
        kern,
        out_shape=jax.ShapeDtypeStruct(((TOP_K * t + DUMP_ROWS) * ROW_CHUNKS, LANES), F32),
        grid_spec=pltpu.PrefetchScalarGridSpec(
            num_scalar_prefetch=2,
            grid=(n_blocks,),
            in_specs=[
                pl.BlockSpec(idx_blk, lambda i, be, nv: (i, 0, 0), memory_space=pltpu.SMEM),
                pl.BlockSpec(idx_blk, lambda i, be, nv: (jnp.minimum(i + 1, last), 0, 0), memory_space=pltpu.SMEM),
                pl.BlockSpec(idx_blk, lambda i, be, nv: (jnp.maximum(i - 1, 0), 0, 0), memory_space=pltpu.SMEM),
                pl.BlockSpec(idx_blk, lambda i, be, nv: (i, 0, 0), memory_space=pltpu.SMEM),
                pl.BlockSpec(memory_space=pl.ANY),
                pl.BlockSpec((1, 1, d, f2), wsel_layer),
                pl.BlockSpec((1, 1, f), wsel),
                pl.BlockSpec((1, 1, f), wsel),
                pl.BlockSpec((1, 1, f, d), wsel_layer),
                pl.BlockSpec((1, 1, d), wsel),
                pl.BlockSpec((DEINT, DEINT // 2), const),
                pl.BlockSpec((DEINT, DEINT // 2), const),
            ],
            out_specs=pl.BlockSpec(memory_space=pl.ANY),
            scratch_shapes=[pltpu.VMEM((2, MOE_BLOCK * ROW_CHUNKS, LANES), F32),
                            pltpu.VMEM((2, MOE_BLOCK * ROW_CHUNKS, LANES), F32),
                            pltpu.VMEM((d, f), BF16), pltpu.VMEM((d, f), BF16), pltpu.VMEM((f, d), BF16),
                            pltpu.SemaphoreType.DMA((2,)), pltpu.SemaphoreType.DMA((2,))],
        ),
        compiler_params=_cparams(("arbitrary",)),
        name="moe_experts",
    )(block_expert, n_valid, gsrc, gsrc, gdst, gdst, h, w1, b1g, b1l, w2, b2, p_even, p_odd)


def _combine_kernel(y0_ref, y1_ref, y2_ref, y3_ref, wt_ref, x_ref, mod_ref, o_ref):
    wt = wt_ref[0]
    g2 = mod_ref[0][5:6]
    y_refs = (y0_ref, y1_ref, y2_ref, y3_ref)
    wk = [wt[:, k:k + 1] for k in range(TOP_K)]
    tm = wt.shape[0]
    for c in range(ROW_CHUNKS):
        chunk = pl.ds(c, tm, stride=ROW_CHUNKS)
        y = wk[0] * y_refs[0][chunk, :]
        for k in range(1, TOP_K):
            y = y + wk[k] * y_refs[k][chunk, :]
        cs = slice(c * LANES, (c + 1) * LANES)
        o_ref[0, :, cs] = x_ref[0, :, cs] + g2[:, cs] * y


def _combine(yt, wts, x, mod, tm):
    b, s, d = x.shape
    nt = s // tm
    tiles = b * nt
    tok = lambda i, j: (i, j, 0)

    def yspec(k):
        return pl.BlockSpec((tm * ROW_CHUNKS, LANES), lambda i, j: (k * tiles + i * nt + j, 0))

    return pl.pallas_call(
        _combine_kernel,
        out_shape=jax.ShapeDtypeStruct((b, s, d), F32),
        grid=(b, nt),
        in_specs=[yspec(k) for k in range(TOP_K)] + [
            pl.BlockSpec((1, tm, LANES), tok),
            pl.BlockSpec((1, tm, d), tok),
            pl.BlockSpec((1, 6, d), lambda i, j: (i, 0, 0)),
        ],
        out_specs=pl.BlockSpec((1, tm, d), tok),
        compiler_params=_cparams(("arbitrary", "arbitrary")),
        name="moe_combine",
    )(yt, yt, yt, yt, wts, x, mod)


def _route_slots(top_idx, n_blocks):
    t = top_idx.shape[0]
    n_assign = t * TOP_K
    n_slots = n_blocks * MOE_BLOCK
    sorted_e, order = lax.sort_key_val(top_idx.reshape(-1), jnp.arange(n_assign, dtype=I32))
    experts = jnp.arange(N_EXPERTS + 1, dtype=I32)
    start = jnp.sum((sorted_e[None, :] < experts[:, None]).astype(I32), axis=1)
    counts = start[1:] - start[:-1]
    padded = (counts + MOE_BLOCK - 1) // MOE_BLOCK * MOE_BLOCK
    pad_end = jnp.cumsum(padded)
    pad_start = pad_end - padded
    blk_start = jnp.arange(n_blocks, dtype=I32) * MOE_BLOCK
    block_expert = jnp.minimum(jnp.sum((pad_end[None, :] <= blk_start[:, None]).astype(I32), axis=1), N_EXPERTS - 1)
    n_valid = jnp.clip(pad_start[block_expert] + counts[block_expert] - blk_start, 0, MOE_BLOCK).astype(I32)
    within = jnp.arange(MOE_BLOCK, dtype=I32)[None, :]
    valid = within < n_valid[:, None]
    rank = (blk_start - pad_start[block_expert] + start[block_expert])[:, None] + within
    slot_a = order[jnp.clip(rank, 0, n_assign - 1)]
    slot = blk_start[:, None] + within
    gsrc = jnp.where(valid, slot_a // TOP_K, 0) * ROW_CHUNKS
    gdst = jnp.where(valid, (slot_a % TOP_K) * t + slot_a // TOP_K, n_assign + slot % DUMP_ROWS) * ROW_CHUNKS
    shape = (n_blocks, 1, MOE_BLOCK)
    return gsrc.reshape(shape), gdst.reshape(shape), block_expert, n_valid


def _moe(x_mid, h, idx, wts, mod, layer, w1, b1, w2, b2, tm):
    b, s, d = x_mid.shape
    t = b * s
    n_blocks = -(-t * TOP_K // MOE_BLOCK) + N_EXPERTS
    top_idx = idx.reshape(t, LANES)[:, :TOP_K]
    gsrc, gdst, block_expert, n_valid = _route_slots(top_idx, n_blocks)
    yt = _experts(h.reshape(t * ROW_CHUNKS, LANES), gsrc, gdst, block_expert, n_valid, layer, w1,
                  b1[:, None, 0::2], b1[:, None, 1::2], w2, b2[:, None, :])
    return _combine(yt, wts, x_mid, mod, tm)


def _pad_cols(w, n):
    return jnp.pad(w, ((0, 0), (0, n - w.shape[-1])))


def _attn_colgain(diff_qk_gain, nsa_q_gain, nsa_k_gain):
    scale = HEAD_DIM ** -0.5 * LOG2E
    ones = jnp.ones((LANES,), F32)
    parts = [jnp.tile(diff_qk_gain[0] * scale, 8), jnp.tile(diff_qk_gain[1], 8), jnp.ones((512,), F32),
             jnp.tile(nsa_q_gain * scale, 8), ones, ones, jnp.tile(nsa_k_gain[1], 2), ones,
             jnp.tile(nsa_k_gain[2], 2), ones, ones]
    return jnp.concatenate(parts).reshape(1, W_IN_PAD)


def _overlap_matrix(s):
    n_sel = s // SEL_BLOCK
    ncp = s // CMP_STRIDE
    c_start = np.arange(ncp) * CMP_STRIDE
    s_start = np.arange(n_sel) * SEL_BLOCK
    ovl = (c_start[None, :] < s_start[:, None] + SEL_BLOCK) & (c_start[None, :] + CMP_BLOCK > s_start[:, None])
    ovl[:, ncp - 1] = False
    return jnp.asarray(ovl.astype(np.float32), dtype=BF16)


def _attention_layer(x, mod, gain, w_in, w_out, diff_qk_gain, diff_lambda, diff_subln, nsa_q_gain, nsa_k_gain,
                     cmp_pos, cmp_w1, cmp_w2, lam_init):
    b, s, d = x.shape
    n_gate = NSA_Q_HEADS * 3
    gate_perm = np.arange(n_gate).reshape(NSA_KV_HEADS, NSA_GROUP, 3).transpose(0, 2, 1).reshape(-1)
    w_pad = jnp.concatenate([w_in[:, :COL_GATE], w_in[:, COL_GATE + gate_perm],
                             jnp.zeros((d, W_IN_PAD - COL_GATE - n_gate), w_in.dtype)], axis=1).astype(BF16)
    zb, dqT, dvT, nqT, vselT, vwinT, gT, chunks = _attn_in_proj(
        x, mod, gain, w_pad, _attn_colgain(diff_qk_gain, nsa_q_gain, nsa_k_gain))

    o_diff = _diff_attention(zb, dqT.reshape(b, DIFF_HEADS, 2, HEAD_DIM, s), dvT,
                             _score_bound(diff_qk_gain[0], diff_qk_gain[1]), diff_lambda, diff_subln, lam_init,
                             DIFF_TQ, ATTN_TM)

    ncp = s // CMP_STRIDE
    cmp_out = _compress(chunks, cmp_pos.reshape(2, 1, CMP_BLOCK * HEAD_DIM), cmp_w1, cmp_w2, nsa_k_gain[0:1])
    kc = cmp_out[:, 0:2].transpose(0, 2, 1, 3).reshape(b, ncp, LANES).astype(BF16)
    vcT = cmp_out[:, 2:4].transpose(0, 1, 3, 2).astype(BF16)
    gatesT = gT[:, :n_gate].reshape(b, NSA_KV_HEADS, 3, NSA_GROUP, s)
    shifts = jnp.stack([_score_bound(nsa_q_gain, nsa_k_gain[1]), _score_bound(nsa_q_gain, nsa_k_gain[2])])
    o_nsa = _nsa_attention(zb, nqT.reshape(b, NSA_KV_HEADS, NSA_GROUP, HEAD_DIM, s), vselT, vwinT, kc, vcT,
                           _overlap_matrix(s), gatesT, shifts, NSA_TILE)
    return o_diff, o_nsa


def kernel(x, c, mod_w, mod_b, norm_mix, norm_ffn, attn_w_in, attn_w_out, diff_qk_gain, diff_lambda, diff_subln,
           nsa_q_gain, nsa_k_gain, nsa_cmp_pos, nsa_cmp_w1, nsa_cmp_w2, conv_pw1_w, conv_pw1_b, conv_dw_w,
           conv_dw_b, conv_ln_g, conv_ln_b, conv_pw2_w, conv_pw2_b, router_w, router_b, moe_w1, moe_b1, moe_w2,
           moe_b2):
    b, s, d = x.shape
    depth = mod_w.shape[0]
    tm = 512 if s % 512 == 0 else 256
    mods = _modulation(c, mod_w, mod_b).reshape(depth, b, 6, d)
    half = d // 2
    for i in range(depth):
        mod = mods[i]
        j = i // 2
        if i % 2 == 0:
            lam_init = 0.8 - 0.6 * math.exp(-0.3 * i)
            o_diff, o_nsa = _attention_layer(
                x, mod, norm_mix[i:i + 1], attn_w_in[j], attn_w_out[j], diff_qk_gain[j], diff_lambda[j],
                diff_subln[j], nsa_q_gain[j], nsa_k_gain[j], nsa_cmp_pos[j], nsa_cmp_w1[j], nsa_cmp_w2[j], lam_init)
            a1, a2, c1, c2 = o_diff, o_nsa, 0, 0
            w_o = attn_w_out[j].astype(BF16)
            bias = jnp.zeros((1, d), F32)
        else:
            u = _conv_in_proj(x, mod, norm_mix[i:i + 1], conv_pw1_w[j].astype(BF16), conv_pw1_b[j:j + 1], tm)
            v = _dwconv_ln_swish(u, conv_dw_w[j].reshape(CONV_WIDTH, d), conv_dw_b[j:j + 1], conv_ln_g[j:j + 1],
                                 conv_ln_b[j:j + 1], 256)
            a1, a2, c1, c2 = v, v, 0, 1
            w_o = conv_pw2_w[j].astype(BF16)
            bias = conv_pw2_b[j:j + 1]
        rw = _pad_cols(router_w[i], LANES)
        rw_hi = rw.astype(BF16)
        rw_lo = (rw - rw_hi.astype(F32)).astype(BF16)
        rb = jnp.concatenate([router_b[i], jnp.full((LANES - N_EXPERTS,), NEG, F32)]).reshape(1, LANES)
        x_mid, h, idx, wts = _out_proj_router(a1, a2, c1, c2, w_o[:half], w_o[half:], bias, x, mod,
                                              norm_ffn[i:i + 1], jnp.concatenate([rw_hi, rw_lo], axis=1), rb, tm)
        x = _moe(x_mid, h, idx, wts, mod, i, moe_w1, moe_b1[i], moe_w2, moe_b2[i], 256)
    return x
```

```python
import functools
import math

import jax
import jax.numpy as jnp
import numpy as np
from jax import lax
from jax.experimental import pallas as pl
from jax.experimental.pallas import tpu as pltpu

F32 = jnp.float32
BF16 = jnp.bfloat16
I32 = jnp.int32

D_MODEL = 1024
HEAD_DIM = 64
DIFF_HEADS = 4
NSA_Q_HEADS = 8
NSA_KV_HEADS = 2
NSA_GROUP = NSA_Q_HEADS // NSA_KV_HEADS
CMP_BLOCK = 32
CMP_STRIDE = 16
CMP_HIDDEN = 256
SEL_BLOCK = 64
SEL_TOP = 16
SEL_FORCED_SCORE = 1.0e4
WINDOW = 512
CONV_WIDTH = 31
N_EXPERTS = 32
TOP_K = 4
SWIGLU_ALPHA = 1.702
SWIGLU_LIMIT = 7.0
MOE_BLOCK = 256
NORM_EPS = 1e-6

LANES = 128
NEG = -1e30
TAKEN = -3e38
VMEM_LIMIT = 56 * 1024 * 1024

COL_DQ, COL_DK, COL_DV, COL_NQ = 0, 512, 1024, 1536
COL_KCMP, COL_VCMP, COL_KSEL, COL_VSEL, COL_KWIN, COL_VWIN = 2048, 2176, 2304, 2432, 2560, 2688
COL_GATE = 2816
W_IN_PAD = 2944
NORM_SLABS = frozenset(list(range(0, 8)) + list(range(12, 16)) + [COL_KSEL // LANES, COL_KWIN // LANES])


def _cparams(sem):
    return pltpu.CompilerParams(dimension_semantics=sem, vmem_limit_bytes=VMEM_LIMIT)


def _sigmoid(v):
    return 1.0 / (1.0 + jnp.exp(-v))


def _norm_mod(x, g, sc, sh):
    ms = jnp.mean(x * x, axis=-1, keepdims=True)
    return (x * lax.rsqrt(ms + NORM_EPS)) * g * (1.0 + sc) + sh


def _mod_kernel(c_ref, w_ref, b_ref, o_ref):
    c = c_ref[...]
    cond = c * _sigmoid(c)
    o_ref[0] = jnp.dot(cond, w_ref[0], preferred_element_type=F32, precision=lax.Precision.HIGHEST) + b_ref[0]


def _modulation(c, mod_w, mod_b):
    depth, d, n = mod_w.shape
    b = c.shape[0]
    tn = 1536
    return pl.pallas_call(
        _mod_kernel,
        out_shape=jax.ShapeDtypeStruct((depth, b, n), F32),
        grid=(depth, n // tn),
        in_specs=[
            pl.BlockSpec((b, d), lambda i, j: (0, 0)),
            pl.BlockSpec((1, d, tn), lambda i, j: (i, 0, j)),
            pl.BlockSpec((1, 1, tn), lambda i, j: (i, 0, j)),
        ],
        out_specs=pl.BlockSpec((1, b, tn), lambda i, j: (i, 0, j)),
        compiler_params=_cparams(("parallel", "parallel")),
        name="modulation",
    )(c, mod_w, mod_b.reshape(depth, 1, n))


KZ_DK, KZ_KSEL, KZ_KWIN, KZ_WIDTH = 0, 512, 640, 768
KEY_COLS = {COL_DK + 128 * u: KZ_DK + 128 * u for u in range(4)}
KEY_COLS.update({COL_KSEL: KZ_KSEL, COL_KWIN: KZ_KWIN})
ATTN_TM = 512
DIFF_TQ = 1024
NSA_TILE = 512
GATE_ROWS = 32


def _in_attn_kernel(x_ref, mod_ref, g_ref, w_ref, cg_ref, z_ref, dqT_ref, dvT_ref, nqT_ref, vsT_ref, vwT_ref, gT_ref,
                    ch_ref, cmp_sc):
    m = mod_ref[0]
    h = _norm_mod(x_ref[0], g_ref[...], m[1:2], m[0:1]).astype(BF16)
    lo = lax.broadcasted_iota(I32, (1, LANES), 1) < HEAD_DIM
    groups = ((0, 512), (512, 1024), (1024, 1536), (1536, 2048), (2048, 2816), (2816, 2944))
    halves = ATTN_TM // NSA_TILE
    for c0, c1 in groups:
        z = jnp.dot(h, w_ref[:, c0:c1], preferred_element_type=F32)
        for s in range((c1 - c0) // LANES):
            a0 = c0 + s * LANES
            zs = z[:, s * LANES:(s + 1) * LANES]
            if a0 == COL_GATE:
                gT_ref[0] = _sigmoid(zs).T[:GATE_ROWS]
                continue
            if a0 // LANES in NORM_SLABS:
                z2 = zs * zs
                s_lo = jnp.sum(jnp.where(lo, z2, 0.0), axis=-1, keepdims=True)
                s_hi = jnp.sum(jnp.where(lo, 0.0, z2), axis=-1, keepdims=True)
                inv = jnp.where(lo, lax.rsqrt(s_lo * (1.0 / HEAD_DIM) + NORM_EPS),
                                lax.rsqrt(s_hi * (1.0 / HEAD_DIM) + NORM_EPS))
                zs = zs * inv
            zs = zs * cg_ref[:, a0:a0 + LANES]
            if COL_DQ <= a0 < COL_DK:
                dqT_ref[0, a0 - COL_DQ:a0 - COL_DQ + LANES, :] = zs.T.astype(BF16)
            elif COL_DV <= a0 < COL_NQ:
                dvT_ref[0, 0, a0 - COL_DV:a0 - COL_DV + LANES, :] = zs.T.astype(BF16)
            elif COL_NQ <= a0 < COL_KCMP:
                nqT_ref[0, a0 - COL_NQ:a0 - COL_NQ + LANES, :] = zs.T.astype(BF16)
            elif a0 in (COL_VSEL, COL_VWIN):
                zt = zs.T.astype(BF16)
                ref = vsT_ref if a0 == COL_VSEL else vwT_ref
                for u in range(halves):
                    ref[0, u] = zt[:, u * NSA_TILE:(u + 1) * NSA_TILE]
            elif a0 in (COL_KCMP, COL_VCMP):
                cmp_sc[...] = zs
                first = 0 if a0 == COL_KCMP else NSA_KV_HEADS
                n_rows = zs.shape[0] // CMP_STRIDE
                for pair in range(CMP_STRIDE // 2):
                    even = cmp_sc[pl.ds(2 * pair, n_rows, stride=CMP_STRIDE), :]
                    odd = cmp_sc[pl.ds(2 * pair + 1, n_rows, stride=CMP_STRIDE), :]
                    cs = slice(pair * LANES, (pair + 1) * LANES)
                    ch_ref[0, first, :, cs] = jnp.where(lo, even, pltpu.roll(odd, HEAD_DIM, 1)).astype(BF16)
                    ch_ref[0, first + 1, :, cs] = jnp.where(lo, pltpu.roll(even, HEAD_DIM, 1), odd).astype(BF16)
            else:
                k0 = KEY_COLS[a0]
                z_ref[0, :, k0:k0 + LANES] = zs.astype(BF16)


def _attn_in_proj(x, mod, gain, w_pad, colgain):
    b, s, d = x.shape
    tm = ATTN_TM
    halves = tm // NSA_TILE
    tok = lambda i, j: (i, j, 0)
    featT = lambda i, j: (i, 0, j)
    return pl.pallas_call(
        _in_attn_kernel,
        out_shape=(jax.ShapeDtypeStruct((b, s, KZ_WIDTH), BF16),
                   jax.ShapeDtypeStruct((b, 512, s), BF16),
                   jax.ShapeDtypeStruct((b, s // tm, 512, tm), BF16),
                   jax.ShapeDtypeStruct((b, 512, s), BF16),
                   jax.ShapeDtypeStruct((b, s // NSA_TILE, LANES, NSA_TILE), BF16),
                   jax.ShapeDtypeStruct((b, s // NSA_TILE, LANES, NSA_TILE), BF16),
                   jax.ShapeDtypeStruct((b, GATE_ROWS, s), F32),
                   jax.ShapeDtypeStruct((b, 2 * NSA_KV_HEADS, s // CMP_STRIDE, CMP_STRIDE * HEAD_DIM), BF16)),
        grid=(b, s // tm),
        in_specs=[
            pl.BlockSpec((1, tm, d), tok),
            pl.BlockSpec((1, 6, d), lambda i, j: (i, 0, 0)),
            pl.BlockSpec((1, d), lambda i, j: (0, 0)),
            pl.BlockSpec((d, W_IN_PAD), lambda i, j: (0, 0)),
            pl.BlockSpec((1, W_IN_PAD), lambda i, j: (0, 0)),
        ],
        out_specs=(pl.BlockSpec((1, tm, KZ_WIDTH), tok),
                   pl.BlockSpec((1, 512, tm), featT),
                   pl.BlockSpec((1, 1, 512, tm), lambda i, j: (i, j, 0, 0)),
                   pl.BlockSpec((1, 512, tm), featT),
                   pl.BlockSpec((1, halves, LANES, NSA_TILE), lambda i, j: (i, j, 0, 0)),
                   pl.BlockSpec((1, halves, LANES, NSA_TILE), lambda i, j: (i, j, 0, 0)),
                   pl.BlockSpec((1, GATE_ROWS, tm), featT),
                   pl.BlockSpec((1, 2 * NSA_KV_HEADS, tm // CMP_STRIDE, CMP_STRIDE * HEAD_DIM),
                                lambda i, j: (i, 0, j, 0))),
        scratch_shapes=[pltpu.VMEM((tm, LANES), F32)],
        compiler_params=_cparams(("parallel", "parallel")),
        name="attn_in_proj",
    )(x, mod, gain, w_pad, colgain)


FIXED_SHIFT_LIMIT = 57.0
FIXED_TILES_PER_STEP = 4
LOG2E = math.log2(math.e)


def _score_bound(gain_q, gain_k):
    return (math.sqrt(HEAD_DIM) * LOG2E) * jnp.max(jnp.abs(gain_q * gain_k))


SUM_ROWS = 8


def _with_ones(vt):
    return jnp.concatenate([vt, jnp.ones((SUM_ROWS, vt.shape[1]), vt.dtype)], axis=0)


def _fixed_update(ss, vts, shift, acc_ref):
    ps = [jnp.exp2(s - shift).astype(BF16) for s in ss]
    acc_ref[...] += sum(jnp.dot(_with_ones(vt), p, preferred_element_type=F32) for vt, p in zip(vts, ps))


def _online_update(ss, vts, m_ref, acc_ref):
    for s, vt in zip(ss, vts):
        m_old = m_ref[...]
        m_new = jnp.maximum(m_old, jnp.max(s, axis=0, keepdims=True))
        alpha = jnp.exp2(m_old - m_new)
        p = jnp.exp2(s - m_new).astype(BF16)
        acc_ref[...] = alpha * acc_ref[...] + jnp.dot(_with_ones(vt), p, preferred_element_type=F32)
        m_ref[...] = m_new


def _tile_loops(n_full, tiles, width, n_masked=1):
    def group(jj, c):
        tiles(tuple(width * jj + u for u in range(width)), (False,) * width)
        return c

    lax.fori_loop(0, n_full // width, group, 0)
    rem = n_full % width
    base = n_full - rem
    for r in range(width):
        @pl.when(rem == r)
        def _():
            tiles(tuple(base + u for u in range(r + n_masked)), (False,) * r + (True,) * n_masked)


def _diff_kernel(mb_ref, lam_ref, qT_ref, k_ref, vT_ref, sub_ref, o_ref, m0, m1, a0, a1, *, tq, tk, lam_init):
    i = pl.program_id(2)
    m_refs, acc_refs = (m0, m1), (a0, a1)
    q = qT_ref[0, 0]
    zero = jnp.zeros((HEAD_DIM, tq), BF16)
    qp = (jnp.concatenate([q[0], zero], axis=0), jnp.concatenate([zero, q[1]], axis=0))
    for mm in range(2):
        m_refs[mm][...] = jnp.full(m_refs[mm].shape, NEG, F32)
        acc_refs[mm][...] = jnp.zeros(acc_refs[mm].shape, F32)
    shift = mb_ref[0]
    dv = 2 * HEAD_DIM

    def tiles(js, masks, fixed):
        kts = [k_ref[0, pl.ds(pl.multiple_of(j * tk, tk), tk), :] for j in js]
        vts = [vT_ref[0, j] for j in js]
        for mm in range(2):
            ss = []
            for j, kt, masked in zip(js, kts, masks):
                s = jnp.dot(kt, qp[mm], preferred_element_type=F32)
                if masked:
                    kpos = j * tk + lax.broadcasted_iota(I32, (tk, tq), 0)
                    t = i * tq + lax.broadcasted_iota(I32, (tk, tq), 1)
                    s = jnp.where(kpos <= t, s, NEG)
                ss.append(s)
            if fixed:
                _fixed_update(ss, vts, shift, acc_refs[mm])
            else:
                _online_update(ss, vts, m_refs[mm], acc_refs[mm])

    n_full = (i * tq) // tk
    n_masked = max(tq // tk, 1)

    @pl.when(shift <= FIXED_SHIFT_LIMIT)
    def _():
        _tile_loops(n_full, functools.partial(tiles, fixed=True), FIXED_TILES_PER_STEP, n_masked)

    @pl.when(shift > FIXED_SHIFT_LIMIT)
    def _():
        _tile_loops(n_full, functools.partial(tiles, fixed=False), 1, n_masked)

    lv = lam_ref[...]
    lam = (jnp.exp(jnp.sum(lv[0:1] * lv[1:2], axis=-1, keepdims=True))
           - jnp.exp(jnp.sum(lv[2:3] * lv[3:4], axis=-1, keepdims=True)) + lam_init)
    o = a0[:dv] / a0[dv:dv + 1] - lam * (a1[:dv] / a1[dv:dv + 1])
    ms = jnp.mean(o * o, axis=0, keepdims=True)
    o = o * lax.rsqrt(ms + NORM_EPS) * sub_ref[...] * (1.0 - lam_init)
    o_ref[0] = o.T.astype(BF16)


def _diff_attention(zb, qT, vT, shift, lam_vecs, subln, lam_init, tq, tk):
    b, s, _ = zb.shape
    kern = functools.partial(_diff_kernel, tq=tq, tk=tk, lam_init=lam_init)
    dv = 2 * HEAD_DIM
    return pl.pallas_call(
        kern,
        out_shape=jax.ShapeDtypeStruct((b, s, DIFF_HEADS * dv), BF16),
        grid=(b, DIFF_HEADS, s // tq),
        in_specs=[
            pl.BlockSpec(memory_space=pltpu.SMEM),
            pl.BlockSpec((4, HEAD_DIM), lambda bi, h, i: (0, 0)),
            pl.BlockSpec((1, 1, 2, HEAD_DIM, tq), lambda bi, h, i: (bi, h, 0, 0, i)),
            pl.BlockSpec((1, s, LANES), lambda bi, h, i: (bi, 0, KZ_DK // LANES + h)),
            pl.BlockSpec((1, s // tk, 2 * HEAD_DIM, tk), lambda bi, h, i: (bi, 0, h, 0)),
            pl.BlockSpec((2 * HEAD_DIM, 1), lambda bi, h, i: (0, 0)),
        ],
        out_specs=pl.BlockSpec((1, tq, LANES), lambda bi, h, i: (bi, i, h)),
        scratch_shapes=[pltpu.VMEM((1, tq), F32)] * 2 + [pltpu.VMEM((dv + SUM_ROWS, tq), F32)] * 2,
        compiler_params=_cparams(("parallel", "parallel", "arbitrary")),
        name="diff_attention",
    )(shift.reshape(1), lam_vecs, qT, zb, vT, subln.reshape(-1, 1))


def _gelu_tanh(v):
    return 0.5 * v * (1.0 + jnp.tanh(math.sqrt(2.0 / math.pi) * (v + 0.044715 * (v * v * v))))


def _cmp_kernel(c_ref, pos_ref, w1_ref, w2_ref, kg_ref, o_ref, *, n_cmp):
    jh = pl.program_id(1)
    half = CMP_STRIDE * HEAD_DIM
    c = c_ref[0, 0]
    w1a = w1_ref[0, :half, :].astype(BF16)
    w1b = w1_ref[0, half:, :].astype(BF16)
    pos = jnp.broadcast_to(pos_ref[0], (8, 2 * half)).astype(BF16)
    bias = (jnp.dot(pos[:, :half], w1a, preferred_element_type=F32)
            + jnp.dot(pos[:, half:], w1b, preferred_element_type=F32))[0:1]
    u = jnp.dot(c, w1a, preferred_element_type=F32)
    v = jnp.dot(c, w1b, preferred_element_type=F32)
    ncp = u.shape[0]
    hid = _gelu_tanh(u + pltpu.roll(v, ncp - 1, 0) + bias)
    y = jnp.dot(hid.astype(BF16), w2_ref[0].astype(BF16), preferred_element_type=F32)
    yn = y * lax.rsqrt(jnp.mean(y * y, axis=-1, keepdims=True) + NORM_EPS) * kg_ref[...]
    y = jnp.where(jh < NSA_KV_HEADS, yn, y)
    row = lax.broadcasted_iota(I32, y.shape, 0)
    o_ref[0, 0] = jnp.where(row < n_cmp, y, 0.0)


def _compress(chunks, pos, w1, w2, kgain):
    b, _, ncp, cd = chunks.shape
    kern = functools.partial(_cmp_kernel, n_cmp=ncp - 1)
    return pl.pallas_call(
        kern,
        out_shape=jax.ShapeDtypeStruct((b, 4, ncp, HEAD_DIM), F32),
        grid=(b, 4),
        in_specs=[
            pl.BlockSpec((1, 1, ncp, cd), lambda bi, j: (bi, j, 0, 0)),
            pl.BlockSpec((1, 1, 2 * cd), lambda bi, j: (j // NSA_KV_HEADS, 0, 0)),
            pl.BlockSpec((1, 2 * cd, CMP_HIDDEN), lambda bi, j: (j // NSA_KV_HEADS, 0, 0)),
            pl.BlockSpec((1, CMP_HIDDEN, HEAD_DIM), lambda bi, j: (j // NSA_KV_HEADS, 0, 0)),
            pl.BlockSpec((1, HEAD_DIM), lambda bi, j: (0, 0)),
        ],
        out_specs=pl.BlockSpec((1, 1, ncp, HEAD_DIM), lambda bi, j: (bi, j, 0, 0)),
        compiler_params=_cparams(("parallel", "parallel")),
        name="nsa_compress",
    )(chunks, pos, w1, w2, kgain)


def _nsa_kernel(mb_ref, qT_ref, ksel_ref, vselT_ref, kwin_ref, vwinT_ref, kc_ref, vcT_ref, ovl_ref, gate_ref, o_ref,
                selb_sc, m_s, m_w, a_s, a_w, *, tq, n_top):
    hk = pl.program_id(1)
    i = pl.program_id(2)
    g4 = NSA_GROUP
    nq = g4 * tq
    q4 = jnp.concatenate([qT_ref[0, 0, g] for g in range(g4)], axis=1)
    half = lax.broadcasted_iota(I32, (2 * HEAD_DIM, nq), 0) // HEAD_DIM
    qp = jnp.where(half == hk, jnp.concatenate([q4, q4], axis=0), jnp.zeros((), BF16))
    t1 = i * tq + lax.broadcasted_iota(I32, (1, tq), 1)

    def tile4(a):
        return jnp.concatenate([a] * g4, axis=1)

    ncp = kc_ref.shape[1]
    sc = jnp.dot(kc_ref[0], qp, preferred_element_type=F32)
    cend = CMP_STRIDE * lax.broadcasted_iota(I32, (ncp, tq), 0) + (CMP_BLOCK - 1)
    sc = sc + tile4(jnp.where(cend <= t1, 0.0, NEG))
    e = jnp.where(sc > 0.5 * NEG, jnp.exp2(sc - jnp.max(sc, axis=0, keepdims=True)), 0.0)
    p = e / jnp.maximum(jnp.sum(e, axis=0, keepdims=True), 1e-30)
    o_c = jnp.dot(vcT_ref[0, 0], p.astype(BF16), preferred_element_type=F32)

    psum = p[:, 0:tq]
    for g in range(1, g4):
        psum = psum + p[:, g * tq:(g + 1) * tq]
    p_hi = psum.astype(BF16)
    p_lo = (psum - p_hi.astype(F32)).astype(BF16)
    imp = (jnp.dot(ovl_ref[...], p_hi, preferred_element_type=F32)
           + jnp.dot(ovl_ref[...], p_lo, preferred_element_type=F32))
    n_sel = imp.shape[0]
    jrow = lax.broadcasted_iota(I32, (n_sel, tq), 0)
    cur = t1 // SEL_BLOCK
    forced = (jrow == 0) | (jrow == cur) | (jrow == cur - 1)
    score = jnp.where(forced, SEL_FORCED_SCORE, jnp.where(jrow <= cur, imp, NEG))
    selb = jnp.full((n_sel, tq), NEG, F32)
    jrow_f = jrow.astype(F32)
    for _ in range(n_top):
        best = jnp.max(score, axis=0, keepdims=True)
        pick = jnp.min(jnp.where(score == best, jrow_f, float(n_sel)), axis=0, keepdims=True)
        hit = jrow_f == pick
        selb = jnp.where(hit, 0.0, selb)
        score = jnp.where(hit, TAKEN, score)
    selb_sc[...] = selb

    for ref in (m_s, m_w):
        ref[...] = jnp.full(ref.shape, NEG, F32)
    for ref in (a_s, a_w):
        ref[...] = jnp.zeros(ref.shape, F32)
    tk = tq
    bpt = tk // SEL_BLOCK
    krow = lax.broadcasted_iota(I32, (tk, tq), 0)
    shift_s, shift_w = mb_ref[0], mb_ref[1]
    fixed = jnp.maximum(shift_s, shift_w) <= FIXED_SHIFT_LIMIT

    def sel_tiles(js, masks, fixed):
        ss = []
        for j, masked in zip(js, masks):
            kt = ksel_ref[0, pl.ds(pl.multiple_of(j * tk, tk), tk), :]
            rows = [jnp.broadcast_to(selb_sc[pl.ds(j * bpt + r, 1), :], (SEL_BLOCK, tq)) for r in range(bpt)]
            bias = jnp.concatenate(rows, axis=0)
            if masked:
                bias = jnp.where(j * tk + krow <= t1, bias, NEG)
            ss.append(jnp.dot(kt, qp, preferred_element_type=F32) + tile4(bias))
        vts = [vselT_ref[0, j] for j in js]
        if fixed:
            _fixed_update(ss, vts, shift_s, a_s)
        else:
            _online_update(ss, vts, m_s, a_s)

    def win_tiles(js, fixed):
        ss = []
        for j in js:
            kt = kwin_ref[0, pl.ds(pl.multiple_of(j * tk, tk), tk), :]
            kpos = j * tk + krow
            bias = jnp.where(kpos <= t1, jnp.where(kpos > t1 - WINDOW, 0.0, NEG), NEG)
            ss.append(jnp.dot(kt, qp, preferred_element_type=F32) + tile4(bias))
        vts = [vwinT_ref[0, j] for j in js]
        if fixed:
            _fixed_update(ss, vts, shift_w, a_w)
        else:
            _online_update(ss, vts, m_w, a_w)

    max_win = (WINDOW + tk - 1) // tk + 1
    n_win = jnp.minimum(i + 1, max_win)
    for use_fixed in (True, False):
        @pl.when(fixed == use_fixed)
        def _():
            _tile_loops(i, functools.partial(sel_tiles, fixed=use_fixed), FIXED_TILES_PER_STEP if use_fixed else 1)
            for c in range(1, max_win + 1):
                @pl.when(n_win == c)
                def _():
                    win_tiles(tuple(i - (c - 1) + u for u in range(c)), use_fixed)

    def gate_row(br):
        gt = gate_ref[0, 0, br]
        return jnp.concatenate([gt[g:g + 1] for g in range(g4)], axis=1)

    dv = HEAD_DIM
    out = (o_c * gate_row(0) + (a_s[:dv] / a_s[dv:dv + 1]) * gate_row(1) + (a_w[:dv] / a_w[dv:dv + 1]) * gate_row(2))
    stacked = jnp.concatenate([out[:, g * tq:(g + 1) * tq] for g in range(g4)], axis=0)
    o_ref[0] = stacked.T.astype(BF16)


def _nsa_attention(zb, qT, vselT, vwinT, kc, vcT, ovl, gatesT, shifts, tq):
    b, s, _ = zb.shape
    nt = s // tq
    n_sel = s // SEL_BLOCK
    ncp = kc.shape[1]
    kern = functools.partial(_nsa_kernel, tq=tq, n_top=min(SEL_TOP, n_sel))
    gd = NSA_GROUP * HEAD_DIM
    nq = NSA_GROUP * tq
    return pl.pallas_call(
        kern,
        out_shape=jax.ShapeDtypeStruct((b, s, NSA_Q_HEADS * HEAD_DIM), BF16),
        grid=(b, NSA_KV_HEADS, nt),
        in_specs=[
            pl.BlockSpec(memory_space=pltpu.SMEM),
            pl.BlockSpec((1, 1, NSA_GROUP, HEAD_DIM, tq), lambda bi, h, i: (bi, h, 0, 0, i)),
            pl.BlockSpec((1, s, LANES), lambda bi, h, i: (bi, 0, KZ_KSEL // LANES)),
            pl.BlockSpec((1, nt, HEAD_DIM, tq), lambda bi, h, i: (bi, 0, h, 0)),
            pl.BlockSpec((1, s, LANES), lambda bi, h, i: (bi, 0, KZ_KWIN // LANES)),
            pl.BlockSpec((1, nt, HEAD_DIM, tq), lambda bi, h, i: (bi, 0, h, 0)),
            pl.BlockSpec((1, ncp, LANES), lambda bi, h, i: (bi, 0, 0)),
            pl.BlockSpec((1, 1, HEAD_DIM, ncp), lambda bi, h, i: (bi, h, 0, 0)),
            pl.BlockSpec((n_sel, ncp), lambda bi, h, i: (0, 0)),
            pl.BlockSpec((1, 1, 3, NSA_GROUP, tq), lambda bi, h, i: (bi, h, 0, 0, i)),
        ],
        out_specs=pl.BlockSpec((1, tq, gd), lambda bi, h, i: (bi, i, h)),
        scratch_shapes=([pltpu.VMEM((n_sel, tq), F32)] + [pltpu.VMEM((1, nq), F32)] * 2
                        + [pltpu.VMEM((HEAD_DIM + SUM_ROWS, nq), F32)] * 2),
        compiler_params=_cparams(("parallel", "parallel", "arbitrary")),
        name="nsa_attention",
    )(shifts, qT, zb, vselT, zb, vwinT, kc, vcT, ovl, gatesT)


def _in_conv_kernel(x_ref, mod_ref, g_ref, w_ref, b_ref, u_ref):
    m = mod_ref[0]
    d = x_ref.shape[-1]
    h = _norm_mod(x_ref[0], g_ref[...], m[1:2], m[0:1]).astype(BF16)
    a = jnp.dot(h, w_ref[:, :d], preferred_element_type=F32) + b_ref[:, :d]
    g = jnp.dot(h, w_ref[:, d:], preferred_element_type=F32) + b_ref[:, d:]
    u_ref[0] = a * _sigmoid(g)


def _conv_in_proj(x, mod, gain, w, bias, tm):
    b, s, d = x.shape
    return pl.pallas_call(
        _in_conv_kernel,
        out_shape=jax.ShapeDtypeStruct((b, s, d), F32),
        grid=(b, s // tm),
        in_specs=[
            pl.BlockSpec((1, tm, d), lambda i, j: (i, j, 0)),
            pl.BlockSpec((1, 6, d), lambda i, j: (i, 0, 0)),
            pl.BlockSpec((1, d), lambda i, j: (0, 0)),
            pl.BlockSpec((d, 2 * d), lambda i, j: (0, 0)),
            pl.BlockSpec((1, 2 * d), lambda i, j: (0, 0)),
        ],
        out_specs=pl.BlockSpec((1, tm, d), lambda i, j: (i, j, 0)),
        compiler_params=_cparams(("parallel", "parallel")),
        name="conv_in_proj",
    )(x, mod, gain, w, bias)


HALO = 32


def _dwconv_kernel(u_ref, halo_ref, w_ref, b_ref, lg_ref, lb_ref, o_ref, buf_sc, acc_sc, *, tm):
    i = pl.program_id(1)
    d = u_ref.shape[-1]
    buf_sc[0:HALO, :] = jnp.where(i > 0, halo_ref[0], 0.0)
    buf_sc[HALO:, :] = u_ref[0]
    off = HALO - (CONV_WIDTH - 1)
    cw = 256
    for c in range(d // cw):
        cs = slice(c * cw, (c + 1) * cw)
        acc = jnp.zeros((tm, cw), F32) + b_ref[:, cs]
        for j in range(CONV_WIDTH):
            acc = acc + w_ref[j:j + 1, cs] * buf_sc[off + j:off + j + tm, cs]
        acc_sc[:, cs] = acc
    y = acc_sc[...]
    mu = jnp.mean(y, axis=-1, keepdims=True)
    yc = y - mu
    var = jnp.mean(yc * yc, axis=-1, keepdims=True)
    yn = yc * lax.rsqrt(var + NORM_EPS) * lg_ref[...] + lb_ref[...]
    o_ref[0] = (yn * _sigmoid(yn)).astype(BF16)


def _dwconv_ln_swish(u, dw_w, dw_b, ln_g, ln_b, tm):
    b, s, d = u.shape
    kern = functools.partial(_dwconv_kernel, tm=tm)
    hb = tm // HALO
    return pl.pallas_call(
        kern,
        out_shape=jax.ShapeDtypeStruct((b, s, d), BF16),
        grid=(b, s // tm),
        in_specs=[
            pl.BlockSpec((1, tm, d), lambda bi, i: (bi, i, 0)),
            pl.BlockSpec((1, HALO, d), lambda bi, i: (bi, jnp.maximum(i * hb - 1, 0), 0)),
            pl.BlockSpec((CONV_WIDTH, d), lambda bi, i: (0, 0)),
            pl.BlockSpec((1, d), lambda bi, i: (0, 0)),
            pl.BlockSpec((1, d), lambda bi, i: (0, 0)),
            pl.BlockSpec((1, d), lambda bi, i: (0, 0)),
        ],
        out_specs=pl.BlockSpec((1, tm, d), lambda bi, i: (bi, i, 0)),
        scratch_shapes=[pltpu.VMEM((tm + HALO, d), F32), pltpu.VMEM((tm, d), F32)],
        compiler_params=_cparams(("parallel", "parallel")),
        name="dwconv_ln_swish",
    )(u, u, dw_w, dw_b, ln_g, ln_b)


def _out_router_kernel(a1_ref, a2_ref, w1_ref, w2_ref, b_ref, x_ref, mod_ref, g_ref, rw_ref, rb_ref,
                       xo_ref, h_ref, idx_ref, wt_ref):
    m = mod_ref[0]
    y = (jnp.dot(a1_ref[0], w1_ref[...], preferred_element_type=F32)
         + jnp.dot(a2_ref[0], w2_ref[...], preferred_element_type=F32) + b_ref[...])
    x = x_ref[0] + m[2:3] * y
    xo_ref[0] = x
    h = _norm_mod(x, g_ref[...], m[4:5], m[3:4])
    chunks = h.shape[1] // LANES
    for c in range(chunks):
        h_ref[0, pl.ds(c, h.shape[0], stride=chunks), :] = h[:, c * LANES:(c + 1) * LANES]
    h_hi = h.astype(BF16)
    h_lo = (h - h_hi.astype(F32)).astype(BF16)
    rw_hi = rw_ref[:, :LANES]
    logits = (jnp.dot(h_hi, rw_hi, preferred_element_type=F32)
              + jnp.dot(h_lo, rw_hi, preferred_element_type=F32)
              + jnp.dot(h_hi, rw_ref[:, LANES:], preferred_element_type=F32)) + rb_ref[...]
    lane = lax.broadcasted_iota(I32, logits.shape, 1)
    lane_f = lane.astype(F32)
    idx_out = jnp.zeros(logits.shape, F32)
    val_out = jnp.full(logits.shape, NEG, F32)
    for k in range(TOP_K):
        best = jnp.max(logits, axis=-1, keepdims=True)
        pick = jnp.min(jnp.where(logits == best, lane_f, float(LANES)), axis=-1, keepdims=True)
        idx_out = jnp.where(lane == k, pick, idx_out)
        val_out = jnp.where(lane == k, best, val_out)
        logits = jnp.where(lane_f == pick, TAKEN, logits)
    e = jnp.where(lane < TOP_K, jnp.exp(val_out - jnp.max(val_out, axis=-1, keepdims=True)), 0.0)
    idx_ref[0] = idx_out.astype(I32)
    wt_ref[0] = e / jnp.sum(e, axis=-1, keepdims=True)


def _out_proj_router(a1, a2, c1, c2, w1, w2, bias, x, mod, gain, rw_parts, rb, tm):
    b, s, d = x.shape
    kw = w1.shape[0]
    tok = lambda i, j: (i, j, 0)
    const = lambda i, j: (0, 0)
    return pl.pallas_call(
        _out_router_kernel,
        out_shape=(jax.ShapeDtypeStruct((b, s, d), F32), jax.ShapeDtypeStruct((b, s * (d // LANES), LANES), F32),
                   jax.ShapeDtypeStruct((b, s, LANES), I32), jax.ShapeDtypeStruct((b, s, LANES), F32)),
        grid=(b, s // tm),
        in_specs=[
            pl.BlockSpec((1, tm, kw), lambda i, j: (i, j, c1)),
            pl.BlockSpec((1, tm, kw), lambda i, j: (i, j, c2)),
            pl.BlockSpec((kw, d), const),
            pl.BlockSpec((kw, d), const),
            pl.BlockSpec((1, d), const),
            pl.BlockSpec((1, tm, d), tok),
            pl.BlockSpec((1, 6, d), lambda i, j: (i, 0, 0)),
            pl.BlockSpec((1, d), const),
            pl.BlockSpec((d, 2 * LANES), const),
            pl.BlockSpec((1, LANES), const),
        ],
        out_specs=(pl.BlockSpec((1, tm, d), tok), pl.BlockSpec((1, tm * (d // LANES), LANES), tok),
                   pl.BlockSpec((1, tm, LANES), tok), pl.BlockSpec((1, tm, LANES), tok)),
        compiler_params=_cparams(("parallel", "parallel")),
        name="out_proj_router",
    )(a1, a2, w1, w2, bias, x, mod, gain, rw_parts, rb)


ROW_CHUNKS = D_MODEL // LANES
DEINT = 512
DUMP_ROWS = 512


GATHER_BUFS = 3


def _expert_kernel(be_ref, nv_ref, gsrc0_ref, gsrc1_ref, gsrc2_ref, gdstp_ref, gdst0_ref, h_ref, w1_ref, b1g_ref,
                   b1l_ref, w2_ref, b2_ref, pe_ref, po_ref, yt_ref, xbuf, ybuf, w1g_sc, w1l_sc, w2_sc, gsem, ssem,
                   *, n_blocks, dump_row):
    i = pl.program_id(0)
    cur = i % 2
    nxt = 1 - cur
    gcur = i % GATHER_BUFS
    gnew = (i + 2) % GATHER_BUFS
    rows = MOE_BLOCK

    def tile_rows(first):
        return pl.ds(pl.multiple_of(first, ROW_CHUNKS), ROW_CHUNKS)

    def buf_rows(r):
        return pl.ds(r * ROW_CHUNKS, ROW_CHUNKS) if isinstance(r, int) else tile_rows(r * ROW_CHUNKS)

    def gather(idx_ref, slot, r):
        return pltpu.make_async_copy(h_ref.at[tile_rows(idx_ref[0, 0, r])], xbuf.at[slot, buf_rows(r)], gsem.at[slot])

    def scatter(dst_first, slot, r):
        return pltpu.make_async_copy(ybuf.at[slot, buf_rows(r)], yt_ref.at[tile_rows(dst_first)], ssem.at[slot])

    def wait_gather(slot):
        pltpu.make_async_copy(h_ref.at[pl.ds(0, rows * ROW_CHUNKS)], xbuf.at[slot], gsem.at[slot]).wait()

    def wait_scatter(slot):
        pltpu.make_async_copy(ybuf.at[slot], yt_ref.at[pl.ds(0, rows * ROW_CHUNKS)], ssem.at[slot]).wait()

    @pl.when(i == 0)
    def _():
        ybuf[...] = jnp.zeros(ybuf.shape, F32)

        def prime(r, c):
            gather(gsrc0_ref, 0, r).start()
            gather(gsrc1_ref, 1, r).start()
            return c

        lax.fori_loop(0, rows, prime, 0)

    wait_gather(gcur)

    @pl.when(i >= 1)
    def _():
        wait_scatter(cur)

    first = i == 0
    changed = jnp.logical_or(first, be_ref[i] != be_ref[jnp.maximum(i - 1, 0)])

    @pl.when(changed)
    def _():
        for c in range(w1_ref.shape[3] // DEINT):
            wc = w1_ref[0, 0, :, c * DEINT:(c + 1) * DEINT].astype(BF16)
            cs = slice(c * (DEINT // 2), (c + 1) * (DEINT // 2))
            w1g_sc[:, cs] = jnp.dot(wc, pe_ref[...], preferred_element_type=F32).astype(BF16)
            w1l_sc[:, cs] = jnp.dot(wc, po_ref[...], preferred_element_type=F32).astype(BF16)
        w2_sc[...] = w2_ref[0, 0].astype(BF16)

    for r in range(rows):
        gather(gsrc2_ref, gnew, r).start(priority=r % 2)
        scatter(jnp.where(first, dump_row + r * ROW_CHUNKS, gdstp_ref[0, 0, r]), nxt, r).start(priority=(r + 1) % 2)

    def chunk(c):
        return pl.ds(c, rows, stride=ROW_CHUNKS)

    x = jnp.concatenate([xbuf[gcur, chunk(c), :] for c in range(ROW_CHUNKS)], axis=1)
    row = lax.broadcasted_iota(I32, x.shape, 0)
    xb = jnp.where(row < nv_ref[i], x, 0.0).astype(BF16)
    glu = jnp.dot(xb, w1g_sc[...], preferred_element_type=F32) + b1g_ref[0]
    lin = jnp.dot(xb, w1l_sc[...], preferred_element_type=F32) + b1l_ref[0]
    glu = jnp.minimum(glu, SWIGLU_LIMIT)
    lin = jnp.clip(lin, -SWIGLU_LIMIT, SWIGLU_LIMIT)
    act = glu * _sigmoid(SWIGLU_ALPHA * glu) * (lin + 1.0)
    y = jnp.dot(act.astype(BF16), w2_sc[...], preferred_element_type=F32) + b2_ref[0]
    for c in range(ROW_CHUNKS):
        ybuf[cur, chunk(c), :] = y[:, c * LANES:(c + 1) * LANES]

    @pl.when(i == n_blocks - 1)
    def _():
        def last(r, c):
            scatter(gdst0_ref[0, 0, r], cur, r).start()
            return c

        lax.fori_loop(0, rows, last, 0)
        wait_gather((i + 1) % GATHER_BUFS)
        wait_gather(gnew)
        wait_scatter(nxt)
        wait_scatter(cur)


def _experts(h, gsrc, gdst, block_expert, n_valid, layer, w1, b1g, b1l, w2, b2):
    t = h.shape[0] // ROW_CHUNKS
    n_blocks = gsrc.shape[0]
    d, f2 = w1.shape[2:]
    f = f2 // 2
    sel = np.arange(DEINT)[:, None] - 2 * np.arange(DEINT // 2)[None, :]
    p_even = jnp.asarray((sel == 0).astype(np.float32), dtype=BF16)
    p_odd = jnp.asarray((sel == 1).astype(np.float32), dtype=BF16)
    last = n_blocks - 1

    def wsel(i, be, nv):
        return (be[i], 0, 0)

    def wsel_layer(i, be, nv):
        return (layer, be[i], 0, 0)

    const = lambda i, be, nv: (0, 0)
    idx_blk = (1, 1, MOE_BLOCK)
    kern = functools.partial(_expert_kernel, n_blocks=n_blocks, dump_row=TOP_K * t * ROW_CHUNKS)
    return pl.pallas_call(
        kern,
        out_shape=jax.ShapeDtypeStruct(((TOP_K * t + DUMP_ROWS) * ROW_CHUNKS, LANES), F32),
        grid_spec=pltpu.PrefetchScalarGridSpec(
            num_scalar_prefetch=2,
            grid=(n_blocks,),
            in_specs=[
                pl.BlockSpec(idx_blk, lambda i, be, nv: (i, 0, 0), memory_space=pltpu.SMEM),
                pl.BlockSpec(idx_blk, lambda i, be, nv: (jnp.minimum(i + 1, last), 0, 0), memory_space=pltpu.SMEM),
                pl.BlockSpec(idx_blk, lambda i, be, nv: (jnp.minimum(i + 2, last), 0, 0), memory_space=pltpu.SMEM),
                pl.BlockSpec(idx_blk, lambda i, be, nv: (jnp.maximum(i - 1, 0), 0, 0), memory_space=pltpu.SMEM),
                pl.BlockSpec(idx_blk, lambda i, be, nv: (i, 0, 0), memory_space=pltpu.SMEM),
                pl.BlockSpec(memory_space=pl.ANY),
                pl.BlockSpec((1, 1, d, f2), wsel_layer),
                pl.BlockSpec((1, 1, f), wsel),
                pl.BlockSpec((1, 1, f), wsel),
                pl.BlockSpec((1, 1, f, d), wsel_layer),
                pl.BlockSpec((1, 1, d), wsel),
                pl.BlockSpec((DEINT, DEINT // 2), const),
                pl.BlockSpec((DEINT, DEINT // 2), const),
            ],
            out_specs=pl.BlockSpec(memory_space=pl.ANY),
            scratch_shapes=[pltpu.VMEM((GATHER_BUFS, MOE_BLOCK * ROW_CHUNKS, LANES), F32),
                            pltpu.VMEM((2, MOE_BLOCK * ROW_CHUNKS, LANES), F32),
                            pltpu.VMEM((d, f), BF16), pltpu.VMEM((d, f), BF16), pltpu.VMEM((f, d), BF16),
                            pltpu.SemaphoreType.DMA((GATHER_BUFS,)), pltpu.SemaphoreType.DMA((2,))],
        ),
        compiler_params=_cparams(("arbitrary",)),
        name="moe_experts",
    )(block_expert, n_valid, gsrc, gsrc, gsrc, gdst, gdst, h, w1, b1g, b1l, w2, b2, p_even, p_odd)


def _combine_kernel(y0_ref, y1_ref, y2_ref, y3_ref, wt_ref, x_ref, mod_ref, o_ref):
    wt = wt_ref[0]
    g2 = mod_ref[0][5:6]
    y_refs = (y0_ref, y1_ref, y2_ref, y3_ref)
    wk = [wt[:, k:k + 1] for k in range(TOP_K)]
    tm = wt.shape[0]
    for c in range(ROW_CHUNKS):
        chunk = pl.ds(c, tm, stride=ROW_CHUNKS)
        y = wk[0] * y_refs[0][chunk, :]
        for k in range(1, TOP_K):
            y = y + wk[k] * y_refs[k][chunk, :]
        cs = slice(c * LANES, (c + 1) * LANES)
        o_ref[0, :, cs] = x_ref[0, :, cs] + g2[:, cs] * y


def _combine(yt, wts, x, mod, tm):
    b, s, d = x.shape
    nt = s // tm
    tiles = b * nt
    tok = lambda i, j: (i, j, 0)

    def yspec(k):
        return pl.BlockSpec((tm * ROW_CHUNKS, LANES), lambda i, j: (k * tiles + i * nt + j, 0))

    return pl.pallas_call(
        _combine_kernel,
        out_shape=jax.ShapeDtypeStruct((b, s, d), F32),
        grid=(b, nt),
        in_specs=[yspec(k) for k in range(TOP_K)] + [
            pl.BlockSpec((1, tm, LANES), tok),
            pl.BlockSpec((1, tm, d), tok),
            pl.BlockSpec((1, 6, d), lambda i, j: (i, 0, 0)),
        ],
        out_specs=pl.BlockSpec((1, tm, d), tok),
        compiler_params=_cparams(("arbitrary", "arbitrary")),
        name="moe_combine",
    )(yt, yt, yt, yt, wts, x, mod)


def _route_slots(top_idx, n_blocks):
    t = top_idx.shape[0]
    n_assign = t * TOP_K
    n_slots = n_blocks * MOE_BLOCK
    sorted_e, order = lax.sort_key_val(top_idx.reshape(-1), jnp.arange(n_assign, dtype=I32))
    experts = jnp.arange(N_EXPERTS + 1, dtype=I32)
    start = jnp.sum((sorted_e[None, :] < experts[:, None]).astype(I32), axis=1)
    counts = start[1:] - start[:-1]
    padded = (counts + MOE_BLOCK - 1) // MOE_BLOCK * MOE_BLOCK
    pad_end = jnp.cumsum(padded)
    pad_start = pad_end - padded
    blk_start = jnp.arange(n_blocks, dtype=I32) * MOE_BLOCK
    block_expert = jnp.minimum(jnp.sum((pad_end[None, :] <= blk_start[:, None]).astype(I32), axis=1), N_EXPERTS - 1)
    n_valid = jnp.clip(pad_start[block_expert] + counts[block_expert] - blk_start, 0, MOE_BLOCK).astype(I32)
    within = jnp.arange(MOE_BLOCK, dtype=I32)[None, :]
    valid = within < n_valid[:, None]
    rank = (blk_start - pad_start[block_expert] + start[block_expert])[:, None] + within
    slot_a = order[jnp.clip(rank, 0, n_assign - 1)]
    slot = blk_start[:, None] + within
    gsrc = jnp.where(valid, slot_a // TOP_K, 0) * ROW_CHUNKS
    gdst = jnp.where(valid, (slot_a % TOP_K) * t + slot_a // TOP_K, n_assign + slot % DUMP_ROWS) * ROW_CHUNKS
    shape = (n_blocks, 1, MOE_BLOCK)
    return gsrc.reshape(shape), gdst.reshape(shape), block_expert, n_valid


def _moe(x_mid, h, idx, wts, mod, layer, w1, b1, w2, b2, tm):
    b, s, d = x_mid.shape
    t = b * s
    n_blocks = -(-t * TOP_K // MOE_BLOCK) + N_EXPERTS
    top_idx = idx.reshape(t, LANES)[:, :TOP_K]
    gsrc, gdst, block_expert, n_valid = _route_slots(top_idx, n_blocks)
    yt = _experts(h.reshape(t * ROW_CHUNKS, LANES), gsrc, gdst, block_expert, n_valid, layer, w1,
                  b1[:, None, 0::2], b1[:, None, 1::2], w2, b2[:, None, :])
    return _combine(yt, wts, x_mid, mod, tm)


def _pad_cols(w, n):
    return jnp.pad(w, ((0, 0), (0, n - w.shape[-1])))


def _attn_colgain(diff_qk_gain, nsa_q_gain, nsa_k_gain):
    scale = HEAD_DIM ** -0.5 * LOG2E
    ones = jnp.ones((LANES,), F32)
    parts = [jnp.tile(diff_qk_gain[0] * scale, 8), jnp.tile(diff_qk_gain[1], 8), jnp.ones((512,), F32),
             jnp.tile(nsa_q_gain * scale, 8), ones, ones, jnp.tile(nsa_k_gain[1], 2), ones,
             jnp.tile(nsa_k_gain[2], 2), ones, ones]
    return jnp.concatenate(parts).reshape(1, W_IN_PAD)


def _overlap_matrix(s):
    n_sel = s // SEL_BLOCK
    ncp = s // CMP_STRIDE
    c_start = np.arange(ncp) * CMP_STRIDE
    s_start = np.arange(n_sel) * SEL_BLOCK
    ovl = (c_start[None, :] < s_start[:, None] + SEL_BLOCK) & (c_start[None, :] + CMP_BLOCK > s_start[:, None])
    ovl[:, ncp - 1] = False
    return jnp.asarray(ovl.astype(np.float32), dtype=BF16)


def _attention_layer(x, mod, gain, w_in, w_out, diff_qk_gain, diff_lambda, diff_subln, nsa_q_gain, nsa_k_gain,
                     cmp_pos, cmp_w1, cmp_w2, lam_init):
    b, s, d = x.shape
    n_gate = NSA_Q_HEADS * 3
    gate_perm = np.arange(n_gate).reshape(NSA_KV_HEADS, NSA_GROUP, 3).transpose(0, 2, 1).reshape(-1)
    w_pad = jnp.concatenate([w_in[:, :COL_GATE], w_in[:, COL_GATE + gate_perm],
                             jnp.zeros((d, W_IN_PAD - COL_GATE - n_gate), w_in.dtype)], axis=1).astype(BF16)
    zb, dqT, dvT, nqT, vselT, vwinT, gT, chunks = _attn_in_proj(
        x, mod, gain, w_pad, _attn_colgain(diff_qk_gain, nsa_q_gain, nsa_k_gain))

    o_diff = _diff_attention(zb, dqT.reshape(b, DIFF_HEADS, 2, HEAD_DIM, s), dvT,
                             _score_bound(diff_qk_gain[0], diff_qk_gain[1]), diff_lambda, diff_subln, lam_init,
                             DIFF_TQ, ATTN_TM)

    ncp = s // CMP_STRIDE
    cmp_out = _compress(chunks, cmp_pos.reshape(2, 1, CMP_BLOCK * HEAD_DIM), cmp_w1, cmp_w2, nsa_k_gain[0:1])
    kc = cmp_out[:, 0:2].transpose(0, 2, 1, 3).reshape(b, ncp, LANES).astype(BF16)
    vcT = cmp_out[:, 2:4].transpose(0, 1, 3, 2).astype(BF16)
    gatesT = gT[:, :n_gate].reshape(b, NSA_KV_HEADS, 3, NSA_GROUP, s)
    shifts = jnp.stack([_score_bound(nsa_q_gain, nsa_k_gain[1]), _score_bound(nsa_q_gain, nsa_k_gain[2])])
    o_nsa = _nsa_attention(zb, nqT.reshape(b, NSA_KV_HEADS, NSA_GROUP, HEAD_DIM, s), vselT, vwinT, kc, vcT,
                           _overlap_matrix(s), gatesT, shifts, NSA_TILE)
    return o_diff, o_nsa


def kernel(x, c, mod_w, mod_b, norm_mix, norm_ffn, attn_w_in, attn_w_out, diff_qk_gain, diff_lambda, diff_subln,
           nsa_q_gain, nsa_k_gain, nsa_cmp_pos, nsa_cmp_w1, nsa_cmp_w2, conv_pw1_w, conv_pw1_b, conv_dw_w,
           conv_dw_b, conv_ln_g, conv_ln_b, conv_pw2_w, conv_pw2_b, router_w, router_b, moe_w1, moe_b1, moe_w2,
           moe_b2):
    b, s, d = x.shape
    depth = mod_w.shape[0]
    tm = 512 if s % 512 == 0 else 256
    mods = _modulation(c, mod_w, mod_b).reshape(depth, b, 6, d)
    half = d // 2
    for i in range(depth):
        mod = mods[i]
        j = i // 2
        if i % 2 == 0:
            lam_init = 0.8 - 0.6 * math.exp(-0.3 * i)
            o_diff, o_nsa = _attention_layer(
                x, mod, norm_mix[i:i + 1], attn_w_in[j], attn_w_out[j], diff_qk_gain[j], diff_lambda[j],
                diff_subln[j], nsa_q_gain[j], nsa_k_gain[j], nsa_cmp_pos[j], nsa_cmp_w1[j], nsa_cmp_w2[j], lam_init)
            a1, a2, c1, c2 = o_diff, o_nsa, 0, 0
            w_o = attn_w_out[j].astype(BF16)
            bias = jnp.zeros((1, d), F32)
        else:
            u = _conv_in_proj(x, mod, norm_mix[i:i + 1], conv_pw1_w[j].astype(BF16), conv_pw1_b[j:j + 1], tm)
            v = _dwconv_ln_swish(u, conv_dw_w[j].reshape(CONV_WIDTH, d), conv_dw_b[j:j + 1], conv_ln_g[j:j + 1],
                                 conv_ln_b[j:j + 1], 256)
            a1, a2, c1, c2 = v, v, 0, 1
            w_o = conv_pw2_w[j].astype(BF16)
            bias = conv_pw2_b[j:j + 1]
        rw = _pad_cols(router_w[i], LANES)
        rw_hi = rw.astype(BF16)
        rw_lo = (rw - rw_hi.astype(F32)).astype(BF16)
        rb = jnp.concatenate([router_b[i], jnp.full((LANES - N_EXPERTS,), NEG, F32)]).reshape(1, LANES)
        x_mid, h, idx, wts = _out_proj_router(a1, a2, c1, c2, w_o[:half], w_o[half:], bias, x, mod,
                                              norm_ffn[i:i + 1], jnp.concatenate([rw_hi, rw_lo], axis=1), rb, tm)
        x = _moe(x_mid, h, idx, wts, mod, i, moe_w1, moe_b1[i], moe_w2, moe_b2[i], 256)
    return x
```
